```python
import numpy as np
import jax
import jax.numpy as jnp
from jax import lax


D_MODEL = 1024
BATCH = 4
SEQ = 4096
DEPTH = 2

MIX_WIDTH = D_MODEL
NSA_WIDTH = MIX_WIDTH // 2
MLSTM_WIDTH = MIX_WIDTH - NSA_WIDTH
NSA_HEADS = 8
NSA_KV_HEADS = 2
NSA_HEAD_DIM = NSA_WIDTH // NSA_HEADS
NSA_BRANCHES = 3
CMP_LEN = 32
CMP_STRIDE = 16
CMP_HIDDEN = 128
SEL_BLOCK = 64
SEL_TOP = 16
WINDOW = 512
Q_BLOCK = 128
FORCE_BONUS = 1e4
NEG_INF = -1e30
MLSTM_HEADS = 4
MLSTM_HEAD_DIM = MLSTM_WIDTH // MLSTM_HEADS
MLSTM_CHUNK = 64
CONV_WIDTH = 4
D_FF = 2816
N_EXPERTS = 8
TOP_K = 2
D_FF_EXPERT = 3584
MOE_BLOCK = 256
EPS = 1e-6
N_DENSE_LAYERS = (DEPTH + 1) // 2
N_MOE_LAYERS = DEPTH // 2
NSA_KV_COLS = 2 * NSA_BRANCHES * NSA_KV_HEADS * NSA_HEAD_DIM
IN_SIZES = (NSA_WIDTH, NSA_KV_COLS, NSA_HEADS * NSA_BRANCHES, 2 * MLSTM_WIDTH, MLSTM_WIDTH, MLSTM_WIDTH, MLSTM_HEADS, MLSTM_HEADS)
IN_COLS = sum(IN_SIZES)

kernel_name = 'hybrid_nsa_mlstm_moe_block'


def _split_points():
    return [int(v) for v in np.cumsum(IN_SIZES)[:-1]]


def rmsnorm(x, g):
    x32 = x.astype(jnp.float32)
    y = x32 * lax.rsqrt(jnp.mean(x32 * x32, axis=-1, keepdims=True) + EPS)
    return (y * g.astype(jnp.float32)).astype(x.dtype)


def alibi_slopes(n):
    return jnp.asarray(2.0 ** (-8.0 * np.arange(1, n + 1) / n), dtype=jnp.float32)


def overlap_matrix(n_cmp, n_sel):
    lo_c = np.arange(n_cmp)[:, None] * CMP_STRIDE
    lo_s = np.arange(n_sel)[None, :] * SEL_BLOCK
    ov = np.minimum(lo_c + CMP_LEN, lo_s + SEL_BLOCK) - np.maximum(lo_c, lo_s)
    return jnp.asarray(np.clip(ov, 0, None) / CMP_LEN, dtype=jnp.float32)


def masked_softmax(s, mask):
    return jax.nn.softmax(jnp.where(mask, s, NEG_INF), axis=-1)


def compress_blocks(kv, pe, w1, w2):
    B, S, G, dh = kv.shape
    n_sub = CMP_LEN // CMP_STRIDE
    ch = kv.reshape(B, S // CMP_STRIDE, CMP_STRIDE, G, dh)
    n_cmp = ch.shape[1] - n_sub + 1
    blocks = jnp.concatenate([ch[:, i:i + n_cmp] for i in range(n_sub)], axis=2)
    blocks = blocks + pe[None, None, :, None, :]
    flat = blocks.transpose(0, 1, 3, 2, 4).reshape(B, n_cmp, G, CMP_LEN * dh)
    return jax.nn.silu(flat @ w1) @ w2


def nsa_attention(q, kv, gate_logits, cmp_pe, cmp_w1, cmp_w2):
    B, S, H, dh = q.shape
    G = NSA_KV_HEADS
    R = H // G
    f32 = jnp.float32
    k_c = compress_blocks(kv[:, :, 0], cmp_pe[0], cmp_w1[0], cmp_w2[0])
    v_c = compress_blocks(kv[:, :, 1], cmp_pe[1], cmp_w1[1], cmp_w2[1])
    n_cmp = k_c.shape[1]
    cmp_end = jnp.arange(n_cmp) * CMP_STRIDE + CMP_LEN - 1
    n_sel = S // SEL_BLOCK
    n_top = min(SEL_TOP, n_sel)
    k_s = kv[:, :, 2].reshape(B, n_sel, SEL_BLOCK, G, dh).transpose(0, 3, 1, 2, 4)
    v_s = kv[:, :, 3].reshape(B, n_sel, SEL_BLOCK, G, dh).transpose(0, 3, 1, 2, 4)
    pad = ((0, 0), (WINDOW, 0), (0, 0), (0, 0))
    k_w = jnp.pad(kv[:, :, 4], pad)
    v_w = jnp.pad(kv[:, :, 5], pad)
    slopes = alibi_slopes(H).reshape(G, R)
    ov = overlap_matrix(n_cmp, n_sel)
    gates = jax.nn.sigmoid(gate_logits.astype(f32)).reshape(B, S, G, R, NSA_BRANCHES)
    n_qb = S // Q_BLOCK
    q_blocks = (q * (dh ** -0.5)).reshape(B, n_qb, Q_BLOCK, G, R, dh).swapaxes(0, 1)
    g_blocks = gates.reshape(B, n_qb, Q_BLOCK, G, R, NSA_BRANCHES).swapaxes(0, 1)
    bi = jnp.arange(B)[:, None, None, None]
    gi = jnp.arange(G)[None, :, None, None]
    j_sel = jnp.arange(n_sel)

    def block_fn(args):
        qb, gb, blk = args
        t = blk * Q_BLOCK + jnp.arange(Q_BLOCK)
        s_c = jnp.einsum('bqgrd,bngd->bgrqn', qb, k_c).astype(f32)
        dist_c = (t[:, None] - cmp_end[None, :]).astype(f32)
        s_c = s_c - slopes[:, :, None, None] * dist_c
        mask_c = cmp_end[None, :] <= t[:, None]
        p_c = masked_softmax(s_c, mask_c) * jnp.any(mask_c, axis=-1)[:, None]
        o_c = jnp.einsum('bgrqn,bngd->bqgrd', p_c.astype(v_c.dtype), v_c)
        imp = jnp.einsum('bgrqn,nj->bgqj', p_c, ov)
        cur = t // SEL_BLOCK
        valid_j = j_sel[None, :] <= cur[:, None]
        forced = (j_sel[None, :] == 0) | (j_sel[None, :] == cur[:, None]) | (j_sel[None, :] == cur[:, None] - 1)
        imp = jnp.where(forced, imp + FORCE_BONUS, imp)
        imp = jnp.where(valid_j, imp, -1.0)
        _, idx = lax.top_k(imp, n_top)
        kg = k_s[bi, gi, idx]
        vg = v_s[bi, gi, idx]
        s_s = jnp.einsum('bqgrd,bgqnkd->bgrqnk', qb, kg).astype(f32)
        pos_s = idx[..., None] * SEL_BLOCK + jnp.arange(SEL_BLOCK)
        dist_s = t[None, None, :, None, None] - pos_s
        s_s = s_s - slopes[None, :, :, None, None, None] * dist_s[:, :, None].astype(f32)
        mask_s = (dist_s >= 0)[:, :, None]
        p_s = masked_softmax(s_s.reshape(B, G, R, Q_BLOCK, n_top * SEL_BLOCK),
                             mask_s.reshape(B, G, 1, Q_BLOCK, n_top * SEL_BLOCK)).reshape(s_s.shape)
        o_s = jnp.einsum('bgrqnk,bgqnkd->bqgrd', p_s.astype(vg.dtype), vg)
        k_blk = lax.dynamic_slice_in_dim(k_w, blk * Q_BLOCK, Q_BLOCK + WINDOW, axis=1)
        v_blk = lax.dynamic_slice_in_dim(v_w, blk * Q_BLOCK, Q_BLOCK + WINDOW, axis=1)
        pos_w = blk * Q_BLOCK - WINDOW + jnp.arange(Q_BLOCK + WINDOW)
        dist_w = t[:, None] - pos_w[None, :]
        mask_w = (dist_w >= 0) & (dist_w < WINDOW) & (pos_w[None, :] >= 0)
        s_w = jnp.einsum('bqgrd,bkgd->bgrqk', qb, k_blk).astype(f32) - slopes[:, :, None, None] * dist_w.astype(f32)
        p_w = masked_softmax(s_w, mask_w)
        o_w = jnp.einsum('bgrqk,bkgd->bqgrd', p_w.astype(v_blk.dtype), v_blk)
        out = gb[..., 0:1] * o_c + gb[..., 1:2] * o_s + gb[..., 2:3] * o_w
        return out.astype(q.dtype)

    out = lax.map(block_fn, (q_blocks, g_blocks, jnp.arange(n_qb)))
    return out.swapaxes(0, 1).reshape(B, S, H * dh)


def causal_conv(x, w, b):
    K = w.shape[0]
    S = x.shape[1]
    xp = jnp.pad(x, ((0, 0), (K - 1, 0), (0, 0)))
    y = b
    for i in range(K):
        y = y + xp[:, i:i + S] * w[i]
    return y


def mlstm(q, k, v, o_pre, i_pre, f_pre, norm_g):
    B, S, _ = q.shape
    H, d, L = MLSTM_HEADS, MLSTM_HEAD_DIM, MLSTM_CHUNK
    nc = S // L
    f32 = jnp.float32

    def to_chunks(a):
        return a.astype(f32).reshape(B, nc, L, H, d).transpose(1, 0, 3, 2, 4)

    def gate_chunks(a):
        return a.astype(f32).reshape(B, nc, L, H).transpose(1, 0, 3, 2)

    qc = to_chunks(q)
    kc = to_chunks(k) * (d ** -0.5)
    vc = to_chunks(v)
    lf = jax.nn.log_sigmoid(gate_chunks(f_pre))
    ig = gate_chunks(i_pre)
    causal = jnp.tril(jnp.ones((L, L), dtype=bool))

    def step(carry, inp):
        C, n, m = carry
        qq, kk, vv, lf_c, ig_c = inp
        a = jnp.cumsum(lf_c, axis=-1)
        log_d = a[..., :, None] - a[..., None, :] + ig_c[..., None, :]
        log_d = jnp.where(causal, log_d, -jnp.inf)
        m_inter = a + m[..., None]
        m_t = jnp.maximum(m_inter, jnp.max(log_d, axis=-1))
        d_mat = jnp.exp(log_d - m_t[..., None])
        inter = jnp.exp(m_inter - m_t)
        s_qk = jnp.einsum('bhjd,bhsd->bhjs', qq, kk) * d_mat
        num = inter[..., None] * jnp.einsum('bhed,bhjd->bhje', C, qq) + jnp.einsum('bhjs,bhse->bhje', s_qk, vv)
        den = inter * jnp.einsum('bhd,bhjd->bhj', n, qq) + jnp.sum(s_qk, axis=-1)
        h = num / jnp.maximum(jnp.abs(den), jnp.exp(-m_t))[..., None]
        a_last = a[..., -1]
        log_w = a_last[..., None] - a + ig_c
        m_new = jnp.maximum(a_last + m, jnp.max(log_w, axis=-1))
        w = jnp.exp(log_w - m_new[..., None])
        decay = jnp.exp(a_last + m - m_new)
        C_new = decay[..., None, None] * C + jnp.einsum('bhse,bhsd->bhed', w[..., None] * vv, kk)
        n_new = decay[..., None] * n + jnp.einsum('bhs,bhsd->bhd', w, kk)
        return (C_new, n_new, m_new), h

    init = (jnp.zeros((B, H, d, d), f32), jnp.zeros((B, H, d), f32), jnp.zeros((B, H), f32))
    _, h = lax.scan(step, init, (qc, kc, vc, lf, ig))
    h = h.transpose(1, 0, 3, 2, 4).reshape(B, S, H * d)
    h = jax.nn.sigmoid(o_pre.astype(f32)) * h
    hh = h.reshape(B, S, H, d)
    hh = hh * lax.rsqrt(jnp.mean(hh * hh, axis=-1, keepdims=True) + EPS)
    return (hh.reshape(B, S, H * d) * norm_g.astype(f32)).astype(q.dtype)


def hybrid_mixer(h, w_in, b_in, cmp_pe, cmp_w1, cmp_w2, conv_w, conv_b, mlstm_norm_g, w_out):
    B, S, _ = h.shape
    proj = h @ w_in + b_in
    q_n, kv_n, gate_n, qk_m, v_m, o_m, i_m, f_m = jnp.split(proj, _split_points(), axis=-1)
    y_nsa = nsa_attention(q_n.reshape(B, S, NSA_HEADS, NSA_HEAD_DIM),
                          kv_n.reshape(B, S, 2 * NSA_BRANCHES, NSA_KV_HEADS, NSA_HEAD_DIM),
                          gate_n, cmp_pe, cmp_w1, cmp_w2)
    qk_m = jax.nn.silu(causal_conv(qk_m, conv_w, conv_b))
    q_m, k_m = jnp.split(qk_m, 2, axis=-1)
    y_ml = mlstm(q_m, k_m, v_m, o_m, i_m, f_m, mlstm_norm_g)
    return jnp.concatenate([y_nsa, y_ml], axis=-1) @ w_out


def swiglu(h, wg, wu, wd):
    return (jax.nn.silu(h @ wg) * (h @ wu)) @ wd


def moe_swiglu(h, router_w, w_gate, w_up, w_down):
    B, S, D = h.shape
    N = B * S
    xf = h.reshape(N, D)
    logits = (xf @ router_w).astype(jnp.float32)
    top_val, top_idx = lax.top_k(logits, TOP_K)
    weights = jax.nn.softmax(top_val, axis=-1)
    e_flat = top_idx.reshape(-1)
    tok_flat = jnp.repeat(jnp.arange(N), TOP_K)
    w_flat = weights.reshape(-1)
    order = jnp.argsort(e_flat)
    e_sorted = e_flat[order]
    counts = jnp.bincount(e_flat, length=N_EXPERTS)
    starts = jnp.cumsum(counts) - counts
    padded = (counts + MOE_BLOCK - 1) // MOE_BLOCK * MOE_BLOCK
    pad_ends = jnp.cumsum(padded)
    pad_starts = pad_ends - padded
    dest = pad_starts[e_sorted] + (jnp.arange(N * TOP_K) - starts[e_sorted])
    n_rows = ((N * TOP_K + MOE_BLOCK - 1) // MOE_BLOCK + N_EXPERTS) * MOE_BLOCK
    row_tok = jnp.zeros((n_rows,), jnp.int32).at[dest].set(tok_flat[order])
    row_w = jnp.zeros((n_rows,), jnp.float32).at[dest].set(w_flat[order])
    n_blocks = n_rows // MOE_BLOCK
    blk_expert = jnp.minimum(jnp.searchsorted(pad_ends, jnp.arange(n_blocks) * MOE_BLOCK, side='right'), N_EXPERTS - 1)

    def expert_block(args):
        toks, e = args
        xb = xf[toks]
        return (jax.nn.silu(xb @ w_gate[e]) * (xb @ w_up[e])) @ w_down[e]

    y = lax.map(expert_block, (row_tok.reshape(n_blocks, MOE_BLOCK), blk_expert))
    y = y.reshape(n_rows, D) * row_w[:, None].astype(y.dtype)
    out = jnp.zeros((N, D), h.dtype).at[row_tok].add(y)
    return out.reshape(B, S, D)


def setup_inputs(seed: int = 0) -> dict:
    key = jax.random.key(seed)
    ks = jax.random.split(key, 24)
    f32 = jnp.float32

    def nrm(k, shape, scale):
        return jax.random.normal(k, shape, f32) * scale

    x = nrm(ks[0], (BATCH, SEQ, D_MODEL), 1.0)
    c = nrm(ks[1], (BATCH, D_MODEL), 1.0)
    ada_w = nrm(ks[2], (DEPTH, D_MODEL, 6 * D_MODEL), 0.5 * D_MODEL ** -0.5)
    ada_b = nrm(ks[3], (DEPTH, 6 * D_MODEL), 0.02)
    norm_mix_g = 1.0 + nrm(ks[4], (DEPTH, D_MODEL), 0.05)
    norm_ffn_g = 1.0 + nrm(ks[5], (DEPTH, D_MODEL), 0.05)
    w_in = nrm(ks[6], (DEPTH, D_MODEL, IN_COLS), D_MODEL ** -0.5)
    b_in = nrm(ks[7], (DEPTH, IN_COLS), 0.02)
    b_in = b_in.at[:, -MLSTM_HEADS:].add(jnp.linspace(3.0, 6.0, MLSTM_HEADS, dtype=f32))
    cmp_pe = nrm(ks[8], (DEPTH, 2, CMP_LEN, NSA_HEAD_DIM), 0.02)
    cmp_w1 = nrm(ks[9], (DEPTH, 2, CMP_LEN * NSA_HEAD_DIM, CMP_HIDDEN), (CMP_LEN * NSA_HEAD_DIM) ** -0.5)
    cmp_w2 = nrm(ks[10], (DEPTH, 2, CMP_HIDDEN, NSA_HEAD_DIM), CMP_HIDDEN ** -0.5)
    conv_w = nrm(ks[11], (DEPTH, CONV_WIDTH, 2 * MLSTM_WIDTH), CONV_WIDTH ** -0.5)
    conv_b = nrm(ks[12], (DEPTH, 2 * MLSTM_WIDTH), 0.02)
    mlstm_norm_g = 1.0 + nrm(ks[13], (DEPTH, MLSTM_WIDTH), 0.05)
    w_out = nrm(ks[14], (DEPTH, MIX_WIDTH, D_MODEL), MIX_WIDTH ** -0.5)
    ffn_w_gate = nrm(ks[15], (N_DENSE_LAYERS, D_MODEL, D_FF), D_MODEL ** -0.5)
    ffn_w_up = nrm(ks[16], (N_DENSE_LAYERS, D_MODEL, D_FF), D_MODEL ** -0.5)
    ffn_w_down = nrm(ks[17], (N_DENSE_LAYERS, D_FF, D_MODEL), D_FF ** -0.5)
    router_w = nrm(ks[18], (N_MOE_LAYERS, D_MODEL, N_EXPERTS), D_MODEL ** -0.5)
    moe_w_gate = nrm(ks[19], (N_MOE_LAYERS, N_EXPERTS, D_MODEL, D_FF_EXPERT), D_MODEL ** -0.5)
    moe_w_up = nrm(ks[20], (N_MOE_LAYERS, N_EXPERTS, D_MODEL, D_FF_EXPERT), D_MODEL ** -0.5)
    moe_w_down = nrm(ks[21], (N_MOE_LAYERS, N_EXPERTS, D_FF_EXPERT, D_MODEL), D_FF_EXPERT ** -0.5)
    final_norm_g = 1.0 + nrm(ks[22], (D_MODEL,), 0.05)
    return {'x': x, 'c': c, 'ada_w': ada_w, 'ada_b': ada_b, 'norm_mix_g': norm_mix_g,
            'norm_ffn_g': norm_ffn_g, 'w_in': w_in, 'b_in': b_in, 'cmp_pe': cmp_pe,
            'cmp_w1': cmp_w1, 'cmp_w2': cmp_w2, 'conv_w': conv_w, 'conv_b': conv_b,
            'mlstm_norm_g': mlstm_norm_g, 'w_out': w_out, 'ffn_w_gate': ffn_w_gate,
            'ffn_w_up': ffn_w_up, 'ffn_w_down': ffn_w_down, 'router_w': router_w,
            'moe_w_gate': moe_w_gate, 'moe_w_up': moe_w_up, 'moe_w_down': moe_w_down,
            'final_norm_g': final_norm_g}


def reference(x, c, ada_w, ada_b, norm_mix_g, norm_ffn_g, w_in, b_in, cmp_pe, cmp_w1, cmp_w2,
              conv_w, conv_b, mlstm_norm_g, w_out, ffn_w_gate, ffn_w_up, ffn_w_down, router_w,
              moe_w_gate, moe_w_up, moe_w_down, final_norm_g):
    c_act = jax.nn.silu(c)
    for l in range(DEPTH):
        mod = (c_act @ ada_w[l] + ada_b[l])[:, None, :]
        sh1, sc1, g1, sh2, sc2, g2 = jnp.split(mod, 6, axis=-1)
        h = rmsnorm(x, norm_mix_g[l]) * (1.0 + sc1) + sh1
        x = x + g1 * hybrid_mixer(h, w_in[l], b_in[l], cmp_pe[l], cmp_w1[l], cmp_w2[l],
                                  conv_w[l], conv_b[l], mlstm_norm_g[l], w_out[l])
        h = rmsnorm(x, norm_ffn_g[l]) * (1.0 + sc2) + sh2
        if l % 2 == 0:
            i = l // 2
            y = swiglu(h, ffn_w_gate[i], ffn_w_up[i], ffn_w_down[i])
        else:
            i = l // 2
            y = moe_swiglu(h, router_w[i], moe_w_gate[i], moe_w_up[i], moe_w_down[i])
        x = x + g2 * y
    return rmsnorm(x, final_norm_g)
```

```python
import functools

import numpy as np
import jax
import jax.numpy as jnp
from jax import lax
from jax.experimental import pallas as pl
from jax.experimental.pallas import tpu as pltpu

F32 = jnp.float32
BF16 = jnp.bfloat16
I32 = jnp.int32

NSA_HEADS = 8
NSA_KV_HEADS = 2
NSA_REP = NSA_HEADS // NSA_KV_HEADS
NSA_HEAD_DIM = 64
CMP_LEN = 32
CMP_STRIDE = 16
CMP_HIDDEN = 128
SEL_BLOCK = 64
SEL_SHIFT = 6
SEL_TOP = 16
WINDOW = 512
Q_BLOCK = 128
FORCE_BONUS = 1e4
NEG_INF = -1e30
MLSTM_HEADS = 4
MLSTM_HEAD_DIM = 128
CONV_WIDTH = 4
N_EXPERTS = 8
EPS = 1e-6

LANES = 128
SUBLANES = 8
VMEM_LIMIT = 56 * 1024 * 1024

ROW_TILE = 512
SEL_CHUNK = 512
MLSTM_CHUNK = 256
MOE_TILE = 512
GATHER_TILE = 256

SM_GATE = 0
SM_I = 24
SM_F = 28


def _cparams(sem, vmem=VMEM_LIMIT):
    return pltpu.CompilerParams(dimension_semantics=sem, vmem_limit_bytes=vmem)


def _sigmoid(x):
    return 1.0 / (1.0 + jnp.exp(-x))


def _silu(x):
    return x * _sigmoid(x)


def _log_sigmoid(x):
    return jnp.minimum(x, 0.0) - jnp.log1p(jnp.exp(-jnp.abs(x)))


def _dot(a, b):
    return jnp.dot(a, b, preferred_element_type=F32)


def _dot_nt(a, b):
    return lax.dot_general(a, b, (((1,), (1,)), ((), ())), preferred_element_type=F32)


def _split3(x):
    hi = x.astype(BF16)
    r1 = x - hi.astype(F32)
    mid = r1.astype(BF16)
    lo = (r1 - mid.astype(F32)).astype(BF16)
    return hi, mid, lo


def _rmsnorm_mod(x, g, sc, sh):
    ms = jnp.mean(x * x, axis=-1, keepdims=True)
    y = x * lax.rsqrt(ms + EPS) * g
    return y * (1.0 + sc) + sh


def _adaln_kernel(c_ref, w_ref, b_ref, o_ref):
    c = c_ref[...]
    ca = _silu(c).astype(BF16)
    o_ref[0] = _dot(ca, w_ref[0].astype(BF16)) + b_ref[0]


def _adaln(c, ada_w, ada_b):
    depth, d, n6 = ada_w.shape
    b = c.shape[0]
    cp = jnp.zeros((SUBLANES, d), F32).at[:b].set(c)
    tn = n6 // 4
    out = pl.pallas_call(
        _adaln_kernel,
        grid=(depth, n6 // tn),
        in_specs=[
            pl.BlockSpec((SUBLANES, d), lambda l, j: (0, 0)),
            pl.BlockSpec((1, d, tn), lambda l, j: (l, 0, j)),
            pl.BlockSpec((1, 1, tn), lambda l, j: (l, 0, j)),
        ],
        out_specs=pl.BlockSpec((1, SUBLANES, tn), lambda l, j: (l, 0, j)),
        out_shape=jax.ShapeDtypeStruct((depth, SUBLANES, n6), F32),
        compiler_params=_cparams(("parallel", "parallel")),
        name="adaln",
    )(cp, ada_w, ada_b.reshape(depth, 1, n6))
    return out[:, :b]


_SEC = (("q", 512), ("kv", 768), ("sm", 128), ("qk", 1024), ("v", 512), ("o", 512))


def _inproj_kernel(x_ref, g_ref, sc_ref, sh_ref, w_ref, b_ref,
                   q_ref, kv_ref, sm_ref, qk_ref, v_ref, o_ref):
    h = _rmsnorm_mod(x_ref[...], g_ref[...], sc_ref[0], sh_ref[0]).astype(BF16)

    def sec(lo, width):
        return _dot(h, w_ref[:, lo:lo + width]) + b_ref[:, lo:lo + width]

    q_ref[...] = (sec(0, 512) * (NSA_HEAD_DIM ** -0.5)).astype(BF16)
    for i in range(6):
        kv_ref[i] = sec(512 + i * LANES, LANES).astype(BF16)
    sm_ref[...] = sec(1280, 128)
    qk_ref[...] = sec(1408, 1024)
    v_ref[...] = sec(2432, 512).astype(BF16)
    o_ref[...] = sec(2944, 512)


def _inproj_weights(w_in, b_in):
    d = w_in.shape[0]
    o_q, o_kv, o_gate, o_qk, o_v, o_o, o_i, o_f = [int(v) for v in np.cumsum((0, 512, 768, 24, 1024, 512, 512, 4))]

    def cols(a):
        kv = a[..., o_kv:o_gate].reshape(a.shape[:-1] + (3, 2, NSA_KV_HEADS, NSA_HEAD_DIM))
        kv = jnp.swapaxes(kv, -3, -2)
        kv = kv.reshape(a.shape[:-1] + (768,))
        small = jnp.concatenate(
            [a[..., o_gate:o_qk], a[..., o_i:o_f], a[..., o_f:o_f + 4],
             jnp.zeros(a.shape[:-1] + (LANES - 32,), a.dtype)], axis=-1)
        return jnp.concatenate(
            [a[..., o_q:o_kv], kv, small, a[..., o_qk:o_v], a[..., o_v:o_o], a[..., o_o:o_i]], axis=-1)

    return cols(w_in).astype(BF16), cols(b_in)[None, :].astype(F32)


def _inproj(x2, g, sc, sh, w, b, seq):
    n, d = x2.shape
    tm = min(ROW_TILE, seq)
    per_b = seq // tm
    row = lambda i: (i, 0)
    bat = lambda i: (i // per_b, 0, 0)
    fix = lambda i: (0, 0)
    outs = (
        jax.ShapeDtypeStruct((n, 512), BF16),
        jax.ShapeDtypeStruct((6, n, LANES), BF16),
        jax.ShapeDtypeStruct((n, 128), F32),
        jax.ShapeDtypeStruct((n, 1024), F32),
        jax.ShapeDtypeStruct((n, 512), BF16),
        jax.ShapeDtypeStruct((n, 512), F32),
    )
    return pl.pallas_call(
        _inproj_kernel,
        grid=(n // tm,),
        in_specs=[
            pl.BlockSpec((tm, d), row),
            pl.BlockSpec((1, d), fix),
            pl.BlockSpec((1, 1, d), bat),
            pl.BlockSpec((1, 1, d), bat),
            pl.BlockSpec(w.shape, fix),
            pl.BlockSpec(b.shape, fix),
        ],
        out_specs=(
            pl.BlockSpec((tm, 512), row),
            pl.BlockSpec((6, tm, LANES), lambda i: (0, i, 0)),
            pl.BlockSpec((tm, 128), row),
            pl.BlockSpec((tm, 1024), row),
            pl.BlockSpec((tm, 512), row),
            pl.BlockSpec((tm, 512), row),
        ),
        out_shape=outs,
        compiler_params=_cparams(("parallel",)),
        name="inproj",
    )(x2, g, sc, sh, w, b)


def _compress_kernel(ch_ref, pe_ref, w1_ref, w2_ref, o_ref):
    ch = ch_ref[0].astype(F32)
    a0 = (ch + pe_ref[0:1, :]).astype(BF16)
    a1 = (ch + pe_ref[1:2, :]).astype(BF16)
    half = ch.shape[1]
    h0 = _dot(a0, w1_ref[0:half, :])
    h1 = _dot(a1, w1_ref[half:2 * half, :])
    n_chunk = ch.shape[0]
    hid = h0 + pltpu.roll(h1, n_chunk - 1, axis=0)
    o_ref[0] = _dot(_silu(hid).astype(BF16), w2_ref[...]).astype(BF16)


def _compress_weights(cmp_pe, cmp_w1, cmp_w2):
    dh, hid = NSA_HEAD_DIM, CMP_HIDDEN
    pe = jnp.concatenate([cmp_pe[0], cmp_pe[1]], axis=-1)
    pe = pe.reshape(2, CMP_STRIDE * LANES)
    w1 = cmp_w1.reshape(2, CMP_LEN, dh, hid)
    z = jnp.zeros((CMP_LEN, dh, hid), cmp_w1.dtype)
    wk = jnp.concatenate([w1[0], z], axis=1)
    wv = jnp.concatenate([z, w1[1]], axis=1)
    w1p = jnp.concatenate([wk, wv], axis=2).reshape(CMP_LEN * LANES, 2 * hid)
    z2 = jnp.zeros((hid, dh), cmp_w2.dtype)
    w2p = jnp.concatenate([jnp.concatenate([cmp_w2[0], z2], axis=1),
                           jnp.concatenate([z2, cmp_w2[1]], axis=1)], axis=0)
    return pe.astype(F32), w1p.astype(BF16), w2p.astype(BF16)


def _compress(kv_cmp, pe, w1p, w2p, batch, seq):
    g = kv_cmp.shape[0]
    n_chunk = seq // CMP_STRIDE
    ch = kv_cmp.reshape(g * batch, n_chunk, CMP_STRIDE * LANES)
    return pl.pallas_call(
        _compress_kernel,
        grid=(g * batch,),
        in_specs=[
            pl.BlockSpec((1, n_chunk, CMP_STRIDE * LANES), lambda i: (i, 0, 0)),
            pl.BlockSpec(pe.shape, lambda i: (0, 0)),
            pl.BlockSpec(w1p.shape, lambda i: (0, 0)),
            pl.BlockSpec(w2p.shape, lambda i: (0, 0)),
        ],
        out_specs=pl.BlockSpec((1, n_chunk, LANES), lambda i: (i, 0, 0)),
        out_shape=jax.ShapeDtypeStruct((g * batch, n_chunk, LANES), BF16),
        compiler_params=_cparams(("parallel",)),
        name="nsa_compress",
    )(ch, pe, w1p, w2p)


def _nsa_kernel(slope_ref, q_ref, kc_ref, ks_ref, kw_ref, gate_ref, ov_ref, o_ref, *, seq):
    g = pl.program_id(1)
    qb = pl.program_id(2)
    rows = NSA_REP * Q_BLOCK
    n_cmp = seq // CMP_STRIDE
    n_sel = seq // SEL_BLOCK
    n_top = min(SEL_TOP, n_sel)
    lane = lax.broadcasted_iota(I32, (Q_BLOCK, LANES), 1)

    qf = q_ref[...].astype(F32)
    parts = []
    for r in range(NSA_REP):
        pair = qf[:, (r // 2) * LANES:(r // 2 + 1) * LANES]
        if r % 2:
            pair = pltpu.roll(pair, NSA_HEAD_DIM, axis=1)
        parts.append(jnp.where(lane < NSA_HEAD_DIM, pair, 0.0))
    q = jnp.concatenate(parts, axis=0).astype(BF16)

    t_tok = qb * Q_BLOCK + lax.broadcasted_iota(I32, (Q_BLOCK, 1), 0)
    t_row = jnp.concatenate([t_tok] * NSA_REP, axis=0)
    slope_row = jnp.concatenate(
        [jnp.full((Q_BLOCK, 1), slope_ref[g * NSA_REP + r], F32) for r in range(NSA_REP)], axis=0)

    def scores(kv, pos):
        dist = (t_row - pos).astype(F32)
        return _dot_nt(q, kv) - slope_row * dist, t_row - pos

    kvc = kc_ref[0]
    pos_c = lax.broadcasted_iota(I32, (1, n_cmp), 1) * CMP_STRIDE + (CMP_LEN - 1)
    s, dist = scores(kvc, pos_c)
    mask = dist >= 0
    s = jnp.where(mask, s, NEG_INF)
    m = jnp.max(s, axis=-1, keepdims=True)
    e = jnp.exp(s - m)
    p_c = e / jnp.sum(e, axis=-1, keepdims=True)
    p_c = p_c * (t_row >= CMP_LEN - 1).astype(F32)
    o_c = _dot(p_c.astype(BF16), kvc)

    p4 = p_c[0:Q_BLOCK]
    for r in range(1, NSA_REP):
        p4 = p4 + p_c[r * Q_BLOCK:(r + 1) * Q_BLOCK]
    ov = ov_ref[...]
    imp = sum(_dot(piece, ov) for piece in _split3(p4))
    j_sel = lax.broadcasted_iota(I32, (1, n_sel), 1)
    cur = jnp.right_shift(t_tok, SEL_SHIFT)
    forced = (j_sel == 0) | (j_sel == cur) | (j_sel == cur - 1)
    imp = jnp.where(forced, imp + FORCE_BONUS, imp)
    imp = jnp.where(j_sel <= cur, imp, -1.0)
    rank = jnp.zeros((Q_BLOCK, n_sel), F32)
    for k in range(n_sel):
        col = imp[:, k:k + 1]
        ahead = (col > imp) | ((col == imp) & (j_sel > k))
        rank = rank + ahead.astype(F32)
    sel = jnp.where(rank < n_top, 1.0, 0.0).astype(BF16)
    sel4 = jnp.concatenate([sel] * NSA_REP, axis=0)

    blocks_per_chunk = SEL_CHUNK // SEL_BLOCK
    n_chunks = (2 * qb + 2 + blocks_per_chunk - 1) // blocks_per_chunk
    j_row = lax.broadcasted_iota(I32, (n_sel, 1), 0)
    k_lane = lax.broadcasted_iota(I32, (1, SEL_CHUNK), 1)

    def sel_step(c, carry):
        m, l, acc = carry
        start = pl.multiple_of(c * SEL_CHUNK, SEL_CHUNK)
        kv = ks_ref[0, pl.ds(start, SEL_CHUNK), :]
        s, dist = scores(kv, start + k_lane)
        expand = jnp.where(j_row == c * blocks_per_chunk + jnp.right_shift(k_lane, SEL_SHIFT),
                           1.0, 0.0).astype(BF16)
        chosen = _dot(sel4, expand)
        s = jnp.where((dist >= 0) & (chosen > 0.5), s, NEG_INF)
        m_new = jnp.maximum(m, jnp.max(s, axis=-1, keepdims=True))
        alpha = jnp.exp(m - m_new)
        p = jnp.exp(s - m_new)
        l = alpha * l + jnp.sum(p, axis=-1, keepdims=True)
        acc = alpha * acc + _dot(p.astype(BF16), kv)
        return m_new, l, acc

    init = (jnp.full((rows, 1), NEG_INF, F32), jnp.zeros((rows, 1), F32), jnp.zeros((rows, LANES), F32))
    _, l_s, acc_s = lax.fori_loop(0, n_chunks, sel_step, init)
    o_s = acc_s / l_s

    span = Q_BLOCK + WINDOW
    start = pl.multiple_of(jnp.maximum(qb * Q_BLOCK - WINDOW, 0), Q_BLOCK)
    kv = kw_ref[0, pl.ds(start, span), :]
    s, dist = scores(kv, start + lax.broadcasted_iota(I32, (1, span), 1))
    s = jnp.where((dist >= 0) & (dist < WINDOW), s, NEG_INF)
    m = jnp.max(s, axis=-1, keepdims=True)
    p = jnp.exp(s - m)
    o_w = _dot(p.astype(BF16), kv) / jnp.sum(p, axis=-1, keepdims=True)

    gates = _sigmoid(gate_ref[0])
    mixed = []
    for r in range(NSA_REP):
        rs = slice(r * Q_BLOCK, (r + 1) * Q_BLOCK)
        mixed.append(gates[:, 3 * r:3 * r + 1] * o_c[rs]
                     + gates[:, 3 * r + 1:3 * r + 2] * o_s[rs]
                     + gates[:, 3 * r + 2:3 * r + 3] * o_w[rs])
    for pr in range(NSA_REP // 2):
        left = pltpu.roll(mixed[2 * pr], NSA_HEAD_DIM, axis=1)
        o_ref[:, pr * LANES:(pr + 1) * LANES] = jnp.where(
            lane < NSA_HEAD_DIM, left, mixed[2 * pr + 1]).astype(o_ref.dtype)


def _overlap_matrix(n_cmp_pad, n_sel):
    lo_c = np.arange(n_cmp_pad)[:, None] * CMP_STRIDE
    lo_s = np.arange(n_sel)[None, :] * SEL_BLOCK
    ov = np.minimum(lo_c + CMP_LEN, lo_s + SEL_BLOCK) - np.maximum(lo_c, lo_s)
    ov = np.clip(ov, 0, None) / CMP_LEN
    ov[n_cmp_pad - 1] = 0.0
    return jnp.asarray(ov, dtype=BF16)


def _nsa(q, kvp, kc, gates, batch, seq):
    n = q.shape[0]
    n_qb = seq // Q_BLOCK
    n_cmp = seq // CMP_STRIDE
    n_sel = seq // SEL_BLOCK
    slopes = jnp.asarray(2.0 ** (-8.0 * np.arange(1, NSA_HEADS + 1) / NSA_HEADS), dtype=F32)
    ov = _overlap_matrix(n_cmp, n_sel)
    kv5 = kvp.reshape(3, NSA_KV_HEADS, batch, seq, LANES)
    ks = kv5[1].reshape(NSA_KV_HEADS * batch, seq, LANES)
    kw = kv5[2].reshape(NSA_KV_HEADS * batch, seq, LANES)
    grid_spec = pltpu.PrefetchScalarGridSpec(
        num_scalar_prefetch=1,
        grid=(batch, NSA_KV_HEADS, n_qb),
        in_specs=[
            pl.BlockSpec((Q_BLOCK, 2 * LANES), lambda b, g, i, s: (b * n_qb + i, g)),
            pl.BlockSpec((1, n_cmp, LANES), lambda b, g, i, s: (g * batch + b, 0, 0)),
            pl.BlockSpec((1, seq, LANES), lambda b, g, i, s: (g * batch + b, 0, 0)),
            pl.BlockSpec((1, seq, LANES), lambda b, g, i, s: (g * batch + b, 0, 0)),
            pl.BlockSpec((1, Q_BLOCK, 16), lambda b, g, i, s: (b * NSA_KV_HEADS + g, i, 0)),
            pl.BlockSpec(ov.shape, lambda b, g, i, s: (0, 0)),
        ],
        out_specs=pl.BlockSpec((Q_BLOCK, 2 * LANES), lambda b, g, i, s: (b * n_qb + i, g)),
    )
    return pl.pallas_call(
        functools.partial(_nsa_kernel, seq=seq),
        grid_spec=grid_spec,
        out_shape=jax.ShapeDtypeStruct((n, 512), BF16),
        compiler_params=_cparams(("parallel", "parallel", "parallel")),
        name="nsa_attention",
    )(slopes, q, kc, ks, kw, gates.reshape(batch * NSA_KV_HEADS, seq, 16), ov)


def _conv_kernel(cur_ref, prev_ref, w_ref, b_ref, q_ref, k_ref):
    i = pl.program_id(1)
    cur = cur_ref[...]
    prev = jnp.where(i > 0, prev_ref[...], 0.0)
    tc = cur.shape[0]
    row = lax.broadcasted_iota(I32, (tc, 1), 0)
    head = jnp.concatenate([jnp.zeros((tc - SUBLANES, cur.shape[1]), F32), prev], axis=0)
    y = b_ref[...]
    for tap in range(CONV_WIDTH):
        back = CONV_WIDTH - 1 - tap
        if back:
            shifted = jnp.where(row < back, pltpu.roll(head, back, axis=0), pltpu.roll(cur, back, axis=0))
        else:
            shifted = cur
        y = y + shifted * w_ref[tap:tap + 1, :]
    y = _silu(y)
    half = y.shape[1] // 2
    q_ref[...] = y[:, :half].astype(BF16)
    k_ref[...] = (y[:, half:] * (MLSTM_HEAD_DIM ** -0.5)).astype(BF16)


def _conv(qk, conv_w, conv_b, batch, seq):
    n, c = qk.shape
    tc = min(ROW_TILE, seq)
    per_b = seq // tc
    sub = tc // SUBLANES
    return pl.pallas_call(
        _conv_kernel,
        grid=(batch, per_b),
        in_specs=[
            pl.BlockSpec((tc, c), lambda b, i: (b * per_b + i, 0)),
            pl.BlockSpec((SUBLANES, c), lambda b, i: (jnp.maximum((b * per_b + i) * sub - 1, 0), 0)),
            pl.BlockSpec(conv_w.shape, lambda b, i: (0, 0)),
            pl.BlockSpec((1, c), lambda b, i: (0, 0)),
        ],
        out_specs=(
            pl.BlockSpec((tc, c // 2), lambda b, i: (b * per_b + i, 0)),
            pl.BlockSpec((tc, c // 2), lambda b, i: (b * per_b + i, 0)),
        ),
        out_shape=(jax.ShapeDtypeStruct((n, c // 2), BF16), jax.ShapeDtypeStruct((n, c // 2), BF16)),
        compiler_params=_cparams(("parallel", "parallel")),
        name="mlstm_conv",
    )(qk, qk, conv_w, conv_b[None, :])


def _mlstm_kernel(q_ref, k_ref, v_ref, o_ref, sm_ref, g_ref, y_ref, c_scr, n_scr, m_scr):
    c_idx = pl.program_id(1)
    lc = q_ref.shape[0]
    d = MLSTM_HEAD_DIM

    @pl.when(c_idx == 0)
    def _():
        c_scr[...] = jnp.zeros_like(c_scr)
        n_scr[...] = jnp.zeros_like(n_scr)
        m_scr[...] = jnp.zeros_like(m_scr)

    sm = sm_ref[...]
    lf = _log_sigmoid(sm)
    ri = lax.broadcasted_iota(I32, (lc, lc), 0)
    ci = lax.broadcasted_iota(I32, (lc, lc), 1)
    causal = ri >= ci
    tri = jnp.where(causal, 1.0, 0.0).astype(BF16)
    a_col = sum(_dot(tri, piece) for piece in _split3(lf))
    a_row = a_col.T
    sm_t = sm.T

    for h in range(MLSTM_HEADS):
        hs = slice(h * d, (h + 1) * d)
        a_j = a_col[:, SM_F + h:SM_F + h + 1]
        a_s = a_row[SM_F + h:SM_F + h + 1, :]
        ig_j = sm[:, SM_I + h:SM_I + h + 1]
        ig_s = sm_t[SM_I + h:SM_I + h + 1, :]
        m_prev = m_scr[h:h + 1, 0:1]
        qh, kh, vh = q_ref[:, hs], k_ref[:, hs], v_ref[:, hs]
        c_prev = c_scr[h]
        n_prev = n_scr[h:h + 1, :]

        log_d = jnp.where(causal, a_j - a_s + ig_s, -jnp.inf)
        m_inter = a_j + m_prev
        m_t = jnp.maximum(m_inter, jnp.max(log_d, axis=-1, keepdims=True))
        d_mat = jnp.exp(log_d - m_t)
        inter = jnp.exp(m_inter - m_t)
        s_qk = _dot_nt(qh, kh) * d_mat
        num = inter * _dot_nt(qh, c_prev.astype(BF16)) + _dot(s_qk.astype(BF16), vh)
        den = inter * jnp.sum(qh.astype(F32) * n_prev, axis=-1, keepdims=True) \
            + jnp.sum(s_qk, axis=-1, keepdims=True)
        hh = num / jnp.maximum(jnp.abs(den), jnp.exp(-m_t))

        a_last = a_j[lc - 1:lc, :]
        log_w = a_last - a_s + ig_s
        m_new = jnp.maximum(a_last + m_prev, jnp.max(log_w, axis=-1, keepdims=True))
        w_col = jnp.exp(a_last - a_j + ig_j - m_new)
        decay = jnp.exp(a_last + m_prev - m_new)
        wv_t = (w_col * vh.astype(F32)).T.astype(BF16)
        c_scr[h] = decay * c_prev + _dot(wv_t, kh)
        n_scr[h:h + 1, :] = decay * n_prev + jnp.sum(w_col * kh.astype(F32), axis=0, keepdims=True)
        m_scr[h:h + 1, :] = jnp.broadcast_to(m_new, (1, LANES))

        hg = _sigmoid(o_ref[:, hs]) * hh
        hn = hg * lax.rsqrt(jnp.mean(hg * hg, axis=-1, keepdims=True) + EPS)
        y_ref[:, hs] = (hn * g_ref[:, hs]).astype(y_ref.dtype)


def _mlstm(qm, km, vm, om, sm, norm_g, batch, seq):
    n, w = qm.shape
    lc = min(MLSTM_CHUNK, seq)
    per_b = seq // lc
    row = lambda b, c: (b * per_b + c, 0)
    return pl.pallas_call(
        _mlstm_kernel,
        grid=(batch, per_b),
        in_specs=[
            pl.BlockSpec((lc, w), row),
            pl.BlockSpec((lc, w), row),
            pl.BlockSpec((lc, w), row),
            pl.BlockSpec((lc, w), row),
            pl.BlockSpec((lc, LANES), row),
            pl.BlockSpec((1, w), lambda b, c: (0, 0)),
        ],
        out_specs=pl.BlockSpec((lc, w), row),
        out_shape=jax.ShapeDtypeStruct((n, w), BF16),
        scratch_shapes=[
            pltpu.VMEM((MLSTM_HEADS, MLSTM_HEAD_DIM, MLSTM_HEAD_DIM), F32),
            pltpu.VMEM((SUBLANES, MLSTM_HEAD_DIM), F32),
            pltpu.VMEM((SUBLANES, LANES), F32),
        ],
        compiler_params=_cparams(("parallel", "arbitrary")),
        name="mlstm",
    )(qm, km, vm, om, sm, norm_g[None, :])


def _outproj_kernel(*refs, with_router):
    if with_router:
        (x_ref, ya_ref, yb_ref, w_ref, g1_ref, gn_ref, sc_ref, sh_ref, rw_ref,
         x1_ref, h_ref, lg_ref) = refs
    else:
        x_ref, ya_ref, yb_ref, w_ref, g1_ref, gn_ref, sc_ref, sh_ref, x1_ref, h_ref = refs
    half = ya_ref.shape[1]
    y = _dot(ya_ref[...], w_ref[0:half, :]) + _dot(yb_ref[...], w_ref[half:2 * half, :])
    x1 = x_ref[...] + g1_ref[0] * y
    x1_ref[...] = x1
    h = _rmsnorm_mod(x1, gn_ref[...], sc_ref[0], sh_ref[0])
    h_ref[...] = h.astype(h_ref.dtype)
    if with_router:
        lg_ref[...] = lax.dot_general(rw_ref[...], h, (((1,), (1,)), ((), ())),
                                      precision=lax.Precision.HIGHEST, preferred_element_type=F32)


def _outproj(x2, ya, yb, w_out, g1, gn, sc, sh, seq, router_wt=None):
    n, d = x2.shape
    tm = min(ROW_TILE, seq)
    per_b = seq // tm
    row = lambda i: (i, 0)
    bat = lambda i: (i // per_b, 0, 0)
    fix = lambda i: (0, 0)
    with_router = router_wt is not None
    in_specs = [
        pl.BlockSpec((tm, d), row),
        pl.BlockSpec((tm, ya.shape[1]), row),
        pl.BlockSpec((tm, yb.shape[1]), row),
        pl.BlockSpec(w_out.shape, fix),
        pl.BlockSpec((1, 1, d), bat),
        pl.BlockSpec((1, d), fix),
        pl.BlockSpec((1, 1, d), bat),
        pl.BlockSpec((1, 1, d), bat),
    ]
    out_specs = [pl.BlockSpec((tm, d), row), pl.BlockSpec((tm, d), row)]
    out_shape = [jax.ShapeDtypeStruct((n, d), F32),
                 jax.ShapeDtypeStruct((n, d), F32 if with_router else BF16)]
    args = [x2, ya, yb, w_out, g1, gn, sc, sh]
    if with_router:
        in_specs.append(pl.BlockSpec(router_wt.shape, fix))
        out_specs.append(pl.BlockSpec((N_EXPERTS, tm), lambda i: (0, i)))
        out_shape.append(jax.ShapeDtypeStruct((N_EXPERTS, n), F32))
        args.append(router_wt)
    return pl.pallas_call(
        functools.partial(_outproj_kernel, with_router=with_router),
        grid=(n // tm,),
        in_specs=in_specs,
        out_specs=tuple(out_specs),
        out_shape=tuple(out_shape),
        compiler_params=_cparams(("parallel",)),
        name="outproj_router" if with_router else "outproj",
    )(*args)


def _ffn_kernel(h_ref, x_ref, wg_ref, wu_ref, wd_ref, g2_ref, o_ref, act_scr, *, tf):
    h = h_ref[...]
    d_ff = wg_ref.shape[1]
    for j in range(d_ff // tf):
        cs = slice(j * tf, (j + 1) * tf)
        act_scr[:, cs] = (_silu(_dot(h, wg_ref[:, cs])) * _dot(h, wu_ref[:, cs])).astype(BF16)
    o_ref[...] = x_ref[...] + g2_ref[0] * _dot(act_scr[...], wd_ref[...])


def _ffn(h, x1, wg, wu, wd, g2, seq):
    n, d = x1.shape
    d_ff = wg.shape[1]
    tm = min(ROW_TILE, seq)
    per_b = seq // tm
    tf = 256
    row = lambda i: (i, 0)
    fix = lambda i: (0, 0)
    once = dict(pipeline_mode=pl.Buffered(1))
    return pl.pallas_call(
        functools.partial(_ffn_kernel, tf=tf),
        grid=(n // tm,),
        in_specs=[
            pl.BlockSpec((tm, d), row),
            pl.BlockSpec((tm, d), row),
            pl.BlockSpec(wg.shape, fix, **once),
            pl.BlockSpec(wu.shape, fix, **once),
            pl.BlockSpec(wd.shape, fix, **once),
            pl.BlockSpec((1, 1, d), lambda i: (i // per_b, 0, 0)),
        ],
        out_specs=pl.BlockSpec((tm, d), row),
        out_shape=jax.ShapeDtypeStruct((n, d), F32),
        scratch_shapes=[pltpu.VMEM((tm, d_ff), BF16)],
        compiler_params=_cparams(("parallel",)),
        name="dense_ffn",
    )(h, x1, wg, wu, wd, g2)


def _route_kernel(lg_ref, dest_ref, wt_ref, meta_ref, cnt_scr, exc_scr, *, tile):
    n_e, n = lg_ref.shape
    lg = lg_ref[...]
    e_iota = lax.broadcasted_iota(I32, (n_e, n), 0)
    m1 = jnp.max(lg, axis=0, keepdims=True)
    e0 = jnp.min(jnp.where(lg == m1, e_iota, n_e), axis=0, keepdims=True)
    lg2 = jnp.where(e_iota == e0, -jnp.inf, lg)
    m2 = jnp.max(lg2, axis=0, keepdims=True)
    e1 = jnp.min(jnp.where(lg2 == m2, e_iota, n_e), axis=0, keepdims=True)
    ex = jnp.exp(m2 - m1)
    wt_ref[0:1, :] = 1.0 / (1.0 + ex)
    wt_ref[1:2, :] = ex / (1.0 + ex)
    oh0 = e_iota == e0
    oh1 = e_iota == e1
    cnt_scr[...] = jnp.where(oh0, 1.0, 0.0) + jnp.where(oh1, 1.0, 0.0)

    ri = lax.broadcasted_iota(I32, (LANES, 2 * LANES), 0)
    ci = lax.broadcasted_iota(I32, (LANES, 2 * LANES), 1)
    prefix_total = jnp.where((ci >= LANES) | (ri < ci), 1.0, 0.0).astype(BF16)

    def block(kb, carry):
        ls = pl.ds(pl.multiple_of(kb * LANES, LANES), LANES)
        both = _dot(cnt_scr[:, ls].astype(BF16), prefix_total)
        exc_scr[:, ls] = both[:, :LANES] + carry
        return carry + both[:, LANES:]

    total = lax.fori_loop(0, n // LANES, block, jnp.zeros((n_e, LANES), F32))
    padded = jnp.floor((total + (tile - 1)) / tile) * tile
    e_col = lax.broadcasted_iota(I32, (n_e, LANES), 0)
    starts = jnp.zeros((n_e, LANES), F32)
    for e in range(n_e - 1):
        starts = starts + jnp.where(e_col > e, padded[e:e + 1, :], 0.0)
    ends = starts + padded
    slot = starts[:, 0:1] + exc_scr[...]
    dest_ref[0:1, :] = jnp.sum(jnp.where(oh0, slot, 0.0), axis=0, keepdims=True).astype(I32)
    dest_ref[1:2, :] = jnp.sum(jnp.where(oh1, slot, 0.0), axis=0, keepdims=True).astype(I32)
    blk_start = (lax.broadcasted_iota(I32, (n_e, LANES), 1) * tile).astype(F32)
    blk_exp = jnp.sum(jnp.where(ends <= blk_start, 1.0, 0.0), axis=0, keepdims=True)
    meta_ref[0:1, :] = jnp.minimum(blk_exp, n_e - 1.0).astype(I32)
    meta_ref[1:2, :] = (ends[n_e - 1:n_e, :] / tile).astype(I32)
    meta_ref[2:SUBLANES, :] = jnp.zeros((SUBLANES - 2, LANES), I32)


def _route(logits_t, tile):
    n_e, n = logits_t.shape
    return pl.pallas_call(
        functools.partial(_route_kernel, tile=tile),
        out_shape=(jax.ShapeDtypeStruct((2, n), I32), jax.ShapeDtypeStruct((2, n), F32),
                   jax.ShapeDtypeStruct((SUBLANES, LANES), I32)),
        scratch_shapes=[pltpu.VMEM((n_e, n), F32), pltpu.VMEM((n_e, n), F32)],
        compiler_params=pltpu.CompilerParams(vmem_limit_bytes=VMEM_LIMIT),
        name="moe_route",
    )(logits_t)


def _dispatch_kernel(dest_ref, h_ref, xs_in_ref, xs_ref, sem):
    del xs_in_ref
    i = pl.program_id(0)
    td = h_ref.shape[0]

    def copy(r, k):
        row = dest_ref[k * (pl.num_programs(0) * td) + i * td + r]
        return pltpu.make_async_copy(h_ref.at[pl.ds(r, 1)], xs_ref.at[pl.ds(row, 1)], sem)

    def start(r, _):
        copy(r, 0).start()
        copy(r, 1).start()
        return 0

    def wait(r, _):
        copy(r, 0).wait()
        copy(r, 1).wait()
        return 0

    lax.fori_loop(0, td, start, 0)
    lax.fori_loop(0, td, wait, 0)


def _dispatch(dest, h, n_rows):
    n, d = h.shape
    td = min(GATHER_TILE, n)
    xs0 = jnp.zeros((n_rows, d), h.dtype)
    grid_spec = pltpu.PrefetchScalarGridSpec(
        num_scalar_prefetch=1,
        grid=(n // td,),
        in_specs=[pl.BlockSpec((td, d), lambda i, s: (i, 0)), pl.BlockSpec(memory_space=pl.ANY)],
        out_specs=pl.BlockSpec(memory_space=pl.ANY),
        scratch_shapes=[pltpu.SemaphoreType.DMA(())],
    )
    return pl.pallas_call(
        _dispatch_kernel,
        grid_spec=grid_spec,
        out_shape=jax.ShapeDtypeStruct((n_rows, d), h.dtype),
        input_output_aliases={2: 0},
        compiler_params=_cparams(("arbitrary",)),
        name="moe_dispatch",
    )(dest.reshape(-1), h, xs0)


def _expert_kernel(meta_ref, x_ref, wg_ref, wu_ref, wd_ref, y_ref, xb_scr):
    i = pl.program_id(0)
    j = pl.program_id(1)

    @pl.when(i < meta_ref[1, 0])
    def _():
        @pl.when(j == 0)
        def _():
            xb_scr[...] = x_ref[...].astype(BF16)

        xb = xb_scr[...]
        act = (_silu(_dot(xb, wg_ref[0])) * _dot(xb, wu_ref[0])).astype(BF16)
        part = _dot(act, wd_ref[0])

        @pl.when(j == 0)
        def _():
            y_ref[...] = part

        @pl.when(j > 0)
        def _():
            y_ref[...] += part

    @pl.when((i >= meta_ref[1, 0]) & (j == 0))
    def _():
        y_ref[...] = jnp.zeros_like(y_ref)


def _experts(meta, xs, wg, wu, wd, tile):
    n_rows, d = xs.shape
    d_ff = wg.shape[2]
    tf = 896 if d_ff % 896 == 0 else d_ff
    n_blk = n_rows // tile

    def blk(i, s):
        return jnp.minimum(i, s[1, 0] - 1)

    grid_spec = pltpu.PrefetchScalarGridSpec(
        num_scalar_prefetch=1,
        grid=(n_blk, d_ff // tf),
        in_specs=[
            pl.BlockSpec((tile, d), lambda i, j, s: (blk(i, s), 0)),
            pl.BlockSpec((1, d, tf), lambda i, j, s: (s[0, blk(i, s)], 0, jnp.where(i < s[1, 0], j, 0))),
            pl.BlockSpec((1, d, tf), lambda i, j, s: (s[0, blk(i, s)], 0, jnp.where(i < s[1, 0], j, 0))),
            pl.BlockSpec((1, tf, d), lambda i, j, s: (s[0, blk(i, s)], jnp.where(i < s[1, 0], j, 0), 0)),
        ],
        out_specs=pl.BlockSpec((tile, d), lambda i, j, s: (i, 0)),
        scratch_shapes=[pltpu.VMEM((tile, d), BF16)],
    )
    return pl.pallas_call(
        _expert_kernel,
        grid_spec=grid_spec,
        out_shape=jax.ShapeDtypeStruct((n_rows, d), F32),
        compiler_params=_cparams(("arbitrary", "arbitrary")),
        name="moe_experts",
    )(meta, xs, wg, wu, wd)


def _combine_kernel(dest_ref, y_ref, x_ref, wt_ref, g2_ref, gn_ref, o_ref, buf, sem):
    i = pl.program_id(0)
    tc = x_ref.shape[0]

    def copy(r, k):
        row = dest_ref[k * (pl.num_programs(0) * tc) + i * tc + r]
        return pltpu.make_async_copy(y_ref.at[pl.ds(row, 1)], buf.at[k, pl.ds(r, 1)], sem)

    def start(r, _):
        copy(r, 0).start()
        copy(r, 1).start()
        return 0

    def wait(r, _):
        copy(r, 0).wait()
        copy(r, 1).wait()
        return 0

    lax.fori_loop(0, tc, start, 0)
    lax.fori_loop(0, tc, wait, 0)
    wt = wt_ref[...]
    y = buf[0] * wt[:, 0:1] + buf[1] * wt[:, 1:2]
    x = x_ref[...] + g2_ref[0] * y
    ms = jnp.mean(x * x, axis=-1, keepdims=True)
    o_ref[...] = x * lax.rsqrt(ms + EPS) * gn_ref[...]


def _combine(dest, y, x1, wt, g2, gn, seq):
    n, d = x1.shape
    tc = min(GATHER_TILE, seq)
    per_b = seq // tc
    grid_spec = pltpu.PrefetchScalarGridSpec(
        num_scalar_prefetch=1,
        grid=(n // tc,),
        in_specs=[
            pl.BlockSpec(memory_space=pl.ANY),
            pl.BlockSpec((tc, d), lambda i, s: (i, 0)),
            pl.BlockSpec((tc, 2), lambda i, s: (i, 0)),
            pl.BlockSpec((1, 1, d), lambda i, s: (i // per_b, 0, 0)),
            pl.BlockSpec((1, d), lambda i, s: (0, 0)),
        ],
        out_specs=pl.BlockSpec((tc, d), lambda i, s: (i, 0)),
        scratch_shapes=[pltpu.VMEM((2, tc, d), F32), pltpu.SemaphoreType.DMA(())],
    )
    return pl.pallas_call(
        _combine_kernel,
        grid_spec=grid_spec,
        out_shape=jax.ShapeDtypeStruct((n, d), F32),
        compiler_params=_cparams(("arbitrary",)),
        name="moe_combine_norm",
    )(dest.reshape(-1), y, x1, wt, g2, gn)


def _final_norm_kernel(x_ref, g_ref, o_ref):
    x = x_ref[...]
    ms = jnp.mean(x * x, axis=-1, keepdims=True)
    o_ref[...] = x * lax.rsqrt(ms + EPS) * g_ref[...]


def _mixer(x2, mod, l, batch, seq, norm_mix_g, w_in, b_in, cmp_pe, cmp_w1, cmp_w2,
           conv_w, conv_b, mlstm_norm_g):
    d = x2.shape[1]
    sh1, sc1 = mod[l, :, 0:d], mod[l, :, d:2 * d]
    w, b = _inproj_weights(w_in[l], b_in[l])
    q, kvp, sm, qk, vm, om = _inproj(x2, norm_mix_g[l][None, :], sc1[:, None, :], sh1[:, None, :], w, b, seq)
    pe, w1p, w2p = _compress_weights(cmp_pe[l], cmp_w1[l], cmp_w2[l])
    kc = _compress(kvp[0:NSA_KV_HEADS], pe, w1p, w2p, batch, seq)
    gates = sm[:, SM_GATE:SM_GATE + 24].reshape(batch, seq, NSA_KV_HEADS, NSA_REP * 3)
    gates = jnp.pad(gates.transpose(0, 2, 1, 3), ((0, 0), (0, 0), (0, 0), (0, 16 - NSA_REP * 3)))
    y_nsa = _nsa(q, kvp, kc, gates, batch, seq)
    qm, km = _conv(qk, conv_w[l], conv_b[l], batch, seq)
    y_ml = _mlstm(qm, km, vm, om, sm, mlstm_norm_g[l], batch, seq)
    return y_nsa, y_ml


def kernel(x, c, ada_w, ada_b, norm_mix_g, norm_ffn_g, w_in, b_in, cmp_pe, cmp_w1, cmp_w2, conv_w, conv_b, mlstm_norm_g, w_out, ffn_w_gate, ffn_w_up, ffn_w_down, router_w, moe_w_gate, moe_w_up, moe_w_down, final_norm_g):
    batch, seq, d = x.shape
    depth = ada_w.shape[0]
    n = batch * seq
    mod = _adaln(c, ada_w, ada_b)
    x2 = x.reshape(n, d)
    for l in range(depth):
        g1 = mod[l, :, 2 * d:3 * d][:, None, :]
        sh2 = mod[l, :, 3 * d:4 * d][:, None, :]
        sc2 = mod[l, :, 4 * d:5 * d][:, None, :]
        g2 = mod[l, :, 5 * d:6 * d][:, None, :]
        y_nsa, y_ml = _mixer(x2, mod, l, batch, seq, norm_mix_g, w_in, b_in, cmp_pe, cmp_w1, cmp_w2,
                             conv_w, conv_b, mlstm_norm_g)
        gn = norm_ffn_g[l][None, :]
        i = l // 2
        last = l == depth - 1
        if l % 2 == 0:
            x1, h = _outproj(x2, y_nsa, y_ml, w_out[l].astype(BF16), g1, gn, sc2, sh2, seq)
            x2 = _ffn(h, x1, ffn_w_gate[i].astype(BF16), ffn_w_up[i].astype(BF16),
                      ffn_w_down[i].astype(BF16), g2, seq)
            if last:
                x2 = _final_norm(x2, final_norm_g)
        else:
            x1, h, logits_t = _outproj(x2, y_nsa, y_ml, w_out[l].astype(BF16), g1, gn, sc2, sh2, seq,
                                       router_wt=router_w[i].T)
            n_rows = 2 * n + N_EXPERTS * MOE_TILE
            dest, wt, meta = _route(logits_t, MOE_TILE)
            xs = _dispatch(dest, h, n_rows)
            y = _experts(meta, xs, moe_w_gate[i].astype(BF16), moe_w_up[i].astype(BF16),
                         moe_w_down[i].astype(BF16), MOE_TILE)
            unit = jnp.ones((1, d), F32)
            x2 = _combine(dest, y, x1, wt.T, g2, final_norm_g[None, :] if last else unit, seq)
            if not last:
                raise NotImplementedError("a MoE layer that is not the last layer")
    return x2.reshape(batch, seq, d)


def _final_norm(x2, g):
    n, d = x2.shape
    tm = min(ROW_TILE, n)
    return pl.pallas_call(
        _final_norm_kernel,
        grid=(n // tm,),
        in_specs=[pl.BlockSpec((tm, d), lambda i: (i, 0)), pl.BlockSpec((1, d), lambda i: (0, 0))],
        out_specs=pl.BlockSpec((tm, d), lambda i: (i, 0)),
        out_shape=jax.ShapeDtypeStruct((n, d), F32),
        compiler_params=_cparams(("parallel",)),
        name="final_norm",
    )(x2, g[None, :])
```

```python
import functools

import numpy as np
import jax
import jax.numpy as jnp
from jax import lax
from jax.experimental import pallas as pl
from jax.experimental.pallas import tpu as pltpu

F32 = jnp.float32
BF16 = jnp.bfloat16
I32 = jnp.int32

NSA_HEADS = 8
NSA_KV_HEADS = 2
NSA_REP = NSA_HEADS // NSA_KV_HEADS
NSA_HEAD_DIM = 64
CMP_LEN = 32
CMP_STRIDE = 16
CMP_HIDDEN = 128
SEL_BLOCK = 64
SEL_SHIFT = 6
SEL_TOP = 16
WINDOW = 512
Q_BLOCK = 128
FORCE_BONUS = 1e4
NEG_INF = -1e30
MLSTM_HEADS = 4
MLSTM_HEAD_DIM = 128
CONV_WIDTH = 4
N_EXPERTS = 8
EPS = 1e-6

LANES = 128
SUBLANES = 8
VMEM_LIMIT = 56 * 1024 * 1024

ROW_TILE = 512
SEL_CHUNK = 512
MLSTM_CHUNK = 256
MOE_TILE = 512
GATHER_TILE = 256
DMA_UNROLL = 8

SM_GATE = 0
SM_I = 24
SM_F = 28


def _cparams(sem, vmem=VMEM_LIMIT):
    return pltpu.CompilerParams(dimension_semantics=sem, vmem_limit_bytes=vmem)


def _sigmoid(x):
    return 1.0 / (1.0 + jnp.exp(-x))


def _silu(x):
    return x * _sigmoid(x)


def _log_sigmoid(x):
    return jnp.minimum(x, 0.0) - jnp.log1p(jnp.exp(-jnp.abs(x)))


def _dot(a, b):
    return jnp.dot(a, b, preferred_element_type=F32)


def _dot_nt(a, b):
    return lax.dot_general(a, b, (((1,), (1,)), ((), ())), preferred_element_type=F32)


def _split3(x):
    hi = x.astype(BF16)
    r1 = x - hi.astype(F32)
    mid = r1.astype(BF16)
    lo = (r1 - mid.astype(F32)).astype(BF16)
    return hi, mid, lo


def _rmsnorm_mod(x, g, sc, sh):
    ms = jnp.mean(x * x, axis=-1, keepdims=True)
    y = x * lax.rsqrt(ms + EPS) * g
    return y * (1.0 + sc) + sh


def _adaln_kernel(c_ref, w_ref, b_ref, o_ref):
    c = c_ref[...]
    ca = _silu(c).astype(BF16)
    o_ref[0] = _dot(ca, w_ref[0].astype(BF16)) + b_ref[0]


def _adaln(c, ada_w, ada_b):
    depth, d, n6 = ada_w.shape
    b = c.shape[0]
    cp = jnp.zeros((SUBLANES, d), F32).at[:b].set(c)
    tn = n6 // 4
    out = pl.pallas_call(
        _adaln_kernel,
        grid=(depth, n6 // tn),
        in_specs=[
            pl.BlockSpec((SUBLANES, d), lambda l, j: (0, 0)),
            pl.BlockSpec((1, d, tn), lambda l, j: (l, 0, j)),
            pl.BlockSpec((1, 1, tn), lambda l, j: (l, 0, j)),
        ],
        out_specs=pl.BlockSpec((1, SUBLANES, tn), lambda l, j: (l, 0, j)),
        out_shape=jax.ShapeDtypeStruct((depth, SUBLANES, n6), F32),
        compiler_params=_cparams(("parallel", "parallel")),
        name="adaln",
    )(cp, ada_w, ada_b.reshape(depth, 1, n6))
    return out[:, :b]


_SEC = (("q", 512), ("kv", 768), ("sm", 128), ("qk", 1024), ("v", 512), ("o", 512))


def _inproj_kernel(x_ref, g_ref, sc_ref, sh_ref, w_ref, b_ref,
                   q_ref, kv_ref, sm_ref, qk_ref, v_ref, o_ref):
    h = _rmsnorm_mod(x_ref[...], g_ref[...], sc_ref[0], sh_ref[0]).astype(BF16)

    def sec(lo, width):
        return _dot(h, w_ref[:, lo:lo + width]) + b_ref[:, lo:lo + width]

    q_ref[...] = (sec(0, 512) * (NSA_HEAD_DIM ** -0.5)).astype(BF16)
    for i in range(6):
        kv_ref[i] = sec(512 + i * LANES, LANES).astype(BF16)
    sm_ref[...] = sec(1280, 128)
    qk_ref[...] = sec(1408, 1024)
    v_ref[...] = sec(2432, 512).astype(BF16)
    o_ref[...] = sec(2944, 512)


def _inproj_weights(w_in, b_in):
    d = w_in.shape[0]
    o_q, o_kv, o_gate, o_qk, o_v, o_o, o_i, o_f = [int(v) for v in np.cumsum((0, 512, 768, 24, 1024, 512, 512, 4))]

    def cols(a):
        kv = a[..., o_kv:o_gate].reshape(a.shape[:-1] + (3, 2, NSA_KV_HEADS, NSA_HEAD_DIM))
        kv = jnp.swapaxes(kv, -3, -2)
        kv = kv.reshape(a.shape[:-1] + (768,))
        small = jnp.concatenate(
            [a[..., o_gate:o_qk], a[..., o_i:o_f], a[..., o_f:o_f + 4],
             jnp.zeros(a.shape[:-1] + (LANES - 32,), a.dtype)], axis=-1)
        return jnp.concatenate(
            [a[..., o_q:o_kv], kv, small, a[..., o_qk:o_v], a[..., o_v:o_o], a[..., o_o:o_i]], axis=-1)

    return cols(w_in).astype(BF16), cols(b_in)[None, :].astype(F32)


def _inproj(x2, g, sc, sh, w, b, seq):
    n, d = x2.shape
    tm = min(ROW_TILE, seq)
    per_b = seq // tm
    row = lambda i: (i, 0)
    bat = lambda i: (i // per_b, 0, 0)
    fix = lambda i: (0, 0)
    outs = (
        jax.ShapeDtypeStruct((n, 512), BF16),
        jax.ShapeDtypeStruct((6, n, LANES), BF16),
        jax.ShapeDtypeStruct((n, 128), F32),
        jax.ShapeDtypeStruct((n, 1024), F32),
        jax.ShapeDtypeStruct((n, 512), BF16),
        jax.ShapeDtypeStruct((n, 512), F32),
    )
    return pl.pallas_call(
        _inproj_kernel,
        grid=(n // tm,),
        in_specs=[
            pl.BlockSpec((tm, d), row),
            pl.BlockSpec((1, d), fix),
            pl.BlockSpec((1, 1, d), bat),
            pl.BlockSpec((1, 1, d), bat),
            pl.BlockSpec(w.shape, fix),
            pl.BlockSpec(b.shape, fix),
        ],
        out_specs=(
            pl.BlockSpec((tm, 512), row),
            pl.BlockSpec((6, tm, LANES), lambda i: (0, i, 0)),
            pl.BlockSpec((tm, 128), row),
            pl.BlockSpec((tm, 1024), row),
            pl.BlockSpec((tm, 512), row),
            pl.BlockSpec((tm, 512), row),
        ),
        out_shape=outs,
        compiler_params=_cparams(("parallel",)),
        name="inproj",
    )(x2, g, sc, sh, w, b)


def _compress_kernel(ch_ref, pe_ref, w1_ref, w2_ref, o_ref):
    ch = ch_ref[0].astype(F32)
    a0 = (ch + pe_ref[0:1, :]).astype(BF16)
    a1 = (ch + pe_ref[1:2, :]).astype(BF16)
    half = ch.shape[1]
    h0 = _dot(a0, w1_ref[0:half, :])
    h1 = _dot(a1, w1_ref[half:2 * half, :])
    n_chunk = ch.shape[0]
    hid = h0 + pltpu.roll(h1, n_chunk - 1, axis=0)
    o_ref[0] = _dot(_silu(hid).astype(BF16), w2_ref[...]).astype(BF16)


def _compress_weights(cmp_pe, cmp_w1, cmp_w2):
    dh, hid = NSA_HEAD_DIM, CMP_HIDDEN
    pe = jnp.concatenate([cmp_pe[0], cmp_pe[1]], axis=-1)
    pe = pe.reshape(2, CMP_STRIDE * LANES)
    w1 = cmp_w1.reshape(2, CMP_LEN, dh, hid)
    z = jnp.zeros((CMP_LEN, dh, hid), cmp_w1.dtype)
    wk = jnp.concatenate([w1[0], z], axis=1)
    wv = jnp.concatenate([z, w1[1]], axis=1)
    w1p = jnp.concatenate([wk, wv], axis=2).reshape(CMP_LEN * LANES, 2 * hid)
    z2 = jnp.zeros((hid, dh), cmp_w2.dtype)
    w2p = jnp.concatenate([jnp.concatenate([cmp_w2[0], z2], axis=1),
                           jnp.concatenate([z2, cmp_w2[1]], axis=1)], axis=0)
    return pe.astype(F32), w1p.astype(BF16), w2p.astype(BF16)


def _compress(kv_cmp, pe, w1p, w2p, batch, seq):
    g = kv_cmp.shape[0]
    n_chunk = seq // CMP_STRIDE
    ch = kv_cmp.reshape(g * batch, n_chunk, CMP_STRIDE * LANES)
    return pl.pallas_call(
        _compress_kernel,
        grid=(g * batch,),
        in_specs=[
            pl.BlockSpec((1, n_chunk, CMP_STRIDE * LANES), lambda i: (i, 0, 0)),
            pl.BlockSpec(pe.shape, lambda i: (0, 0)),
            pl.BlockSpec(w1p.shape, lambda i: (0, 0)),
            pl.BlockSpec(w2p.shape, lambda i: (0, 0)),
        ],
        out_specs=pl.BlockSpec((1, n_chunk, LANES), lambda i: (i, 0, 0)),
        out_shape=jax.ShapeDtypeStruct((g * batch, n_chunk, LANES), BF16),
        compiler_params=_cparams(("parallel",)),
        name="nsa_compress",
    )(ch, pe, w1p, w2p)


def _nsa2_kernel(slope_ref, q_ref, kc_ref, ks_ref, kw_ref, gate_ref, ovt_ref, feat_ref, featc_ref,
                 o_ref, kaug_c, vt_c, kaug_s, vt_s, kaug_w, vt_w, *, seq):
    g = pl.program_id(1)
    qb = pl.program_id(2)
    n_cmp = seq // CMP_STRIDE
    n_sel = seq // SEL_BLOCK
    n_top = min(SEL_TOP, n_sel)
    dh = NSA_HEAD_DIM
    cols = NSA_REP * Q_BLOCK

    @pl.when(qb == 0)
    def _():
        def build(src_ref, f_ref, kaug, vt, n_rows):
            step = min(SEL_CHUNK, n_rows)
            lane = lax.broadcasted_iota(I32, (step, LANES), 1)
            row = lax.broadcasted_iota(I32, (LANES, step), 0)
            for c0 in range(0, n_rows, step):
                x = src_ref[0, c0:c0 + step, :].astype(F32)
                kaug[c0:c0 + step, :] = jnp.where(lane < dh, x, f_ref[c0:c0 + step, :].astype(F32)).astype(BF16)
                vt[:, c0:c0 + step] = jnp.where(row == 0, 1.0, x.T).astype(BF16)

        build(kc_ref, featc_ref, kaug_c, vt_c, n_cmp)
        build(ks_ref, feat_ref, kaug_s, vt_s, seq)
        build(kw_ref, feat_ref, kaug_w, vt_w, seq)

    lane = lax.broadcasted_iota(I32, (Q_BLOCK, LANES), 1)
    blk_f = (lane - dh).astype(F32)
    t_lane = qb * Q_BLOCK + lax.broadcasted_iota(I32, (1, Q_BLOCK), 1)
    qf = q_ref[...].astype(F32)

    def q_aug(selmat):
        parts = []
        for r in range(NSA_REP):
            pair = qf[:, (r // 2) * LANES:(r // 2 + 1) * LANES]
            if r % 2:
                pair = pltpu.roll(pair, dh, axis=1)
            slope = slope_ref[g * NSA_REP + r]
            far = slope * SEL_BLOCK * blk_f
            if selmat is not None:
                far = jnp.where(selmat > 0.5, far, NEG_INF)
            parts.append(jnp.where(lane < dh, pair, jnp.where(lane == dh, slope, far)))
        return jnp.concatenate(parts, axis=0).astype(BF16)

    def tile4(x):
        return jnp.concatenate([x] * NSA_REP, axis=1)

    q_all = q_aug(None)

    s = _dot_nt(kaug_c[...], q_all)
    end_c = lax.broadcasted_iota(I32, (n_cmp, Q_BLOCK), 0) * CMP_STRIDE + (CMP_LEN - 1)
    s = s + tile4(jnp.where(end_c <= t_lane, 0.0, NEG_INF))
    e = jnp.exp(s - jnp.max(s, axis=0, keepdims=True))
    p_c = e / jnp.sum(e, axis=0, keepdims=True)
    p_c = p_c * tile4((t_lane >= CMP_LEN - 1).astype(F32))
    o_c = _dot(vt_c[...], p_c.astype(BF16))

    p4 = p_c[:, 0:Q_BLOCK]
    for r in range(1, NSA_REP):
        p4 = p4 + p_c[:, r * Q_BLOCK:(r + 1) * Q_BLOCK]
    ovt = ovt_ref[...]
    imp = sum(_dot(ovt, piece) for piece in _split3(p4))
    j_col = lax.broadcasted_iota(I32, (n_sel, 1), 0)
    cur = jnp.right_shift(t_lane, SEL_SHIFT)
    forced = (j_col == 0) | (j_col == cur) | (j_col == cur - 1)
    imp = jnp.where(forced, imp + FORCE_BONUS, imp)
    imp = jnp.where(j_col <= cur, imp, -1.0)
    groups = [imp[v * SUBLANES:(v + 1) * SUBLANES, :] for v in range(n_sel // SUBLANES)]
    j_grp = lax.broadcasted_iota(I32, (SUBLANES, Q_BLOCK), 0)
    ranks = [jnp.zeros((SUBLANES, Q_BLOCK), F32) for _ in groups]
    for k in range(n_sel):
        row_k = groups[k // SUBLANES][k % SUBLANES:k % SUBLANES + 1, :]
        for v, grp in enumerate(groups):
            ge = jnp.where(row_k >= grp, 1.0, 0.0)
            gt = jnp.where(row_k > grp, 1.0, 0.0)
            if v * SUBLANES > k:
                inc = ge
            elif (v + 1) * SUBLANES - 1 < k:
                inc = gt
            else:
                inc = jnp.where(j_grp + v * SUBLANES > k, ge, gt)
            ranks[v] = ranks[v] + inc
    sel_t = jnp.where(jnp.concatenate(ranks, axis=0) < n_top, 1.0, 0.0)
    pad_lo = jnp.zeros((dh, Q_BLOCK), F32)
    pieces = [pad_lo, sel_t]
    if n_sel < dh:
        pieces.append(jnp.zeros((dh - n_sel, Q_BLOCK), F32))
    selmat = jnp.concatenate(pieces, axis=0).T
    q_sel = q_aug(selmat)

    def sel_scores(c):
        start = pl.multiple_of(c * SEL_CHUNK, SEL_CHUNK)
        return _dot_nt(kaug_s[pl.ds(start, SEL_CHUNK), :], q_sel)

    def sel_update(c, s, carry):
        m, acc = carry
        start = pl.multiple_of(c * SEL_CHUNK, SEL_CHUNK)
        m_new = jnp.maximum(m, jnp.max(s, axis=0, keepdims=True))
        p = jnp.exp(s - m_new).astype(BF16)
        acc = jnp.exp(m - m_new) * acc + _dot(vt_s[:, pl.ds(start, SEL_CHUNK)], p)
        return m_new, acc

    def sel_pair(i, carry):
        s0, s1 = sel_scores(2 * i), sel_scores(2 * i + 1)
        return sel_update(2 * i + 1, s1, sel_update(2 * i, s0, carry))

    def sel_single(c, carry):
        return sel_update(c, sel_scores(c), carry)

    last = (qb * Q_BLOCK) // SEL_CHUNK
    init = (jnp.full((1, cols), NEG_INF, F32), jnp.zeros((LANES, cols), F32))
    carry = lax.fori_loop(0, last // 2, sel_pair, init)
    carry = lax.fori_loop(2 * (last // 2), last, sel_single, carry)
    pos = last * SEL_CHUNK + lax.broadcasted_iota(I32, (SEL_CHUNK, Q_BLOCK), 0)
    s = sel_scores(last) + tile4(jnp.where(pos <= t_lane, 0.0, NEG_INF))
    _, acc = sel_update(last, s, carry)
    o_s = acc / acc[0:1, :]

    span = Q_BLOCK + WINDOW
    start = pl.multiple_of(jnp.maximum(qb * Q_BLOCK - WINDOW, 0), Q_BLOCK)
    s = _dot_nt(kaug_w[pl.ds(start, span), :], q_all)
    dist = t_lane - (start + lax.broadcasted_iota(I32, (span, Q_BLOCK), 0))
    s = s + tile4(jnp.where(dist >= 0, jnp.where(dist < WINDOW, 0.0, NEG_INF), NEG_INF))
    p = jnp.exp(s - jnp.max(s, axis=0, keepdims=True)).astype(BF16)
    acc = _dot(vt_w[:, pl.ds(start, span)], p)
    o_w = acc / acc[0:1, :]

    gates = _sigmoid(gate_ref[0])
    mixed = []
    for r in range(NSA_REP):
        cs = slice(r * Q_BLOCK, (r + 1) * Q_BLOCK)
        mixed.append(gates[3 * r:3 * r + 1, :] * o_c[dh:, cs]
                     + gates[3 * r + 1:3 * r + 2, :] * o_s[dh:, cs]
                     + gates[3 * r + 2:3 * r + 3, :] * o_w[dh:, cs])
    for pr in range(NSA_REP // 2):
        pair_t = jnp.concatenate([mixed[2 * pr], mixed[2 * pr + 1]], axis=0)
        o_ref[:, pr * LANES:(pr + 1) * LANES] = pair_t.T.astype(o_ref.dtype)


def _nsa_constants(seq):
    n_cmp, n_sel = seq // CMP_STRIDE, seq // SEL_BLOCK
    assert n_sel <= NSA_HEAD_DIM, "one feature lane per selection block"
    slopes = 2.0 ** (-8.0 * np.arange(1, NSA_HEADS + 1) / NSA_HEADS)
    far = (slopes[:, None] * SEL_BLOCK * np.arange(n_sel)[None, :]).astype(np.float32)
    assert np.array_equal(far.astype(BF16).astype(np.float32), far), "ALiBi features must be exact in bf16"
    lo_c = np.arange(n_cmp)[:, None] * CMP_STRIDE
    lo_s = np.arange(n_sel)[None, :] * SEL_BLOCK
    ov = np.clip(np.minimum(lo_c + CMP_LEN, lo_s + SEL_BLOCK) - np.maximum(lo_c, lo_s), 0, None) / CMP_LEN
    ov[n_cmp - 1] = 0.0

    def feats(pos):
        f = np.zeros((pos.shape[0], LANES), np.float32)
        f[:, NSA_HEAD_DIM] = pos % SEL_BLOCK
        blk = pos // SEL_BLOCK
        ok = (blk >= 1) & (blk < NSA_HEAD_DIM)
        f[np.nonzero(ok)[0], NSA_HEAD_DIM + blk[ok]] = 1.0
        return f

    feat = feats(np.arange(seq))
    pos_c = np.arange(n_cmp) * CMP_STRIDE + CMP_LEN - 1
    featc = feats(pos_c)
    featc[pos_c >= seq] = 0.0
    return (jnp.asarray(slopes, F32), jnp.asarray(ov.T, BF16), jnp.asarray(feat, BF16), jnp.asarray(featc, BF16))


def _nsa2(q, kvp, kc, gates_t, batch, seq):
    n = q.shape[0]
    n_qb = seq // Q_BLOCK
    n_cmp = seq // CMP_STRIDE
    slopes, ovt, feat, featc = _nsa_constants(seq)
    kv5 = kvp.reshape(3, NSA_KV_HEADS, batch, seq, LANES)
    ks = kv5[1].reshape(NSA_KV_HEADS * batch, seq, LANES)
    kw = kv5[2].reshape(NSA_KV_HEADS * batch, seq, LANES)
    fix = lambda b, g, i, s: (0, 0)
    per_bg = lambda b, g, i, s: (g * batch + b, 0, 0)
    grid_spec = pltpu.PrefetchScalarGridSpec(
        num_scalar_prefetch=1,
        grid=(batch, NSA_KV_HEADS, n_qb),
        in_specs=[
            pl.BlockSpec((Q_BLOCK, 2 * LANES), lambda b, g, i, s: (b * n_qb + i, g)),
            pl.BlockSpec((1, n_cmp, LANES), per_bg),
            pl.BlockSpec((1, seq, LANES), per_bg),
            pl.BlockSpec((1, seq, LANES), per_bg),
            pl.BlockSpec((1, 16, Q_BLOCK), lambda b, g, i, s: (b * NSA_KV_HEADS + g, 0, i)),
            pl.BlockSpec(ovt.shape, fix),
            pl.BlockSpec(feat.shape, fix),
            pl.BlockSpec(featc.shape, fix),
        ],
        out_specs=pl.BlockSpec((Q_BLOCK, 2 * LANES), lambda b, g, i, s: (b * n_qb + i, g)),
        scratch_shapes=[
            pltpu.VMEM((n_cmp, LANES), BF16), pltpu.VMEM((LANES, n_cmp), BF16),
            pltpu.VMEM((seq, LANES), BF16), pltpu.VMEM((LANES, seq), BF16),
            pltpu.VMEM((seq, LANES), BF16), pltpu.VMEM((LANES, seq), BF16),
        ],
    )
    return pl.pallas_call(
        functools.partial(_nsa2_kernel, seq=seq),
        grid_spec=grid_spec,
        out_shape=jax.ShapeDtypeStruct((n, 512), BF16),
        compiler_params=_cparams(("parallel", "parallel", "arbitrary")),
        name="nsa_attention",
    )(slopes, q, kc, ks, kw, gates_t, ovt, feat, featc)


def _conv_kernel(cur_ref, prev_ref, w_ref, b_ref, q_ref, k_ref):
    i = pl.program_id(1)
    cur = cur_ref[...]
    prev = jnp.where(i > 0, prev_ref[...], 0.0)
    tc = cur.shape[0]
    row = lax.broadcasted_iota(I32, (tc, 1), 0)
    head = jnp.concatenate([jnp.zeros((tc - SUBLANES, cur.shape[1]), F32), prev], axis=0)
    y = b_ref[...]
    for tap in range(CONV_WIDTH):
        back = CONV_WIDTH - 1 - tap
        if back:
            shifted = jnp.where(row < back, pltpu.roll(head, back, axis=0), pltpu.roll(cur, back, axis=0))
        else:
            shifted = cur
        y = y + shifted * w_ref[tap:tap + 1, :]
    y = _silu(y)
    half = y.shape[1] // 2
    q_ref[...] = y[:, :half].astype(BF16)
    k_ref[...] = (y[:, half:] * (MLSTM_HEAD_DIM ** -0.5)).astype(BF16)


def _conv(qk, conv_w, conv_b, batch, seq):
    n, c = qk.shape
    tc = min(ROW_TILE, seq)
    per_b = seq // tc
    sub = tc // SUBLANES
    return pl.pallas_call(
        _conv_kernel,
        grid=(batch, per_b),
        in_specs=[
            pl.BlockSpec((tc, c), lambda b, i: (b * per_b + i, 0)),
            pl.BlockSpec((SUBLANES, c), lambda b, i: (jnp.maximum((b * per_b + i) * sub - 1, 0), 0)),
            pl.BlockSpec(conv_w.shape, lambda b, i: (0, 0)),
            pl.BlockSpec((1, c), lambda b, i: (0, 0)),
        ],
        out_specs=(
            pl.BlockSpec((tc, c // 2), lambda b, i: (b * per_b + i, 0)),
            pl.BlockSpec((tc, c // 2), lambda b, i: (b * per_b + i, 0)),
        ),
        out_shape=(jax.ShapeDtypeStruct((n, c // 2), BF16), jax.ShapeDtypeStruct((n, c // 2), BF16)),
        compiler_params=_cparams(("parallel", "parallel")),
        name="mlstm_conv",
    )(qk, qk, conv_w, conv_b[None, :])


def _mlstm_kernel(q_ref, k_ref, v_ref, o_ref, sm_ref, g_ref, y_ref, c_scr, n_scr, m_scr):
    c_idx = pl.program_id(1)
    lc = q_ref.shape[0]
    d = MLSTM_HEAD_DIM

    @pl.when(c_idx == 0)
    def _():
        c_scr[...] = jnp.zeros_like(c_scr)
        n_scr[...] = jnp.zeros_like(n_scr)
        m_scr[...] = jnp.zeros_like(m_scr)

    sm = sm_ref[...]
    lf = _log_sigmoid(sm)
    ri = lax.broadcasted_iota(I32, (lc, lc), 0)
    ci = lax.broadcasted_iota(I32, (lc, lc), 1)
    causal = ri >= ci
    tri = jnp.where(causal, 1.0, 0.0).astype(BF16)
    a_col = sum(_dot(tri, piece) for piece in _split3(lf))
    a_row = a_col.T
    sm_t = sm.T

    for h in range(MLSTM_HEADS):
        hs = slice(h * d, (h + 1) * d)
        a_j = a_col[:, SM_F + h:SM_F + h + 1]
        a_s = a_row[SM_F + h:SM_F + h + 1, :]
        ig_j = sm[:, SM_I + h:SM_I + h + 1]
        ig_s = sm_t[SM_I + h:SM_I + h + 1, :]
        m_prev = m_scr[h:h + 1, 0:1]
        qh, kh, vh = q_ref[:, hs], k_ref[:, hs], v_ref[:, hs]
        c_prev = c_scr[h]
        n_prev = n_scr[h:h + 1, :]

        log_d = jnp.where(causal, a_j - a_s + ig_s, -jnp.inf)
        m_inter = a_j + m_prev
        m_t = jnp.maximum(m_inter, jnp.max(log_d, axis=-1, keepdims=True))
        d_mat = jnp.exp(log_d - m_t)
        inter = jnp.exp(m_inter - m_t)
        s_qk = _dot_nt(qh, kh) * d_mat
        num = inter * _dot_nt(qh, c_prev.astype(BF16)) + _dot(s_qk.astype(BF16), vh)
        den = inter * jnp.sum(qh.astype(F32) * n_prev, axis=-1, keepdims=True) \
            + jnp.sum(s_qk, axis=-1, keepdims=True)
        hh = num / jnp.maximum(jnp.abs(den), jnp.exp(-m_t))

        a_last = a_j[lc - 1:lc, :]
        log_w = a_last - a_s + ig_s
        m_new = jnp.maximum(a_last + m_prev, jnp.max(log_w, axis=-1, keepdims=True))
        w_col = jnp.exp(a_last - a_j + ig_j - m_new)
        decay = jnp.exp(a_last + m_prev - m_new)
        wv_t = (w_col * vh.astype(F32)).T.astype(BF16)
        c_scr[h] = decay * c_prev + _dot(wv_t, kh)
        n_scr[h:h + 1, :] = decay * n_prev + jnp.sum(w_col * kh.astype(F32), axis=0, keepdims=True)
        m_scr[h:h + 1, :] = jnp.broadcast_to(m_new, (1, LANES))

        hg = _sigmoid(o_ref[:, hs]) * hh
        hn = hg * lax.rsqrt(jnp.mean(hg * hg, axis=-1, keepdims=True) + EPS)
        y_ref[:, hs] = (hn * g_ref[:, hs]).astype(y_ref.dtype)


def _mlstm(qm, km, vm, om, sm, norm_g, batch, seq):
    n, w = qm.shape
    lc = min(MLSTM_CHUNK, seq)
    per_b = seq // lc
    row = lambda b, c: (b * per_b + c, 0)
    return pl.pallas_call(
        _mlstm_kernel,
        grid=(batch, per_b),
        in_specs=[
            pl.BlockSpec((lc, w), row),
            pl.BlockSpec((lc, w), row),
            pl.BlockSpec((lc, w), row),
            pl.BlockSpec((lc, w), row),
            pl.BlockSpec((lc, LANES), row),
            pl.BlockSpec((1, w), lambda b, c: (0, 0)),
        ],
        out_specs=pl.BlockSpec((lc, w), row),
        out_shape=jax.ShapeDtypeStruct((n, w), BF16),
        scratch_shapes=[
            pltpu.VMEM((MLSTM_HEADS, MLSTM_HEAD_DIM, MLSTM_HEAD_DIM), F32),
            pltpu.VMEM((SUBLANES, MLSTM_HEAD_DIM), F32),
            pltpu.VMEM((SUBLANES, LANES), F32),
        ],
        compiler_params=_cparams(("parallel", "arbitrary")),
        name="mlstm",
    )(qm, km, vm, om, sm, norm_g[None, :])


def _outproj_kernel(*refs, with_router):
    if with_router:
        (x_ref, ya_ref, yb_ref, w_ref, g1_ref, gn_ref, sc_ref, sh_ref, rw_ref,
         x1_ref, h_ref, lg_ref) = refs
    else:
        x_ref, ya_ref, yb_ref, w_ref, g1_ref, gn_ref, sc_ref, sh_ref, x1_ref, h_ref = refs
    half = ya_ref.shape[1]
    y = _dot(ya_ref[...], w_ref[0:half, :]) + _dot(yb_ref[...], w_ref[half:2 * half, :])
    x1 = x_ref[...] + g1_ref[0] * y
    x1_ref[...] = x1
    h = _rmsnorm_mod(x1, gn_ref[...], sc_ref[0], sh_ref[0])
    h_ref[...] = h.astype(h_ref.dtype)
    if with_router:
        lg_ref[...] = lax.dot_general(rw_ref[...], h, (((1,), (1,)), ((), ())),
                                      precision=lax.Precision.HIGHEST, preferred_element_type=F32)


def _outproj(x2, ya, yb, w_out, g1, gn, sc, sh, seq, router_wt=None):
    n, d = x2.shape
    tm = min(ROW_TILE, seq)
    per_b = seq // tm
    row = lambda i: (i, 0)
    bat = lambda i: (i // per_b, 0, 0)
    fix = lambda i: (0, 0)
    with_router = router_wt is not None
    in_specs = [
        pl.BlockSpec((tm, d), row),
        pl.BlockSpec((tm, ya.shape[1]), row),
        pl.BlockSpec((tm, yb.shape[1]), row),
        pl.BlockSpec(w_out.shape, fix),
        pl.BlockSpec((1, 1, d), bat),
        pl.BlockSpec((1, d), fix),
        pl.BlockSpec((1, 1, d), bat),
        pl.BlockSpec((1, 1, d), bat),
    ]
    out_specs = [pl.BlockSpec((tm, d), row), pl.BlockSpec((tm, d), row)]
    out_shape = [jax.ShapeDtypeStruct((n, d), F32),
                 jax.ShapeDtypeStruct((n, d), F32 if with_router else BF16)]
    args = [x2, ya, yb, w_out, g1, gn, sc, sh]
    if with_router:
        in_specs.append(pl.BlockSpec(router_wt.shape, fix))
        out_specs.append(pl.BlockSpec((N_EXPERTS, tm), lambda i: (0, i)))
        out_shape.append(jax.ShapeDtypeStruct((N_EXPERTS, n), F32))
        args.append(router_wt)
    return pl.pallas_call(
        functools.partial(_outproj_kernel, with_router=with_router),
        grid=(n // tm,),
        in_specs=in_specs,
        out_specs=tuple(out_specs),
        out_shape=tuple(out_shape),
        compiler_params=_cparams(("parallel",)),
        name="outproj_router" if with_router else "outproj",
    )(*args)


def _ffn_kernel(h_ref, x_ref, wg_ref, wu_ref, wd_ref, g2_ref, o_ref, act_scr, *, tf):
    h = h_ref[...]
    d_ff = wg_ref.shape[1]
    for j in range(d_ff // tf):
        cs = slice(j * tf, (j + 1) * tf)
        act_scr[:, cs] = (_silu(_dot(h, wg_ref[:, cs])) * _dot(h, wu_ref[:, cs])).astype(BF16)
    o_ref[...] = x_ref[...] + g2_ref[0] * _dot(act_scr[...], wd_ref[...])


def _ffn(h, x1, wg, wu, wd, g2, seq):
    n, d = x1.shape
    d_ff = wg.shape[1]
    tm = min(ROW_TILE, seq)
    per_b = seq // tm
    tf = 256
    row = lambda i: (i, 0)
    fix = lambda i: (0, 0)
    once = dict(pipeline_mode=pl.Buffered(1))
    return pl.pallas_call(
        functools.partial(_ffn_kernel, tf=tf),
        grid=(n // tm,),
        in_specs=[
            pl.BlockSpec((tm, d), row),
            pl.BlockSpec((tm, d), row),
            pl.BlockSpec(wg.shape, fix, **once),
            pl.BlockSpec(wu.shape, fix, **once),
            pl.BlockSpec(wd.shape, fix, **once),
            pl.BlockSpec((1, 1, d), lambda i: (i // per_b, 0, 0)),
        ],
        out_specs=pl.BlockSpec((tm, d), row),
        out_shape=jax.ShapeDtypeStruct((n, d), F32),
        scratch_shapes=[pltpu.VMEM((tm, d_ff), BF16)],
        compiler_params=_cparams(("parallel",)),
        name="dense_ffn",
    )(h, x1, wg, wu, wd, g2)


def _route_kernel(lg_ref, dest_ref, wt_ref, meta_ref, cnt_scr, exc_scr, *, tile):
    n_e, n = lg_ref.shape
    lg = lg_ref[...]
    e_iota = lax.broadcasted_iota(I32, (n_e, n), 0)
    m1 = jnp.max(lg, axis=0, keepdims=True)
    e0 = jnp.min(jnp.where(lg == m1, e_iota, n_e), axis=0, keepdims=True)
    lg2 = jnp.where(e_iota == e0, -jnp.inf, lg)
    m2 = jnp.max(lg2, axis=0, keepdims=True)
    e1 = jnp.min(jnp.where(lg2 == m2, e_iota, n_e), axis=0, keepdims=True)
    ex = jnp.exp(m2 - m1)
    wt_ref[0:1, :] = 1.0 / (1.0 + ex)
    wt_ref[1:2, :] = ex / (1.0 + ex)
    oh0 = e_iota == e0
    oh1 = e_iota == e1
    cnt_scr[...] = jnp.where(oh0, 1.0, 0.0) + jnp.where(oh1, 1.0, 0.0)

    ri = lax.broadcasted_iota(I32, (LANES, 2 * LANES), 0)
    ci = lax.broadcasted_iota(I32, (LANES, 2 * LANES), 1)
    prefix_total = jnp.where((ci >= LANES) | (ri < ci), 1.0, 0.0).astype(BF16)

    def block(kb, carry):
        ls = pl.ds(pl.multiple_of(kb * LANES, LANES), LANES)
        both = _dot(cnt_scr[:, ls].astype(BF16), prefix_total)
        exc_scr[:, ls] = both[:, :LANES] + carry
        return carry + both[:, LANES:]

    total = lax.fori_loop(0, n // LANES, block, jnp.zeros((n_e, LANES), F32))
    padded = jnp.floor((total + (tile - 1)) / tile) * tile
    e_col = lax.broadcasted_iota(I32, (n_e, LANES), 0)
    starts = jnp.zeros((n_e, LANES), F32)
    for e in range(n_e - 1):
        starts = starts + jnp.where(e_col > e, padded[e:e + 1, :], 0.0)
    ends = starts + padded
    slot = starts[:, 0:1] + exc_scr[...]
    dest_ref[0:1, :] = jnp.sum(jnp.where(oh0, slot, 0.0), axis=0, keepdims=True).astype(I32)
    dest_ref[1:2, :] = jnp.sum(jnp.where(oh1, slot, 0.0), axis=0, keepdims=True).astype(I32)
    blk_start = (lax.broadcasted_iota(I32, (n_e, LANES), 1) * tile).astype(F32)
    blk_exp = jnp.sum(jnp.where(ends <= blk_start, 1.0, 0.0), axis=0, keepdims=True)
    meta_ref[0:1, :] = jnp.minimum(blk_exp, n_e - 1.0).astype(I32)
    meta_ref[1:2, :] = (ends[n_e - 1:n_e, :] / tile).astype(I32)
    meta_ref[2:SUBLANES, :] = jnp.zeros((SUBLANES - 2, LANES), I32)


def _route(logits_t, tile):
    n_e, n = logits_t.shape
    return pl.pallas_call(
        functools.partial(_route_kernel, tile=tile),
        out_shape=(jax.ShapeDtypeStruct((2, n), I32), jax.ShapeDtypeStruct((2, n), F32),
                   jax.ShapeDtypeStruct((SUBLANES, LANES), I32)),
        scratch_shapes=[pltpu.VMEM((n_e, n), F32), pltpu.VMEM((n_e, n), F32)],
        compiler_params=pltpu.CompilerParams(vmem_limit_bytes=VMEM_LIMIT),
        name="moe_route",
    )(logits_t)


def _dispatch_kernel(dest_ref, h_ref, xs_in_ref, xs_ref, sem):
    del xs_in_ref
    i = pl.program_id(0)
    td = h_ref.shape[0]

    def copy(r, k):
        row = dest_ref[k * (pl.num_programs(0) * td) + i * td + r]
        return pltpu.make_async_copy(h_ref.at[pl.ds(r, 1)], xs_ref.at[pl.ds(row, 1)], sem)

    def start(r, _):
        copy(r, 0).start()
        copy(r, 1).start()
        return 0

    lax.fori_loop(0, td, start, 0, unroll=DMA_UNROLL)
    for _ in range(2):
        pltpu.make_async_copy(h_ref, xs_ref.at[pl.ds(0, td)], sem).wait()


def _dispatch(dest, h, n_rows):
    n, d = h.shape
    td = min(GATHER_TILE, n)
    xs0 = jnp.zeros((n_rows, d), h.dtype)
    grid_spec = pltpu.PrefetchScalarGridSpec(
        num_scalar_prefetch=1,
        grid=(n // td,),
        in_specs=[pl.BlockSpec((td, d), lambda i, s: (i, 0)), pl.BlockSpec(memory_space=pl.ANY)],
        out_specs=pl.BlockSpec(memory_space=pl.ANY),
        scratch_shapes=[pltpu.SemaphoreType.DMA(())],
    )
    return pl.pallas_call(
        _dispatch_kernel,
        grid_spec=grid_spec,
        out_shape=jax.ShapeDtypeStruct((n_rows, d), h.dtype),
        input_output_aliases={2: 0},
        compiler_params=_cparams(("arbitrary",)),
        name="moe_dispatch",
    )(dest.reshape(-1), h, xs0)


def _expert_kernel(meta_ref, x_ref, wg_ref, wu_ref, wd_ref, y_ref, xb_scr):
    i = pl.program_id(0)
    j = pl.program_id(1)

    @pl.when(i < meta_ref[1, 0])
    def _():
        @pl.when(j == 0)
        def _():
            xb_scr[...] = x_ref[...].astype(BF16)

        xb = xb_scr[...]
        act = (_silu(_dot(xb, wg_ref[0])) * _dot(xb, wu_ref[0])).astype(BF16)
        part = _dot(act, wd_ref[0])

        @pl.when(j == 0)
        def _():
            y_ref[...] = part

        @pl.when(j > 0)
        def _():
            y_ref[...] += part

    @pl.when((i >= meta_ref[1, 0]) & (j == 0))
    def _():
        y_ref[...] = jnp.zeros_like(y_ref)


def _experts(meta, xs, wg, wu, wd, tile):
    n_rows, d = xs.shape
    d_ff = wg.shape[2]
    tf = d_ff // 2 if (d_ff // 2) % (2 * LANES) == 0 else d_ff
    n_blk = n_rows // tile

    def blk(i, s):
        return jnp.minimum(i, s[1, 0] - 1)

    grid_spec = pltpu.PrefetchScalarGridSpec(
        num_scalar_prefetch=1,
        grid=(n_blk, d_ff // tf),
        in_specs=[
            pl.BlockSpec((tile, d), lambda i, j, s: (blk(i, s), 0)),
            pl.BlockSpec((1, d, tf), lambda i, j, s: (s[0, blk(i, s)], 0, jnp.where(i < s[1, 0], j, 0))),
            pl.BlockSpec((1, d, tf), lambda i, j, s: (s[0, blk(i, s)], 0, jnp.where(i < s[1, 0], j, 0))),
            pl.BlockSpec((1, tf, d), lambda i, j, s: (s[0, blk(i, s)], jnp.where(i < s[1, 0], j, 0), 0)),
        ],
        out_specs=pl.BlockSpec((tile, d), lambda i, j, s: (i, 0)),
        scratch_shapes=[pltpu.VMEM((tile, d), BF16)],
    )
    return pl.pallas_call(
        _expert_kernel,
        grid_spec=grid_spec,
        out_shape=jax.ShapeDtypeStruct((n_rows, d), F32),
        compiler_params=_cparams(("arbitrary", "arbitrary")),
        name="moe_experts",
    )(meta, xs, wg, wu, wd)


def _combine_kernel(dest_ref, y_ref, x_ref, wt_ref, g2_ref, gn_ref, o_ref, buf, sem):
    i = pl.program_id(0)
    n_steps = pl.num_programs(0)
    tc = x_ref.shape[0]
    slot = i % 2

    def gather(step, to_slot):
        def start(r, _):
            for k in range(2):
                row = dest_ref[k * (n_steps * tc) + step * tc + r]
                pltpu.make_async_copy(y_ref.at[pl.ds(row, 1)], buf.at[to_slot, k, pl.ds(r, 1)],
                                      sem.at[to_slot]).start()
            return 0
        lax.fori_loop(0, tc, start, 0, unroll=DMA_UNROLL)

    @pl.when(i == 0)
    def _():
        gather(0, 0)

    @pl.when(i + 1 < n_steps)
    def _():
        gather(i + 1, 1 - slot)

    for k in range(2):
        pltpu.make_async_copy(y_ref.at[pl.ds(0, tc)], buf.at[slot, k], sem.at[slot]).wait()
    wt = wt_ref[...]
    y = buf[slot, 0] * wt[:, 0:1] + buf[slot, 1] * wt[:, 1:2]
    x = x_ref[...] + g2_ref[0] * y
    ms = jnp.mean(x * x, axis=-1, keepdims=True)
    o_ref[...] = x * lax.rsqrt(ms + EPS) * gn_ref[...]


def _combine(dest, y, x1, wt, g2, gn, seq):
    n, d = x1.shape
    tc = min(GATHER_TILE, seq)
    per_b = seq // tc
    grid_spec = pltpu.PrefetchScalarGridSpec(
        num_scalar_prefetch=1,
        grid=(n // tc,),
        in_specs=[
            pl.BlockSpec(memory_space=pl.ANY),
            pl.BlockSpec((tc, d), lambda i, s: (i, 0)),
            pl.BlockSpec((tc, 2), lambda i, s: (i, 0)),
            pl.BlockSpec((1, 1, d), lambda i, s: (i // per_b, 0, 0)),
            pl.BlockSpec((1, d), lambda i, s: (0, 0)),
        ],
        out_specs=pl.BlockSpec((tc, d), lambda i, s: (i, 0)),
        scratch_shapes=[pltpu.VMEM((2, 2, tc, d), F32), pltpu.SemaphoreType.DMA((2,))],
    )
    return pl.pallas_call(
        _combine_kernel,
        grid_spec=grid_spec,
        out_shape=jax.ShapeDtypeStruct((n, d), F32),
        compiler_params=_cparams(("arbitrary",)),
        name="moe_combine_norm",
    )(dest.reshape(-1), y, x1, wt, g2, gn)


def _final_norm_kernel(x_ref, g_ref, o_ref):
    x = x_ref[...]
    ms = jnp.mean(x * x, axis=-1, keepdims=True)
    o_ref[...] = x * lax.rsqrt(ms + EPS) * g_ref[...]


def _mixer(x2, mod, l, batch, seq, norm_mix_g, w_in, b_in, cmp_pe, cmp_w1, cmp_w2,
           conv_w, conv_b, mlstm_norm_g):
    d = x2.shape[1]
    sh1, sc1 = mod[l, :, 0:d], mod[l, :, d:2 * d]
    w, b = _inproj_weights(w_in[l], b_in[l])
    q, kvp, sm, qk, vm, om = _inproj(x2, norm_mix_g[l][None, :], sc1[:, None, :], sh1[:, None, :], w, b, seq)
    pe, w1p, w2p = _compress_weights(cmp_pe[l], cmp_w1[l], cmp_w2[l])
    kc = _compress(kvp[0:NSA_KV_HEADS], pe, w1p, w2p, batch, seq)
    gates = sm[:, SM_GATE:SM_GATE + 24].reshape(batch, seq, NSA_KV_HEADS, NSA_REP * 3)
    gates = jnp.pad(gates.transpose(0, 2, 3, 1), ((0, 0), (0, 0), (0, 16 - NSA_REP * 3), (0, 0)))
    y_nsa = _nsa2(q, kvp, kc, gates.reshape(batch * NSA_KV_HEADS, 16, seq), batch, seq)
    qm, km = _conv(qk, conv_w[l], conv_b[l], batch, seq)
    y_ml = _mlstm(qm, km, vm, om, sm, mlstm_norm_g[l], batch, seq)
    return y_nsa, y_ml


def kernel(x, c, ada_w, ada_b, norm_mix_g, norm_ffn_g, w_in, b_in, cmp_pe, cmp_w1, cmp_w2, conv_w, conv_b, mlstm_norm_g, w_out, ffn_w_gate, ffn_w_up, ffn_w_down, router_w, moe_w_gate, moe_w_up, moe_w_down, final_norm_g):
    batch, seq, d = x.shape
    depth = ada_w.shape[0]
    n = batch * seq
    mod = _adaln(c, ada_w, ada_b)
    x2 = x.reshape(n, d)
    for l in range(depth):
        g1 = mod[l, :, 2 * d:3 * d][:, None, :]
        sh2 = mod[l, :, 3 * d:4 * d][:, None, :]
        sc2 = mod[l, :, 4 * d:5 * d][:, None, :]
        g2 = mod[l, :, 5 * d:6 * d][:, None, :]
        y_nsa, y_ml = _mixer(x2, mod, l, batch, seq, norm_mix_g, w_in, b_in, cmp_pe, cmp_w1, cmp_w2,
                             conv_w, conv_b, mlstm_norm_g)
        gn = norm_ffn_g[l][None, :]
        i = l // 2
        last = l == depth - 1
        if l % 2 == 0:
            x1, h = _outproj(x2, y_nsa, y_ml, w_out[l].astype(BF16), g1, gn, sc2, sh2, seq)
            x2 = _ffn(h, x1, ffn_w_gate[i].astype(BF16), ffn_w_up[i].astype(BF16),
                      ffn_w_down[i].astype(BF16), g2, seq)
            if last:
                x2 = _final_norm(x2, final_norm_g)
        else:
            x1, h, logits_t = _outproj(x2, y_nsa, y_ml, w_out[l].astype(BF16), g1, gn, sc2, sh2, seq,
                                       router_wt=router_w[i].T)
            n_rows = 2 * n + N_EXPERTS * MOE_TILE
            dest, wt, meta = _route(logits_t, MOE_TILE)
            xs = _dispatch(dest, h, n_rows)
            y = _experts(meta, xs, moe_w_gate[i].astype(BF16), moe_w_up[i].astype(BF16),
                         moe_w_down[i].astype(BF16), MOE_TILE)
            unit = jnp.ones((1, d), F32)
            x2 = _combine(dest, y, x1, wt.T, g2, final_norm_g[None, :] if last else unit, seq)
            if not last:
                raise NotImplementedError("a MoE layer that is not the last layer")
    return x2.reshape(batch, seq, d)


def _final_norm(x2, g):
    n, d = x2.shape
    tm = min(ROW_TILE, n)
    return pl.pallas_call(
        _final_norm_kernel,
        grid=(n // tm,),
        in_specs=[pl.BlockSpec((tm, d), lambda i: (i, 0)), pl.BlockSpec((1, d), lambda i: (0, 0))],
        out_specs=pl.BlockSpec((tm, d), lambda i: (i, 0)),
        out_shape=jax.ShapeDtypeStruct((n, d), F32),
        compiler_params=_cparams(("parallel",)),
        name="final_norm",
    )(x2, g[None, :])
```

```python
import functools

import numpy as np
import jax
import jax.numpy as jnp
from jax import lax
from jax.experimental import pallas as pl
from jax.experimental.pallas import tpu as pltpu

F32 = jnp.float32
BF16 = jnp.bfloat16
I32 = jnp.int32

NSA_HEADS = 8
NSA_KV_HEADS = 2
NSA_REP = NSA_HEADS // NSA_KV_HEADS
NSA_HEAD_DIM = 64
CMP_LEN = 32
CMP_STRIDE = 16
CMP_HIDDEN = 128
SEL_BLOCK = 64
SEL_SHIFT = 6
SEL_TOP = 16
WINDOW = 512
Q_BLOCK = 256
FORCE_BONUS = 1e4
NEG_INF = -1e30
MLSTM_HEADS = 4
MLSTM_HEAD_DIM = 128
CONV_WIDTH = 4
N_EXPERTS = 8
EPS = 1e-6

LANES = 128
SUBLANES = 8
VMEM_LIMIT = 56 * 1024 * 1024

ROW_TILE = 512
SEL_CHUNK = 512
MLSTM_CHUNK = 256
MOE_TILE = 512
GATHER_TILE = 256
DMA_UNROLL = 8

SM_GATE = 0
SM_I = 24
SM_F = 28


def _cparams(sem, vmem=VMEM_LIMIT):
    return pltpu.CompilerParams(dimension_semantics=sem, vmem_limit_bytes=vmem)


def _sigmoid(x):
    return 1.0 / (1.0 + jnp.exp(-x))


def _silu(x):
    return x * _sigmoid(x)


def _log_sigmoid(x):
    return jnp.minimum(x, 0.0) - jnp.log1p(jnp.exp(-jnp.abs(x)))


def _dot(a, b):
    return jnp.dot(a, b, preferred_element_type=F32)


def _dot_nt(a, b):
    return lax.dot_general(a, b, (((1,), (1,)), ((), ())), preferred_element_type=F32)


def _split3(x):
    hi = x.astype(BF16)
    r1 = x - hi.astype(F32)
    mid = r1.astype(BF16)
    lo = (r1 - mid.astype(F32)).astype(BF16)
    return hi, mid, lo


def _rmsnorm_mod(x, g, sc, sh):
    ms = jnp.mean(x * x, axis=-1, keepdims=True)
    y = x * lax.rsqrt(ms + EPS) * g
    return y * (1.0 + sc) + sh


def _adaln_kernel(c_ref, w_ref, b_ref, o_ref):
    c = c_ref[...]
    ca = _silu(c).astype(BF16)
    o_ref[0] = _dot(ca, w_ref[0].astype(BF16)) + b_ref[0]


def _adaln(c, ada_w, ada_b):
    depth, d, n6 = ada_w.shape
    b = c.shape[0]
    cp = jnp.zeros((SUBLANES, d), F32).at[:b].set(c)
    tn = n6 // 4
    out = pl.pallas_call(
        _adaln_kernel,
        grid=(depth, n6 // tn),
        in_specs=[
            pl.BlockSpec((SUBLANES, d), lambda l, j: (0, 0)),
            pl.BlockSpec((1, d, tn), lambda l, j: (l, 0, j)),
            pl.BlockSpec((1, 1, tn), lambda l, j: (l, 0, j)),
        ],
        out_specs=pl.BlockSpec((1, SUBLANES, tn), lambda l, j: (l, 0, j)),
        out_shape=jax.ShapeDtypeStruct((depth, SUBLANES, n6), F32),
        compiler_params=_cparams(("parallel", "parallel")),
        name="adaln",
    )(cp, ada_w, ada_b.reshape(depth, 1, n6))
    return out[:, :b]


def _inproj_kernel(x_ref, g_ref, sc_ref, sh_ref, w_ref, b_ref, cw_ref, cb_ref,
                   q_ref, kv_ref, sm_ref, qm_ref, km_ref, v_ref, o_ref, tail_scr, *, per_b):
    i = pl.program_id(0)

    @pl.when(i == 0)
    def _():
        tail_scr[...] = jnp.zeros_like(tail_scr)

    h = _rmsnorm_mod(x_ref[...], g_ref[...], sc_ref[0], sh_ref[0]).astype(BF16)

    def sec(lo, width):
        return _dot(h, w_ref[:, lo:lo + width]) + b_ref[:, lo:lo + width]

    q_ref[...] = (sec(0, 512) * (NSA_HEAD_DIM ** -0.5)).astype(BF16)
    kv_sm = sec(512, 7 * LANES)
    for s in range(6):
        kv_ref[s] = kv_sm[:, s * LANES:(s + 1) * LANES].astype(BF16)
    sm_ref[...] = kv_sm[:, 6 * LANES:7 * LANES]
    v_ref[...] = sec(2432, 512).astype(BF16)
    o_ref[...] = sec(2944, 512)

    cur = sec(1408, 1024)
    tm = cur.shape[0]
    prev = jnp.where(i % per_b > 0, tail_scr[...], 0.0)
    tail_scr[...] = cur[tm - SUBLANES:tm, :]
    row8 = lax.broadcasted_iota(I32, (SUBLANES, 1), 0)
    y = cb_ref[...]
    for tap in range(CONV_WIDTH):
        back = CONV_WIDTH - 1 - tap
        if back:
            rolled = pltpu.roll(cur, back, axis=0)
            top = jnp.where(row8 < back, pltpu.roll(prev, back, axis=0), rolled[0:SUBLANES])
            shifted = jnp.concatenate([top, rolled[SUBLANES:]], axis=0)
        else:
            shifted = cur
        y = y + shifted * cw_ref[tap:tap + 1, :]
    y = _silu(y)
    half = y.shape[1] // 2
    qm_ref[...] = y[:, :half].astype(BF16)
    km_ref[...] = (y[:, half:] * (MLSTM_HEAD_DIM ** -0.5)).astype(BF16)


def _inproj_weights(w_in, b_in):
    d = w_in.shape[0]
    o_q, o_kv, o_gate, o_qk, o_v, o_o, o_i, o_f = [int(v) for v in np.cumsum((0, 512, 768, 24, 1024, 512, 512, 4))]

    def cols(a):
        kv = a[..., o_kv:o_gate].reshape(a.shape[:-1] + (3, 2, NSA_KV_HEADS, NSA_HEAD_DIM))
        kv = jnp.swapaxes(kv, -3, -2)
        kv = kv.reshape(a.shape[:-1] + (768,))
        small = jnp.concatenate(
            [a[..., o_gate:o_qk], a[..., o_i:o_f], a[..., o_f:o_f + 4],
             jnp.zeros(a.shape[:-1] + (LANES - 32,), a.dtype)], axis=-1)
        return jnp.concatenate(
            [a[..., o_q:o_kv], kv, small, a[..., o_qk:o_v], a[..., o_v:o_o], a[..., o_o:o_i]], axis=-1)

    return cols(w_in).astype(BF16), cols(b_in)[None, :].astype(F32)


def _inproj(x2, g, sc, sh, w, b, conv_w, conv_b, seq):
    n, d = x2.shape
    tm = min(ROW_TILE, seq)
    per_b = seq // tm
    row = lambda i: (i, 0)
    bat = lambda i: (i // per_b, 0, 0)
    fix = lambda i: (0, 0)
    outs = (
        jax.ShapeDtypeStruct((n, 512), BF16),
        jax.ShapeDtypeStruct((6, n, LANES), BF16),
        jax.ShapeDtypeStruct((n, 128), F32),
        jax.ShapeDtypeStruct((n, 512), BF16),
        jax.ShapeDtypeStruct((n, 512), BF16),
        jax.ShapeDtypeStruct((n, 512), BF16),
        jax.ShapeDtypeStruct((n, 512), F32),
    )
    return pl.pallas_call(
        functools.partial(_inproj_kernel, per_b=per_b),
        grid=(n // tm,),
        in_specs=[
            pl.BlockSpec((tm, d), row),
            pl.BlockSpec((1, d), fix),
            pl.BlockSpec((1, 1, d), bat),
            pl.BlockSpec((1, 1, d), bat),
            pl.BlockSpec(w.shape, fix),
            pl.BlockSpec(b.shape, fix),
            pl.BlockSpec(conv_w.shape, fix),
            pl.BlockSpec((1, conv_w.shape[1]), fix),
        ],
        out_specs=(
            pl.BlockSpec((tm, 512), row),
            pl.BlockSpec((6, tm, LANES), lambda i: (0, i, 0)),
            pl.BlockSpec((tm, 128), row),
            pl.BlockSpec((tm, 512), row),
            pl.BlockSpec((tm, 512), row),
            pl.BlockSpec((tm, 512), row),
            pl.BlockSpec((tm, 512), row),
        ),
        out_shape=outs,
        scratch_shapes=[pltpu.VMEM((SUBLANES, 1024), F32)],
        compiler_params=_cparams(("arbitrary",)),
        name="inproj",
    )(x2, g, sc, sh, w, b, conv_w, conv_b[None, :])


def _compress_kernel(ch_ref, pe_ref, w1_ref, w2_ref, o_ref):
    ch = ch_ref[0].astype(F32)
    a0 = (ch + pe_ref[0:1, :]).astype(BF16)
    a1 = (ch + pe_ref[1:2, :]).astype(BF16)
    half = ch.shape[1]
    h0 = _dot(a0, w1_ref[0:half, :])
    h1 = _dot(a1, w1_ref[half:2 * half, :])
    n_chunk = ch.shape[0]
    hid = h0 + pltpu.roll(h1, n_chunk - 1, axis=0)
    o_ref[0] = _dot(_silu(hid).astype(BF16), w2_ref[...]).astype(BF16)


def _compress_weights(cmp_pe, cmp_w1, cmp_w2):
    dh, hid = NSA_HEAD_DIM, CMP_HIDDEN
    pe = jnp.concatenate([cmp_pe[0], cmp_pe[1]], axis=-1)
    pe = pe.reshape(2, CMP_STRIDE * LANES)
    w1 = cmp_w1.reshape(2, CMP_LEN, dh, hid)
    z = jnp.zeros((CMP_LEN, dh, hid), cmp_w1.dtype)
    wk = jnp.concatenate([w1[0], z], axis=1)
    wv = jnp.concatenate([z, w1[1]], axis=1)
    w1p = jnp.concatenate([wk, wv], axis=2).reshape(CMP_LEN * LANES, 2 * hid)
    z2 = jnp.zeros((hid, dh), cmp_w2.dtype)
    w2p = jnp.concatenate([jnp.concatenate([cmp_w2[0], z2], axis=1),
                           jnp.concatenate([z2, cmp_w2[1]], axis=1)], axis=0)
    return pe.astype(F32), w1p.astype(BF16), w2p.astype(BF16)


def _compress(kv_cmp, pe, w1p, w2p, batch, seq):
    g = kv_cmp.shape[0]
    n_chunk = seq // CMP_STRIDE
    ch = kv_cmp.reshape(g * batch, n_chunk, CMP_STRIDE * LANES)
    return pl.pallas_call(
        _compress_kernel,
        grid=(g * batch,),
        in_specs=[
            pl.BlockSpec((1, n_chunk, CMP_STRIDE * LANES), lambda i: (i, 0, 0)),
            pl.BlockSpec(pe.shape, lambda i: (0, 0)),
            pl.BlockSpec(w1p.shape, lambda i: (0, 0)),
            pl.BlockSpec(w2p.shape, lambda i: (0, 0)),
        ],
        out_specs=pl.BlockSpec((1, n_chunk, LANES), lambda i: (i, 0, 0)),
        out_shape=jax.ShapeDtypeStruct((g * batch, n_chunk, LANES), BF16),
        compiler_params=_cparams(("parallel",)),
        name="nsa_compress",
    )(ch, pe, w1p, w2p)


def _nsa2_kernel(slope_ref, q_ref, kc_ref, ks_ref, kw_ref, gate_ref, ovt_ref, feat_ref, featc_ref,
                 o_ref, kaug_c, vt_c, kaug_s, vt_s, kaug_w, vt_w, *, seq):
    g = pl.program_id(1)
    qb = pl.program_id(2)
    n_cmp = seq // CMP_STRIDE
    n_sel = seq // SEL_BLOCK
    n_top = min(SEL_TOP, n_sel)
    dh = NSA_HEAD_DIM
    cols = NSA_REP * Q_BLOCK

    @pl.when(qb == 0)
    def _():
        def build(src_ref, f_ref, kaug, vt, n_rows):
            step = min(SEL_CHUNK, n_rows)
            lane = lax.broadcasted_iota(I32, (step, LANES), 1)
            row = lax.broadcasted_iota(I32, (LANES, step), 0)
            for c0 in range(0, n_rows, step):
                x = src_ref[0, c0:c0 + step, :].astype(F32)
                kaug[c0:c0 + step, :] = jnp.where(lane < dh, x, f_ref[c0:c0 + step, :].astype(F32)).astype(BF16)
                vt[:, c0:c0 + step] = jnp.where(row == 0, 1.0, x.T).astype(BF16)

        build(kc_ref, featc_ref, kaug_c, vt_c, n_cmp)
        build(ks_ref, feat_ref, kaug_s, vt_s, seq)
        build(kw_ref, feat_ref, kaug_w, vt_w, seq)

    lane = lax.broadcasted_iota(I32, (Q_BLOCK, LANES), 1)
    blk_f = (lane - dh).astype(F32)
    t_lane = qb * Q_BLOCK + lax.broadcasted_iota(I32, (1, Q_BLOCK), 1)
    qf = q_ref[...].astype(F32)

    parts = []
    for r in range(NSA_REP):
        pair = qf[:, (r // 2) * LANES:(r // 2 + 1) * LANES]
        if r % 2:
            pair = pltpu.roll(pair, dh, axis=1)
        slope = slope_ref[g * NSA_REP + r]
        parts.append(jnp.where(lane < dh, pair, jnp.where(lane == dh, slope, slope * SEL_BLOCK * blk_f)))
    q_all_f = jnp.concatenate(parts, axis=0)
    q_all = q_all_f.astype(BF16)

    def tile4(x):
        return jnp.concatenate([x] * NSA_REP, axis=1)

    s = _dot_nt(kaug_c[...], q_all)
    end_c = lax.broadcasted_iota(I32, (n_cmp, Q_BLOCK), 0) * CMP_STRIDE + (CMP_LEN - 1)
    s = s + tile4(jnp.where(end_c <= t_lane, 0.0, NEG_INF))
    e = jnp.exp(s - jnp.max(s, axis=0, keepdims=True))
    p_c = e / jnp.sum(e, axis=0, keepdims=True)
    p_c = p_c * tile4((t_lane >= CMP_LEN - 1).astype(F32))
    o_c = _dot(vt_c[...], p_c.astype(BF16))

    p4 = p_c[:, 0:Q_BLOCK]
    for r in range(1, NSA_REP):
        p4 = p4 + p_c[:, r * Q_BLOCK:(r + 1) * Q_BLOCK]
    ovt = ovt_ref[...]
    imp = sum(_dot(ovt, piece) for piece in _split3(p4))
    j_col = lax.broadcasted_iota(I32, (n_sel, 1), 0)
    cur = jnp.right_shift(t_lane, SEL_SHIFT)
    forced = (j_col == 0) | (j_col == cur) | (j_col == cur - 1)
    imp = jnp.where(forced, imp + FORCE_BONUS, imp)
    imp = jnp.where(j_col <= cur, imp, -1.0)
    groups = [imp[v * SUBLANES:(v + 1) * SUBLANES, :] for v in range(n_sel // SUBLANES)]
    j_grp = lax.broadcasted_iota(I32, (SUBLANES, Q_BLOCK), 0)
    ranks = [jnp.zeros((SUBLANES, Q_BLOCK), F32) for _ in groups]
    for k in range(n_sel):
        row_k = groups[k // SUBLANES][k % SUBLANES:k % SUBLANES + 1, :]
        for v, grp in enumerate(groups):
            ge = jnp.where(row_k >= grp, 1.0, 0.0)
            gt = jnp.where(row_k > grp, 1.0, 0.0)
            if v * SUBLANES > k:
                inc = ge
            elif (v + 1) * SUBLANES - 1 < k:
                inc = gt
            else:
                inc = jnp.where(j_grp + v * SUBLANES > k, ge, gt)
            ranks[v] = ranks[v] + inc
    sel_t = jnp.where(jnp.concatenate(ranks, axis=0) < n_top, 1.0, 0.0)
    pad_lo = jnp.zeros((dh, Q_BLOCK), F32)
    pieces = [pad_lo, sel_t]
    if n_sel < dh:
        pieces.append(jnp.zeros((dh - n_sel, Q_BLOCK), F32))
    selmat = jnp.concatenate(pieces, axis=0).T
    drop = jnp.concatenate([jnp.where(lane > dh, selmat, 1.0)] * NSA_REP, axis=0) < 0.5
    q_sel = jnp.where(drop, NEG_INF, q_all_f).astype(BF16)

    def sel_scores(c):
        start = pl.multiple_of(c * SEL_CHUNK, SEL_CHUNK)
        return _dot_nt(kaug_s[pl.ds(start, SEL_CHUNK), :], q_sel)

    def sel_update(c, s, carry):
        m, acc = carry
        start = pl.multiple_of(c * SEL_CHUNK, SEL_CHUNK)
        m_new = jnp.maximum(m, jnp.max(s, axis=0, keepdims=True))
        p = jnp.exp(s - m_new).astype(BF16)
        acc = jnp.exp(m - m_new) * acc + _dot(vt_s[:, pl.ds(start, SEL_CHUNK)], p)
        return m_new, acc

    def sel_pair(i, carry):
        s0, s1 = sel_scores(2 * i), sel_scores(2 * i + 1)
        return sel_update(2 * i + 1, s1, sel_update(2 * i, s0, carry))

    def sel_single(c, carry):
        return sel_update(c, sel_scores(c), carry)

    last = (qb * Q_BLOCK) // SEL_CHUNK
    init = (jnp.full((1, cols), NEG_INF, F32), jnp.zeros((LANES, cols), F32))
    carry = lax.fori_loop(0, last // 2, sel_pair, init)
    carry = lax.fori_loop(2 * (last // 2), last, sel_single, carry)
    pos = last * SEL_CHUNK + lax.broadcasted_iota(I32, (SEL_CHUNK, Q_BLOCK), 0)
    s = sel_scores(last) + tile4(jnp.where(pos <= t_lane, 0.0, NEG_INF))
    _, acc = sel_update(last, s, carry)
    o_s = acc / acc[0:1, :]

    span = Q_BLOCK + WINDOW
    start = pl.multiple_of(jnp.maximum(qb * Q_BLOCK - WINDOW, 0), Q_BLOCK)
    s = _dot_nt(kaug_w[pl.ds(start, span), :], q_all)
    dist = t_lane - (start + lax.broadcasted_iota(I32, (span, Q_BLOCK), 0))
    in_band = pltpu.bitcast(dist, jnp.uint32) < WINDOW
    s = s + tile4(jnp.where(in_band, 0.0, NEG_INF))
    p = jnp.exp(s - jnp.max(s, axis=0, keepdims=True)).astype(BF16)
    acc = _dot(vt_w[:, pl.ds(start, span)], p)
    o_w = acc / acc[0:1, :]

    gates = _sigmoid(gate_ref[0])
    mixed = []
    for r in range(NSA_REP):
        cs = slice(r * Q_BLOCK, (r + 1) * Q_BLOCK)
        mixed.append(gates[3 * r:3 * r + 1, :] * o_c[dh:, cs]
                     + gates[3 * r + 1:3 * r + 2, :] * o_s[dh:, cs]
                     + gates[3 * r + 2:3 * r + 3, :] * o_w[dh:, cs])
    for pr in range(NSA_REP // 2):
        pair_t = jnp.concatenate([mixed[2 * pr], mixed[2 * pr + 1]], axis=0)
        o_ref[:, pr * LANES:(pr + 1) * LANES] = pair_t.T.astype(o_ref.dtype)


def _nsa_constants(seq):
    n_cmp, n_sel = seq // CMP_STRIDE, seq // SEL_BLOCK
    assert n_sel <= NSA_HEAD_DIM, "one feature lane per selection block"
    slopes = 2.0 ** (-8.0 * np.arange(1, NSA_HEADS + 1) / NSA_HEADS)
    far = (slopes[:, None] * SEL_BLOCK * np.arange(n_sel)[None, :]).astype(np.float32)
    assert np.array_equal(far.astype(BF16).astype(np.float32), far), "ALiBi features must be exact in bf16"
    lo_c = np.arange(n_cmp)[:, None] * CMP_STRIDE
    lo_s = np.arange(n_sel)[None, :] * SEL_BLOCK
    ov = np.clip(np.minimum(lo_c + CMP_LEN, lo_s + SEL_BLOCK) - np.maximum(lo_c, lo_s), 0, None) / CMP_LEN
    ov[n_cmp - 1] = 0.0

    def feats(pos):
        f = np.zeros((pos.shape[0], LANES), np.float32)
        f[:, NSA_HEAD_DIM] = pos % SEL_BLOCK
        blk = pos // SEL_BLOCK
        ok = (blk >= 1) & (blk < NSA_HEAD_DIM)
        f[np.nonzero(ok)[0], NSA_HEAD_DIM + blk[ok]] = 1.0
        return f

    feat = feats(np.arange(seq))
    pos_c = np.arange(n_cmp) * CMP_STRIDE + CMP_LEN - 1
    featc = feats(pos_c)
    featc[pos_c >= seq] = 0.0
    return (jnp.asarray(slopes, F32), jnp.asarray(ov.T, BF16), jnp.asarray(feat, BF16), jnp.asarray(featc, BF16))


def _nsa2(q, kvp, kc, gates_t, batch, seq):
    n = q.shape[0]
    n_qb = seq // Q_BLOCK
    n_cmp = seq // CMP_STRIDE
    slopes, ovt, feat, featc = _nsa_constants(seq)
    kv5 = kvp.reshape(3, NSA_KV_HEADS, batch, seq, LANES)
    ks = kv5[1].reshape(NSA_KV_HEADS * batch, seq, LANES)
    kw = kv5[2].reshape(NSA_KV_HEADS * batch, seq, LANES)
    fix = lambda b, g, i, s: (0, 0)
    per_bg = lambda b, g, i, s: (g * batch + b, 0, 0)
    grid_spec = pltpu.PrefetchScalarGridSpec(
        num_scalar_prefetch=1,
        grid=(batch, NSA_KV_HEADS, n_qb),
        in_specs=[
            pl.BlockSpec((Q_BLOCK, 2 * LANES), lambda b, g, i, s: (b * n_qb + i, g)),
            pl.BlockSpec((1, n_cmp, LANES), per_bg),
            pl.BlockSpec((1, seq, LANES), per_bg),
            pl.BlockSpec((1, seq, LANES), per_bg),
            pl.BlockSpec((1, 16, Q_BLOCK), lambda b, g, i, s: (b * NSA_KV_HEADS + g, 0, i)),
            pl.BlockSpec(ovt.shape, fix),
            pl.BlockSpec(feat.shape, fix),
            pl.BlockSpec(featc.shape, fix),
        ],
        out_specs=pl.BlockSpec((Q_BLOCK, 2 * LANES), lambda b, g, i, s: (b * n_qb + i, g)),
        scratch_shapes=[
            pltpu.VMEM((n_cmp, LANES), BF16), pltpu.VMEM((LANES, n_cmp), BF16),
            pltpu.VMEM((seq, LANES), BF16), pltpu.VMEM((LANES, seq), BF16),
            pltpu.VMEM((seq, LANES), BF16), pltpu.VMEM((LANES, seq), BF16),
        ],
    )
    return pl.pallas_call(
        functools.partial(_nsa2_kernel, seq=seq),
        grid_spec=grid_spec,
        out_shape=jax.ShapeDtypeStruct((n, 512), BF16),
        compiler_params=_cparams(("parallel", "parallel", "arbitrary")),
        name="nsa_attention",
    )(slopes, q, kc, ks, kw, gates_t, ovt, feat, featc)


def _mlstm_kernel(q_ref, k_ref, v_ref, o_ref, sm_ref, g_ref, y_ref, c_scr, n_scr, m_scr):
    c_idx = pl.program_id(1)
    lc = q_ref.shape[0]
    d = MLSTM_HEAD_DIM

    @pl.when(c_idx == 0)
    def _():
        c_scr[...] = jnp.zeros_like(c_scr)
        n_scr[...] = jnp.zeros_like(n_scr)
        m_scr[...] = jnp.zeros_like(m_scr)

    sm = sm_ref[...]
    lf = _log_sigmoid(sm)
    ri = lax.broadcasted_iota(I32, (lc, lc), 0)
    ci = lax.broadcasted_iota(I32, (lc, lc), 1)
    tri = jnp.where(ri >= ci, 1.0, 0.0).astype(BF16)
    a_col = sum(_dot(tri, piece) for piece in _split3(lf))
    a_row = a_col.T

    for h in range(MLSTM_HEADS):
        hs = slice(h * d, (h + 1) * d)
        a_j = a_row[SM_F + h:SM_F + h + 1, :]
        gap_s = sm[:, SM_I + h:SM_I + h + 1] - a_col[:, SM_F + h:SM_F + h + 1]
        m_prev = m_scr[h:h + 1, 0:1]
        qh, kh, vh = q_ref[:, hs], k_ref[:, hs], v_ref[:, hs]
        v_t = vh.astype(F32).T.astype(BF16)
        c_prev = c_scr[h]
        n_prev = n_scr[h:h + 1, :]

        log_d = jnp.where(ri <= ci, a_j + gap_s, -jnp.inf)
        m_inter = a_j + m_prev
        m_t = jnp.maximum(m_inter, jnp.max(log_d, axis=0, keepdims=True))
        d_mat = jnp.exp(log_d - m_t)
        inter = jnp.exp(m_inter - m_t)
        s_qk = _dot_nt(kh, qh) * d_mat
        num = inter * _dot_nt(c_prev.astype(BF16), qh) + _dot(v_t, s_qk.astype(BF16))
        n_rows = jnp.broadcast_to(n_prev, (SUBLANES, d)).astype(BF16)
        den = inter * _dot_nt(n_rows, qh)[0:1, :] + jnp.sum(s_qk, axis=0, keepdims=True)
        hh = (num / jnp.maximum(jnp.abs(den), jnp.exp(-m_t))).T

        a_last = a_j[:, lc - 1:lc]
        log_w = a_last + gap_s
        m_new = jnp.maximum(a_last + m_prev, jnp.max(log_w, axis=0, keepdims=True))
        wk = jnp.exp(log_w - m_new) * kh.astype(F32)
        decay = jnp.exp(a_last + m_prev - m_new)
        c_scr[h] = decay * c_prev + _dot(v_t, wk.astype(BF16))
        n_scr[h:h + 1, :] = decay * n_prev + jnp.sum(wk, axis=0, keepdims=True)
        m_scr[h:h + 1, :] = jnp.broadcast_to(m_new, (1, LANES))

        hg = _sigmoid(o_ref[:, hs]) * hh
        hn = hg * lax.rsqrt(jnp.mean(hg * hg, axis=-1, keepdims=True) + EPS)
        y_ref[:, hs] = (hn * g_ref[:, hs]).astype(y_ref.dtype)


def _mlstm(qm, km, vm, om, sm, norm_g, batch, seq):
    n, w = qm.shape
    lc = min(MLSTM_CHUNK, seq)
    per_b = seq // lc
    row = lambda b, c: (b * per_b + c, 0)
    return pl.pallas_call(
        _mlstm_kernel,
        grid=(batch, per_b),
        in_specs=[
            pl.BlockSpec((lc, w), row),
            pl.BlockSpec((lc, w), row),
            pl.BlockSpec((lc, w), row),
            pl.BlockSpec((lc, w), row),
            pl.BlockSpec((lc, LANES), row),
            pl.BlockSpec((1, w), lambda b, c: (0, 0)),
        ],
        out_specs=pl.BlockSpec((lc, w), row),
        out_shape=jax.ShapeDtypeStruct((n, w), BF16),
        scratch_shapes=[
            pltpu.VMEM((MLSTM_HEADS, MLSTM_HEAD_DIM, MLSTM_HEAD_DIM), F32),
            pltpu.VMEM((SUBLANES, MLSTM_HEAD_DIM), F32),
            pltpu.VMEM((SUBLANES, LANES), F32),
        ],
        compiler_params=_cparams(("parallel", "arbitrary")),
        name="mlstm",
    )(qm, km, vm, om, sm, norm_g[None, :])


def _outproj_kernel(*refs, with_router):
    if with_router:
        (x_ref, ya_ref, yb_ref, w_ref, g1_ref, gn_ref, sc_ref, sh_ref, rw_ref,
         x1_ref, h_ref, lg_ref) = refs
    else:
        x_ref, ya_ref, yb_ref, w_ref, g1_ref, gn_ref, sc_ref, sh_ref, x1_ref, h_ref = refs
    half = ya_ref.shape[1]
    y = _dot(ya_ref[...], w_ref[0:half, :]) + _dot(yb_ref[...], w_ref[half:2 * half, :])
    x1 = x_ref[...] + g1_ref[0] * y
    x1_ref[...] = x1
    h = _rmsnorm_mod(x1, gn_ref[...], sc_ref[0], sh_ref[0])
    h_ref[...] = h.astype(h_ref.dtype)
    if with_router:
        lg_ref[...] = lax.dot_general(rw_ref[...], h, (((1,), (1,)), ((), ())),
                                      precision=lax.Precision.HIGHEST, preferred_element_type=F32)


def _outproj(x2, ya, yb, w_out, g1, gn, sc, sh, seq, router_wt=None):
    n, d = x2.shape
    tm = min(ROW_TILE, seq)
    per_b = seq // tm
    row = lambda i: (i, 0)
    bat = lambda i: (i // per_b, 0, 0)
    fix = lambda i: (0, 0)
    with_router = router_wt is not None
    in_specs = [
        pl.BlockSpec((tm, d), row),
        pl.BlockSpec((tm, ya.shape[1]), row),
        pl.BlockSpec((tm, yb.shape[1]), row),
        pl.BlockSpec(w_out.shape, fix),
        pl.BlockSpec((1, 1, d), bat),
        pl.BlockSpec((1, d), fix),
        pl.BlockSpec((1, 1, d), bat),
        pl.BlockSpec((1, 1, d), bat),
    ]
    out_specs = [pl.BlockSpec((tm, d), row), pl.BlockSpec((tm, d), row)]
    out_shape = [jax.ShapeDtypeStruct((n, d), F32),
                 jax.ShapeDtypeStruct((n, d), F32 if with_router else BF16)]
    args = [x2, ya, yb, w_out, g1, gn, sc, sh]
    if with_router:
        in_specs.append(pl.BlockSpec(router_wt.shape, fix))
        out_specs.append(pl.BlockSpec((N_EXPERTS, tm), lambda i: (0, i)))
        out_shape.append(jax.ShapeDtypeStruct((N_EXPERTS, n), F32))
        args.append(router_wt)
    return pl.pallas_call(
        functools.partial(_outproj_kernel, with_router=with_router),
        grid=(n // tm,),
        in_specs=in_specs,
        out_specs=tuple(out_specs),
        out_shape=tuple(out_shape),
        compiler_params=_cparams(("parallel",)),
        name="outproj_router" if with_router else "outproj",
    )(*args)


def _ffn_kernel(h_ref, x_ref, wg_ref, wu_ref, wd_ref, g2_ref, o_ref, act_scr, *, tf):
    h = h_ref[...]
    d_ff = wg_ref.shape[1]
    for j in range(d_ff // tf):
        cs = slice(j * tf, (j + 1) * tf)
        act_scr[:, cs] = (_silu(_dot(h, wg_ref[:, cs])) * _dot(h, wu_ref[:, cs])).astype(BF16)
    o_ref[...] = x_ref[...] + g2_ref[0] * _dot(act_scr[...], wd_ref[...])


def _ffn(h, x1, wg, wu, wd, g2, seq):
    n, d = x1.shape
    d_ff = wg.shape[1]
    tm = min(ROW_TILE, seq)
    per_b = seq // tm
    tf = 256
    row = lambda i: (i, 0)
    fix = lambda i: (0, 0)
    once = dict(pipeline_mode=pl.Buffered(1))
    return pl.pallas_call(
        functools.partial(_ffn_kernel, tf=tf),
        grid=(n // tm,),
        in_specs=[
            pl.BlockSpec((tm, d), row),
            pl.BlockSpec((tm, d), row),
            pl.BlockSpec(wg.shape, fix, **once),
            pl.BlockSpec(wu.shape, fix, **once),
            pl.BlockSpec(wd.shape, fix, **once),
            pl.BlockSpec((1, 1, d), lambda i: (i // per_b, 0, 0)),
        ],
        out_specs=pl.BlockSpec((tm, d), row),
        out_shape=jax.ShapeDtypeStruct((n, d), F32),
        scratch_shapes=[pltpu.VMEM((tm, d_ff), BF16)],
        compiler_params=_cparams(("parallel",)),
        name="dense_ffn",
    )(h, x1, wg, wu, wd, g2)


def _route_kernel(lg_ref, dest_ref, wt_ref, meta_ref, cnt_scr, exc_scr, *, tile):
    n_e, n = lg_ref.shape
    lg = lg_ref[...]
    e_iota = lax.broadcasted_iota(I32, (n_e, n), 0)
    m1 = jnp.max(lg, axis=0, keepdims=True)
    e0 = jnp.min(jnp.where(lg == m1, e_iota, n_e), axis=0, keepdims=True)
    lg2 = jnp.where(e_iota == e0, -jnp.inf, lg)
    m2 = jnp.max(lg2, axis=0, keepdims=True)
    e1 = jnp.min(jnp.where(lg2 == m2, e_iota, n_e), axis=0, keepdims=True)
    ex = jnp.exp(m2 - m1)
    wt_ref[0:1, :] = 1.0 / (1.0 + ex)
    wt_ref[1:2, :] = ex / (1.0 + ex)
    oh0 = e_iota == e0
    oh1 = e_iota == e1
    cnt_scr[...] = jnp.where(oh0, 1.0, 0.0) + jnp.where(oh1, 1.0, 0.0)

    ri = lax.broadcasted_iota(I32, (LANES, 2 * LANES), 0)
    ci = lax.broadcasted_iota(I32, (LANES, 2 * LANES), 1)
    prefix_total = jnp.where((ci >= LANES) | (ri < ci), 1.0, 0.0).astype(BF16)

    def block(kb, carry):
        ls = pl.ds(pl.multiple_of(kb * LANES, LANES), LANES)
        both = _dot(cnt_scr[:, ls].astype(BF16), prefix_total)
        exc_scr[:, ls] = both[:, :LANES] + carry
        return carry + both[:, LANES:]

    total = lax.fori_loop(0, n // LANES, block, jnp.zeros((n_e, LANES), F32))
    padded = jnp.floor((total + (tile - 1)) / tile) * tile
    e_col = lax.broadcasted_iota(I32, (n_e, LANES), 0)
    starts = jnp.zeros((n_e, LANES), F32)
    for e in range(n_e - 1):
        starts = starts + jnp.where(e_col > e, padded[e:e + 1, :], 0.0)
    ends = starts + padded
    slot = starts[:, 0:1] + exc_scr[...]
    dest_ref[0:1, :] = jnp.sum(jnp.where(oh0, slot, 0.0), axis=0, keepdims=True).astype(I32)
    dest_ref[1:2, :] = jnp.sum(jnp.where(oh1, slot, 0.0), axis=0, keepdims=True).astype(I32)
    blk_start = (lax.broadcasted_iota(I32, (n_e, LANES), 1) * tile).astype(F32)
    blk_exp = jnp.sum(jnp.where(ends <= blk_start, 1.0, 0.0), axis=0, keepdims=True)
    meta_ref[0:1, :] = jnp.minimum(blk_exp, n_e - 1.0).astype(I32)
    meta_ref[1:2, :] = (ends[n_e - 1:n_e, :] / tile).astype(I32)
    on_diag = lax.broadcasted_iota(I32, (n_e, LANES), 1) == e_col
    meta_ref[2:3, :] = jnp.sum(jnp.where(on_diag, ends / tile, 0.0), axis=0, keepdims=True).astype(I32)
    meta_ref[3:4, :] = jnp.sum(jnp.where(on_diag, starts / tile, 0.0), axis=0, keepdims=True).astype(I32)
    meta_ref[4:SUBLANES, :] = jnp.zeros((SUBLANES - 4, LANES), I32)


def _route(logits_t, tile):
    n_e, n = logits_t.shape
    return pl.pallas_call(
        functools.partial(_route_kernel, tile=tile),
        out_shape=(jax.ShapeDtypeStruct((2, n), I32), jax.ShapeDtypeStruct((2, n), F32),
                   jax.ShapeDtypeStruct((SUBLANES, LANES), I32)),
        scratch_shapes=[pltpu.VMEM((n_e, n), F32), pltpu.VMEM((n_e, n), F32)],
        compiler_params=pltpu.CompilerParams(vmem_limit_bytes=VMEM_LIMIT),
        name="moe_route",
    )(logits_t)


def _dispatch_kernel(dest_ref, meta_ref, h_ref, xs_ref, zero_scr, sem, zsem, *, tile):
    i = pl.program_id(0)
    td = h_ref.shape[0]

    @pl.when(i == 0)
    def _():
        zero_scr[...] = jnp.zeros_like(zero_scr)
        n_blk = xs_ref.shape[0] // tile

        def zero_block(b):
            return pltpu.make_async_copy(zero_scr, xs_ref.at[pl.ds(pl.multiple_of(b * tile, tile), tile)], zsem)

        def each_block(fn):
            for e in range(N_EXPERTS):
                @pl.when(meta_ref[2, e] > meta_ref[3, e])
                def _():
                    fn(zero_block(meta_ref[2, e] - 1))

            def tail(b, _):
                fn(zero_block(b))
                return 0
            lax.fori_loop(meta_ref[1, 0], n_blk, tail, 0)

        each_block(lambda c: c.start())
        each_block(lambda c: c.wait())

    def copy(r, k):
        row = dest_ref[k * (pl.num_programs(0) * td) + i * td + r]
        return pltpu.make_async_copy(h_ref.at[pl.ds(r, 1)], xs_ref.at[pl.ds(row, 1)], sem)

    def start(r, _):
        copy(r, 0).start()
        copy(r, 1).start()
        return 0

    lax.fori_loop(0, td, start, 0, unroll=DMA_UNROLL)
    for _ in range(2):
        pltpu.make_async_copy(h_ref, xs_ref.at[pl.ds(0, td)], sem).wait()


def _dispatch(dest, meta, h, n_rows, tile):
    n, d = h.shape
    td = min(GATHER_TILE, n)
    grid_spec = pltpu.PrefetchScalarGridSpec(
        num_scalar_prefetch=2,
        grid=(n // td,),
        in_specs=[pl.BlockSpec((td, d), lambda i, s, m: (i, 0))],
        out_specs=pl.BlockSpec(memory_space=pl.ANY),
        scratch_shapes=[pltpu.VMEM((tile, d), h.dtype), pltpu.SemaphoreType.DMA(()),
                        pltpu.SemaphoreType.DMA(())],
    )
    return pl.pallas_call(
        functools.partial(_dispatch_kernel, tile=tile),
        grid_spec=grid_spec,
        out_shape=jax.ShapeDtypeStruct((n_rows, d), h.dtype),
        compiler_params=_cparams(("arbitrary",)),
        name="moe_dispatch",
    )(dest.reshape(-1), meta, h)


def _expert_kernel(meta_ref, x_ref, wg_ref, wu_ref, wd_ref, y_ref, xb_scr):
    i = pl.program_id(0)
    j = pl.program_id(1)

    @pl.when(i < meta_ref[1, 0])
    def _():
        @pl.when(j == 0)
        def _():
            xb_scr[...] = x_ref[...].astype(BF16)

        xb = xb_scr[...]
        act = (_silu(_dot(xb, wg_ref[0])) * _dot(xb, wu_ref[0])).astype(BF16)
        part = _dot(act, wd_ref[0])

        @pl.when(j == 0)
        def _():
            y_ref[...] = part

        @pl.when(j > 0)
        def _():
            y_ref[...] += part

    @pl.when((i >= meta_ref[1, 0]) & (j == 0))
    def _():
        y_ref[...] = jnp.zeros_like(y_ref)


def _experts(meta, xs, wg, wu, wd, tile):
    n_rows, d = xs.shape
    d_ff = wg.shape[2]
    tf = d_ff // 2 if (d_ff // 2) % (2 * LANES) == 0 else d_ff
    n_blk = n_rows // tile

    def blk(i, s):
        return jnp.minimum(i, s[1, 0] - 1)

    grid_spec = pltpu.PrefetchScalarGridSpec(
        num_scalar_prefetch=1,
        grid=(n_blk, d_ff // tf),
        in_specs=[
            pl.BlockSpec((tile, d), lambda i, j, s: (blk(i, s), 0)),
            pl.BlockSpec((1, d, tf), lambda i, j, s: (s[0, blk(i, s)], 0, jnp.where(i < s[1, 0], j, 0))),
            pl.BlockSpec((1, d, tf), lambda i, j, s: (s[0, blk(i, s)], 0, jnp.where(i < s[1, 0], j, 0))),
            pl.BlockSpec((1, tf, d), lambda i, j, s: (s[0, blk(i, s)], jnp.where(i < s[1, 0], j, 0), 0)),
        ],
        out_specs=pl.BlockSpec((tile, d), lambda i, j, s: (i, 0)),
        scratch_shapes=[pltpu.VMEM((tile, d), BF16)],
    )
    return pl.pallas_call(
        _expert_kernel,
        grid_spec=grid_spec,
        out_shape=jax.ShapeDtypeStruct((n_rows, d), F32),
        compiler_params=_cparams(("arbitrary", "arbitrary")),
        name="moe_experts",
    )(meta, xs, wg, wu, wd)


def _combine_kernel(dest_ref, y_ref, x_ref, wt_ref, g2_ref, gn_ref, o_ref, buf, sem):
    i = pl.program_id(0)
    n_steps = pl.num_programs(0)
    tc = x_ref.shape[0]
    slot = i % 2

    def gather(step, to_slot):
        def start(r, _):
            for k in range(2):
                row = dest_ref[k * (n_steps * tc) + step * tc + r]
                pltpu.make_async_copy(y_ref.at[pl.ds(row, 1)], buf.at[to_slot, k, pl.ds(r, 1)],
                                      sem.at[to_slot]).start()
            return 0
        lax.fori_loop(0, tc, start, 0, unroll=DMA_UNROLL)

    @pl.when(i == 0)
    def _():
        gather(0, 0)

    @pl.when(i + 1 < n_steps)
    def _():
        gather(i + 1, 1 - slot)

    for k in range(2):
        pltpu.make_async_copy(y_ref.at[pl.ds(0, tc)], buf.at[slot, k], sem.at[slot]).wait()
    wt = wt_ref[...]
    y = buf[slot, 0] * wt[:, 0:1] + buf[slot, 1] * wt[:, 1:2]
    x = x_ref[...] + g2_ref[0] * y
    ms = jnp.mean(x * x, axis=-1, keepdims=True)
    o_ref[...] = x * lax.rsqrt(ms + EPS) * gn_ref[...]


def _combine(dest, y, x1, wt, g2, gn, seq):
    n, d = x1.shape
    tc = min(GATHER_TILE, seq)
    per_b = seq // tc
    grid_spec = pltpu.PrefetchScalarGridSpec(
        num_scalar_prefetch=1,
        grid=(n // tc,),
        in_specs=[
            pl.BlockSpec(memory_space=pl.ANY),
            pl.BlockSpec((tc, d), lambda i, s: (i, 0)),
            pl.BlockSpec((tc, 2), lambda i, s: (i, 0)),
            pl.BlockSpec((1, 1, d), lambda i, s: (i // per_b, 0, 0)),
            pl.BlockSpec((1, d), lambda i, s: (0, 0)),
        ],
        out_specs=pl.BlockSpec((tc, d), lambda i, s: (i, 0)),
        scratch_shapes=[pltpu.VMEM((2, 2, tc, d), F32), pltpu.SemaphoreType.DMA((2,))],
    )
    return pl.pallas_call(
        _combine_kernel,
        grid_spec=grid_spec,
        out_shape=jax.ShapeDtypeStruct((n, d), F32),
        compiler_params=_cparams(("arbitrary",)),
        name="moe_combine_norm",
    )(dest.reshape(-1), y, x1, wt, g2, gn)


def _final_norm_kernel(x_ref, g_ref, o_ref):
    x = x_ref[...]
    ms = jnp.mean(x * x, axis=-1, keepdims=True)
    o_ref[...] = x * lax.rsqrt(ms + EPS) * g_ref[...]


def _mixer(x2, mod, l, batch, seq, norm_mix_g, w_in, b_in, cmp_pe, cmp_w1, cmp_w2,
           conv_w, conv_b, mlstm_norm_g):
    d = x2.shape[1]
    sh1, sc1 = mod[l, :, 0:d], mod[l, :, d:2 * d]
    w, b = _inproj_weights(w_in[l], b_in[l])
    q, kvp, sm, qm, km, vm, om = _inproj(x2, norm_mix_g[l][None, :], sc1[:, None, :], sh1[:, None, :], w, b,
                                         conv_w[l], conv_b[l], seq)
    pe, w1p, w2p = _compress_weights(cmp_pe[l], cmp_w1[l], cmp_w2[l])
    kc = _compress(kvp[0:NSA_KV_HEADS], pe, w1p, w2p, batch, seq)
    gates = sm[:, SM_GATE:SM_GATE + 24].reshape(batch, seq, NSA_KV_HEADS, NSA_REP * 3)
    gates = jnp.pad(gates.transpose(0, 2, 3, 1), ((0, 0), (0, 0), (0, 16 - NSA_REP * 3), (0, 0)))
    y_nsa = _nsa2(q, kvp, kc, gates.reshape(batch * NSA_KV_HEADS, 16, seq), batch, seq)
    y_ml = _mlstm(qm, km, vm, om, sm, mlstm_norm_g[l], batch, seq)
    return y_nsa, y_ml


def kernel(x, c, ada_w, ada_b, norm_mix_g, norm_ffn_g, w_in, b_in, cmp_pe, cmp_w1, cmp_w2, conv_w, conv_b, mlstm_norm_g, w_out, ffn_w_gate, ffn_w_up, ffn_w_down, router_w, moe_w_gate, moe_w_up, moe_w_down, final_norm_g):
    batch, seq, d = x.shape
    depth = ada_w.shape[0]
    n = batch * seq
    mod = _adaln(c, ada_w, ada_b)
    x2 = x.reshape(n, d)
    for l in range(depth):
        g1 = mod[l, :, 2 * d:3 * d][:, None, :]
        sh2 = mod[l, :, 3 * d:4 * d][:, None, :]
        sc2 = mod[l, :, 4 * d:5 * d][:, None, :]
        g2 = mod[l, :, 5 * d:6 * d][:, None, :]
        y_nsa, y_ml = _mixer(x2, mod, l, batch, seq, norm_mix_g, w_in, b_in, cmp_pe, cmp_w1, cmp_w2,
                             conv_w, conv_b, mlstm_norm_g)
        gn = norm_ffn_g[l][None, :]
        i = l // 2
        last = l == depth - 1
        if l % 2 == 0:
            x1, h = _outproj(x2, y_nsa, y_ml, w_out[l].astype(BF16), g1, gn, sc2, sh2, seq)
            x2 = _ffn(h, x1, ffn_w_gate[i].astype(BF16), ffn_w_up[i].astype(BF16),
                      ffn_w_down[i].astype(BF16), g2, seq)
            if last:
                x2 = _final_norm(x2, final_norm_g)
        else:
            x1, h, logits_t = _outproj(x2, y_nsa, y_ml, w_out[l].astype(BF16), g1, gn, sc2, sh2, seq,
                                       router_wt=router_w[i].T)
            n_rows = 2 * n + N_EXPERTS * MOE_TILE
            dest, wt, meta = _route(logits_t, MOE_TILE)
            xs = _dispatch(dest, meta, h, n_rows, MOE_TILE)
            y = _experts(meta, xs, moe_w_gate[i].astype(BF16), moe_w_up[i].astype(BF16),
                         moe_w_down[i].astype(BF16), MOE_TILE)
            unit = jnp.ones((1, d), F32)
            x2 = _combine(dest, y, x1, wt.T, g2, final_norm_g[None, :] if last else unit, seq)
            if not last:
                raise NotImplementedError("a MoE layer that is not the last layer")
    return x2.reshape(batch, seq, d)


def _final_norm(x2, g):
    n, d = x2.shape
    tm = min(ROW_TILE, n)
    return pl.pallas_call(
        _final_norm_kernel,
        grid=(n // tm,),
        in_specs=[pl.BlockSpec((tm, d), lambda i: (i, 0)), pl.BlockSpec((1, d), lambda i: (0, 0))],
        out_specs=pl.BlockSpec((tm, d), lambda i: (i, 0)),
        out_shape=jax.ShapeDtypeStruct((n, d), F32),
        compiler_params=_cparams(("parallel",)),
        name="final_norm",
    )(x2, g[None, :])
```

```python
import functools

import numpy as np
import jax
import jax.numpy as jnp
from jax import lax
from jax.experimental import pallas as pl
from jax.experimental.pallas import tpu as pltpu

F32 = jnp.float32
BF16 = jnp.bfloat16
I32 = jnp.int32

NSA_HEADS = 8
NSA_KV_HEADS = 2
NSA_REP = NSA_HEADS // NSA_KV_HEADS
NSA_HEAD_DIM = 64
CMP_LEN = 32
CMP_STRIDE = 16
CMP_HIDDEN = 128
SEL_BLOCK = 64
SEL_SHIFT = 6
SEL_TOP = 16
WINDOW = 512
Q_BLOCK = 256
FORCE_BONUS = 1e4
NEG_INF = -1e30
MLSTM_HEADS = 4
MLSTM_HEAD_DIM = 128
CONV_WIDTH = 4
N_EXPERTS = 8
EPS = 1e-6

LANES = 128
SUBLANES = 8
VMEM_LIMIT = 56 * 1024 * 1024

ROW_TILE = 512
SEL_CHUNK = 512
MLSTM_CHUNK = 256
MOE_TILE = 512
GATHER_TILE = 256
DMA_UNROLL = 8

SM_GATE = 0
SM_I = 24
SM_F = 28


def _cparams(sem, vmem=VMEM_LIMIT):
    return pltpu.CompilerParams(dimension_semantics=sem, vmem_limit_bytes=vmem)


def _sigmoid(x):
    return 1.0 / (1.0 + jnp.exp(-x))


def _silu(x):
    return x * _sigmoid(x)


def _log_sigmoid(x):
    return jnp.minimum(x, 0.0) - jnp.log1p(jnp.exp(-jnp.abs(x)))


def _dot(a, b):
    return jnp.dot(a, b, preferred_element_type=F32)


def _dot_nt(a, b):
    return lax.dot_general(a, b, (((1,), (1,)), ((), ())), preferred_element_type=F32)


def _split3(x):
    hi = x.astype(BF16)
    r1 = x - hi.astype(F32)
    mid = r1.astype(BF16)
    lo = (r1 - mid.astype(F32)).astype(BF16)
    return hi, mid, lo


def _rmsnorm_mod(x, g, sc, sh):
    ms = jnp.mean(x * x, axis=-1, keepdims=True)
    y = x * lax.rsqrt(ms + EPS) * g
    return y * (1.0 + sc) + sh


def _adaln_kernel(c_ref, w_ref, b_ref, o_ref):
    c = c_ref[...]
    ca = _silu(c).astype(BF16)
    o_ref[0] = _dot(ca, w_ref[0].astype(BF16)) + b_ref[0]


def _adaln(c, ada_w, ada_b):
    depth, d, n6 = ada_w.shape
    b = c.shape[0]
    cp = jnp.zeros((SUBLANES, d), F32).at[:b].set(c)
    tn = n6 // 4
    out = pl.pallas_call(
        _adaln_kernel,
        grid=(depth, n6 // tn),
        in_specs=[
            pl.BlockSpec((SUBLANES, d), lambda l, j: (0, 0)),
            pl.BlockSpec((1, d, tn), lambda l, j: (l, 0, j)),
            pl.BlockSpec((1, 1, tn), lambda l, j: (l, 0, j)),
        ],
        out_specs=pl.BlockSpec((1, SUBLANES, tn), lambda l, j: (l, 0, j)),
        out_shape=jax.ShapeDtypeStruct((depth, SUBLANES, n6), F32),
        compiler_params=_cparams(("parallel", "parallel")),
        name="adaln",
    )(cp, ada_w, ada_b.reshape(depth, 1, n6))
    return out[:, :b]


def _inproj_kernel(x_ref, g_ref, sc_ref, sh_ref, w_ref, b_ref, cw_ref, cb_ref,
                   q_ref, kv_ref, sm_ref, qm_ref, km_ref, v_ref, o_ref, tail_scr, *, per_b):
    i = pl.program_id(0)

    @pl.when(i == 0)
    def _():
        tail_scr[...] = jnp.zeros_like(tail_scr)

    h = _rmsnorm_mod(x_ref[...], g_ref[...], sc_ref[0], sh_ref[0]).astype(BF16)

    def sec(lo, width):
        return _dot(h, w_ref[:, lo:lo + width]) + b_ref[:, lo:lo + width]

    strip = 2 * LANES
    row8 = lax.broadcasted_iota(I32, (SUBLANES, 1), 0)
    first = i % per_b == 0
    half = qm_ref.shape[1]
    for c0 in range(0, 2 * half, strip):
        cs = slice(c0, c0 + strip)
        cur = sec(1408 + c0, strip)
        tm = cur.shape[0]
        prev = jnp.where(first, 0.0, tail_scr[:, cs])
        tail_scr[:, cs] = cur[tm - SUBLANES:tm, :]
        y = cb_ref[:, cs]
        for tap in range(CONV_WIDTH):
            back = CONV_WIDTH - 1 - tap
            if back:
                rolled = pltpu.roll(cur, back, axis=0)
                top = jnp.where(row8 < back, pltpu.roll(prev, back, axis=0), rolled[0:SUBLANES])
                shifted = jnp.concatenate([top, rolled[SUBLANES:]], axis=0)
            else:
                shifted = cur
            y = y + shifted * cw_ref[tap:tap + 1, cs]
        y = _silu(y)
        if c0 < half:
            qm_ref[:, cs] = y.astype(BF16)
        else:
            km_ref[:, c0 - half:c0 - half + strip] = (y * (MLSTM_HEAD_DIM ** -0.5)).astype(BF16)

    for c0 in range(0, 512, strip):
        cs = slice(c0, c0 + strip)
        q_ref[:, cs] = (sec(c0, strip) * (NSA_HEAD_DIM ** -0.5)).astype(BF16)
        v_ref[:, cs] = sec(2432 + c0, strip).astype(BF16)
        o_ref[:, cs] = sec(2944 + c0, strip)
    for s in range(0, 6, 2):
        pair = sec(512 + s * LANES, strip)
        kv_ref[s] = pair[:, :LANES].astype(BF16)
        kv_ref[s + 1] = pair[:, LANES:].astype(BF16)
    sm_ref[...] = sec(512 + 6 * LANES, LANES)


def _inproj_weights(w_in, b_in):
    d = w_in.shape[0]
    o_q, o_kv, o_gate, o_qk, o_v, o_o, o_i, o_f = [int(v) for v in np.cumsum((0, 512, 768, 24, 1024, 512, 512, 4))]

    def cols(a):
        kv = a[..., o_kv:o_gate].reshape(a.shape[:-1] + (3, 2, NSA_KV_HEADS, NSA_HEAD_DIM))
        kv = jnp.swapaxes(kv, -3, -2)
        kv = kv.reshape(a.shape[:-1] + (768,))
        small = jnp.concatenate(
            [a[..., o_gate:o_qk], a[..., o_i:o_f], a[..., o_f:o_f + 4],
             jnp.zeros(a.shape[:-1] + (LANES - 32,), a.dtype)], axis=-1)
        return jnp.concatenate(
            [a[..., o_q:o_kv], kv, small, a[..., o_qk:o_v], a[..., o_v:o_o], a[..., o_o:o_i]], axis=-1)

    return cols(w_in).astype(BF16), cols(b_in)[None, :].astype(F32)


def _inproj(x2, g, sc, sh, w, b, conv_w, conv_b, seq):
    n, d = x2.shape
    tm = min(ROW_TILE, seq)
    per_b = seq // tm
    row = lambda i: (i, 0)
    bat = lambda i: (i // per_b, 0, 0)
    fix = lambda i: (0, 0)
    outs = (
        jax.ShapeDtypeStruct((n, 512), BF16),
        jax.ShapeDtypeStruct((6, n, LANES), BF16),
        jax.ShapeDtypeStruct((n, 128), F32),
        jax.ShapeDtypeStruct((n, 512), BF16),
        jax.ShapeDtypeStruct((n, 512), BF16),
        jax.ShapeDtypeStruct((n, 512), BF16),
        jax.ShapeDtypeStruct((n, 512), F32),
    )
    return pl.pallas_call(
        functools.partial(_inproj_kernel, per_b=per_b),
        grid=(n // tm,),
        in_specs=[
            pl.BlockSpec((tm, d), row),
            pl.BlockSpec((1, d), fix),
            pl.BlockSpec((1, 1, d), bat),
            pl.BlockSpec((1, 1, d), bat),
            pl.BlockSpec(w.shape, fix),
            pl.BlockSpec(b.shape, fix),
            pl.BlockSpec(conv_w.shape, fix),
            pl.BlockSpec((1, conv_w.shape[1]), fix),
        ],
        out_specs=(
            pl.BlockSpec((tm, 512), row),
            pl.BlockSpec((6, tm, LANES), lambda i: (0, i, 0)),
            pl.BlockSpec((tm, 128), row),
            pl.BlockSpec((tm, 512), row),
            pl.BlockSpec((tm, 512), row),
            pl.BlockSpec((tm, 512), row),
            pl.BlockSpec((tm, 512), row),
        ),
        out_shape=outs,
        scratch_shapes=[pltpu.VMEM((SUBLANES, 1024), F32)],
        compiler_params=_cparams(("arbitrary",)),
        name="inproj",
    )(x2, g, sc, sh, w, b, conv_w, conv_b[None, :])


def _compress_kernel(ch_ref, pe_ref, w1_ref, w2_ref, o_ref):
    ch = ch_ref[0].astype(F32)
    a0 = (ch + pe_ref[0:1, :]).astype(BF16)
    a1 = (ch + pe_ref[1:2, :]).astype(BF16)
    half = ch.shape[1]
    h0 = _dot(a0, w1_ref[0:half, :])
    h1 = _dot(a1, w1_ref[half:2 * half, :])
    n_chunk = ch.shape[0]
    hid = h0 + pltpu.roll(h1, n_chunk - 1, axis=0)
    o_ref[0] = _dot(_silu(hid).astype(BF16), w2_ref[...]).astype(BF16)


def _compress_weights(cmp_pe, cmp_w1, cmp_w2):
    dh, hid = NSA_HEAD_DIM, CMP_HIDDEN
    pe = jnp.concatenate([cmp_pe[0], cmp_pe[1]], axis=-1)
    pe = pe.reshape(2, CMP_STRIDE * LANES)
    w1 = cmp_w1.reshape(2, CMP_LEN, dh, hid)
    z = jnp.zeros((CMP_LEN, dh, hid), cmp_w1.dtype)
    wk = jnp.concatenate([w1[0], z], axis=1)
    wv = jnp.concatenate([z, w1[1]], axis=1)
    w1p = jnp.concatenate([wk, wv], axis=2).reshape(CMP_LEN * LANES, 2 * hid)
    z2 = jnp.zeros((hid, dh), cmp_w2.dtype)
    w2p = jnp.concatenate([jnp.concatenate([cmp_w2[0], z2], axis=1),
                           jnp.concatenate([z2, cmp_w2[1]], axis=1)], axis=0)
    return pe.astype(F32), w1p.astype(BF16), w2p.astype(BF16)


def _compress(kv_cmp, pe, w1p, w2p, batch, seq):
    g = kv_cmp.shape[0]
    n_chunk = seq // CMP_STRIDE
    ch = kv_cmp.reshape(g * batch, n_chunk, CMP_STRIDE * LANES)
    return pl.pallas_call(
        _compress_kernel,
        grid=(g * batch,),
        in_specs=[
            pl.BlockSpec((1, n_chunk, CMP_STRIDE * LANES), lambda i: (i, 0, 0)),
            pl.BlockSpec(pe.shape, lambda i: (0, 0)),
            pl.BlockSpec(w1p.shape, lambda i: (0, 0)),
            pl.BlockSpec(w2p.shape, lambda i: (0, 0)),
        ],
        out_specs=pl.BlockSpec((1, n_chunk, LANES), lambda i: (i, 0, 0)),
        out_shape=jax.ShapeDtypeStruct((g * batch, n_chunk, LANES), BF16),
        compiler_params=_cparams(("parallel",)),
        name="nsa_compress",
    )(ch, pe, w1p, w2p)


def _nsa2_kernel(slope_ref, q_ref, kc_ref, ks_ref, kw_ref, gate_ref, ovt_ref, feat_ref, featc_ref,
                 o_ref, kaug_c, vt_c, kaug_s, vt_s, kaug_w, vt_w, todo_scr, *, seq):
    g = pl.program_id(1)
    qb = pl.program_id(2)
    n_cmp = seq // CMP_STRIDE
    n_sel = seq // SEL_BLOCK
    n_top = min(SEL_TOP, n_sel)
    dh = NSA_HEAD_DIM
    cols = NSA_REP * Q_BLOCK
    blocks_per_chunk = SEL_CHUNK // SEL_BLOCK

    @pl.when(qb == 0)
    def _():
        def build(src_ref, f_ref, kaug, vt, n_rows):
            step = min(SEL_CHUNK, n_rows)
            lane = lax.broadcasted_iota(I32, (step, LANES), 1)
            row = lax.broadcasted_iota(I32, (LANES, step), 0)
            for c0 in range(0, n_rows, step):
                x = src_ref[0, c0:c0 + step, :].astype(F32)
                kaug[c0:c0 + step, :] = jnp.where(lane < dh, x, f_ref[c0:c0 + step, :].astype(F32)).astype(BF16)
                vt[:, c0:c0 + step] = jnp.where(row == 0, 1.0, x.T).astype(BF16)

        build(kc_ref, featc_ref, kaug_c, vt_c, n_cmp)
        build(ks_ref, feat_ref, kaug_s, vt_s, seq)
        build(kw_ref, feat_ref, kaug_w, vt_w, seq)

    lane = lax.broadcasted_iota(I32, (Q_BLOCK, LANES), 1)
    blk_f = (lane - dh).astype(F32)
    t_lane = qb * Q_BLOCK + lax.broadcasted_iota(I32, (1, Q_BLOCK), 1)
    qf = q_ref[...].astype(F32)

    parts = []
    for r in range(NSA_REP):
        pair = qf[:, (r // 2) * LANES:(r // 2 + 1) * LANES]
        if r % 2:
            pair = pltpu.roll(pair, dh, axis=1)
        slope = slope_ref[g * NSA_REP + r]
        parts.append(jnp.where(lane < dh, pair, jnp.where(lane == dh, slope, slope * SEL_BLOCK * blk_f)))
    q_all_f = jnp.concatenate(parts, axis=0)
    q_all = q_all_f.astype(BF16)

    def tile4(x):
        return jnp.concatenate([x] * NSA_REP, axis=1)

    s = _dot_nt(kaug_c[...], q_all)
    end_c = lax.broadcasted_iota(I32, (n_cmp, Q_BLOCK), 0) * CMP_STRIDE + (CMP_LEN - 1)
    s = s + tile4(jnp.where(end_c <= t_lane, 0.0, NEG_INF))
    e = jnp.exp(s - jnp.max(s, axis=0, keepdims=True))
    p_c = e / jnp.sum(e, axis=0, keepdims=True)
    p_c = p_c * tile4((t_lane >= CMP_LEN - 1).astype(F32))
    o_c = _dot(vt_c[...], p_c.astype(BF16))

    p4 = p_c[:, 0:Q_BLOCK]
    for r in range(1, NSA_REP):
        p4 = p4 + p_c[:, r * Q_BLOCK:(r + 1) * Q_BLOCK]
    ovt = ovt_ref[...]
    imp = sum(_dot(ovt, piece) for piece in _split3(p4))
    j_col = lax.broadcasted_iota(I32, (n_sel, 1), 0)
    cur = jnp.right_shift(t_lane, SEL_SHIFT)
    forced = (j_col == 0) | (j_col == cur) | (j_col == cur - 1)
    imp = jnp.where(forced, imp + FORCE_BONUS, imp)
    imp = jnp.where(j_col <= cur, imp, -1.0)
    groups = [imp[v * SUBLANES:(v + 1) * SUBLANES, :] for v in range(n_sel // SUBLANES)]
    j_grp = lax.broadcasted_iota(I32, (SUBLANES, Q_BLOCK), 0)
    ranks = [jnp.zeros((SUBLANES, Q_BLOCK), F32) for _ in groups]
    for k in range(n_sel):
        row_k = groups[k // SUBLANES][k % SUBLANES:k % SUBLANES + 1, :]
        for v, grp in enumerate(groups):
            ge = jnp.where(row_k >= grp, 1.0, 0.0)
            gt = jnp.where(row_k > grp, 1.0, 0.0)
            if v * SUBLANES > k:
                inc = ge
            elif (v + 1) * SUBLANES - 1 < k:
                inc = gt
            else:
                inc = jnp.where(j_grp + v * SUBLANES > k, ge, gt)
            ranks[v] = ranks[v] + inc
    sel_t = jnp.where(jnp.concatenate(ranks, axis=0) < n_top, 1.0, 0.0)
    pad_lo = jnp.zeros((dh, Q_BLOCK), F32)
    pieces = [pad_lo, sel_t]
    if n_sel < dh:
        pieces.append(jnp.zeros((dh - n_sel, Q_BLOCK), F32))
    selmat = jnp.concatenate(pieces, axis=0).T
    drop = jnp.concatenate([jnp.where(lane > dh, selmat, 1.0)] * NSA_REP, axis=0) < 0.5
    q_sel = jnp.where(drop, NEG_INF, q_all_f).astype(BF16)

    def sel_scores(c):
        start = pl.multiple_of(c * SEL_CHUNK, SEL_CHUNK)
        return _dot_nt(kaug_s[pl.ds(start, SEL_CHUNK), :], q_sel)

    def sel_update(c, s, carry):
        m, acc = carry
        start = pl.multiple_of(c * SEL_CHUNK, SEL_CHUNK)
        m_new = jnp.maximum(m, jnp.max(s, axis=0, keepdims=True))
        p = jnp.exp(s - m_new).astype(BF16)
        acc = jnp.exp(m - m_new) * acc + _dot(vt_s[:, pl.ds(start, SEL_CHUNK)], p)
        return m_new, acc

    def sel_pair(i, carry):
        c0, c1 = todo_scr[2 * i], todo_scr[2 * i + 1]
        s0, s1 = sel_scores(c0), sel_scores(c1)
        return sel_update(c1, s1, sel_update(c0, s0, carry))

    def sel_single(k, carry):
        c = todo_scr[k]
        return sel_update(c, sel_scores(c), carry)

    last = (qb * Q_BLOCK) // SEL_CHUNK
    n_todo = jnp.int32(0)
    for c in range(seq // SEL_CHUNK - 1):
        rows_c = sel_t[c * blocks_per_chunk:(c + 1) * blocks_per_chunk, :]
        wanted = (jnp.max(rows_c) > 0.5) & (c < last)
        todo_scr[n_todo] = jnp.int32(c)
        n_todo = n_todo + wanted.astype(I32)
    init = (jnp.full((1, cols), NEG_INF, F32), jnp.zeros((LANES, cols), F32))
    carry = lax.fori_loop(0, n_todo // 2, sel_pair, init)
    carry = lax.fori_loop(2 * (n_todo // 2), n_todo, sel_single, carry)
    pos = last * SEL_CHUNK + lax.broadcasted_iota(I32, (SEL_CHUNK, Q_BLOCK), 0)
    s = sel_scores(last) + tile4(jnp.where(pos <= t_lane, 0.0, NEG_INF))
    _, acc = sel_update(last, s, carry)
    o_s = acc / acc[0:1, :]

    span = Q_BLOCK + WINDOW
    start = pl.multiple_of(jnp.maximum(qb * Q_BLOCK - WINDOW, 0), Q_BLOCK)
    s = _dot_nt(kaug_w[pl.ds(start, span), :], q_all)
    dist = t_lane - (start + lax.broadcasted_iota(I32, (span, Q_BLOCK), 0))
    in_band = pltpu.bitcast(dist, jnp.uint32) < WINDOW
    s = s + tile4(jnp.where(in_band, 0.0, NEG_INF))
    p = jnp.exp(s - jnp.max(s, axis=0, keepdims=True)).astype(BF16)
    acc = _dot(vt_w[:, pl.ds(start, span)], p)
    o_w = acc / acc[0:1, :]

    gates = _sigmoid(gate_ref[0])
    mixed = []
    for r in range(NSA_REP):
        cs = slice(r * Q_BLOCK, (r + 1) * Q_BLOCK)
        mixed.append(gates[3 * r:3 * r + 1, :] * o_c[dh:, cs]
                     + gates[3 * r + 1:3 * r + 2, :] * o_s[dh:, cs]
                     + gates[3 * r + 2:3 * r + 3, :] * o_w[dh:, cs])
    for pr in range(NSA_REP // 2):
        pair_t = jnp.concatenate([mixed[2 * pr], mixed[2 * pr + 1]], axis=0)
        o_ref[:, pr * LANES:(pr + 1) * LANES] = pair_t.T.astype(o_ref.dtype)


def _nsa_constants(seq):
    n_cmp, n_sel = seq // CMP_STRIDE, seq // SEL_BLOCK
    assert n_sel <= NSA_HEAD_DIM, "one feature lane per selection block"
    slopes = 2.0 ** (-8.0 * np.arange(1, NSA_HEADS + 1) / NSA_HEADS)
    far = (slopes[:, None] * SEL_BLOCK * np.arange(n_sel)[None, :]).astype(np.float32)
    assert np.array_equal(far.astype(BF16).astype(np.float32), far), "ALiBi features must be exact in bf16"
    lo_c = np.arange(n_cmp)[:, None] * CMP_STRIDE
    lo_s = np.arange(n_sel)[None, :] * SEL_BLOCK
    ov = np.clip(np.minimum(lo_c + CMP_LEN, lo_s + SEL_BLOCK) - np.maximum(lo_c, lo_s), 0, None) / CMP_LEN
    ov[n_cmp - 1] = 0.0

    def feats(pos):
        f = np.zeros((pos.shape[0], LANES), np.float32)
        f[:, NSA_HEAD_DIM] = pos % SEL_BLOCK
        blk = pos // SEL_BLOCK
        ok = (blk >= 1) & (blk < NSA_HEAD_DIM)
        f[np.nonzero(ok)[0], NSA_HEAD_DIM + blk[ok]] = 1.0
        return f

    feat = feats(np.arange(seq))
    pos_c = np.arange(n_cmp) * CMP_STRIDE + CMP_LEN - 1
    featc = feats(pos_c)
    featc[pos_c >= seq] = 0.0
    return (jnp.asarray(slopes, F32), jnp.asarray(ov.T, BF16), jnp.asarray(feat, BF16), jnp.asarray(featc, BF16))


def _nsa2(q, kvp, kc, gates_t, batch, seq):
    n = q.shape[0]
    n_qb = seq // Q_BLOCK
    n_cmp = seq // CMP_STRIDE
    slopes, ovt, feat, featc = _nsa_constants(seq)
    kv5 = kvp.reshape(3, NSA_KV_HEADS, batch, seq, LANES)
    ks = kv5[1].reshape(NSA_KV_HEADS * batch, seq, LANES)
    kw = kv5[2].reshape(NSA_KV_HEADS * batch, seq, LANES)
    fix = lambda b, g, i, s: (0, 0)
    per_bg = lambda b, g, i, s: (g * batch + b, 0, 0)
    grid_spec = pltpu.PrefetchScalarGridSpec(
        num_scalar_prefetch=1,
        grid=(batch, NSA_KV_HEADS, n_qb),
        in_specs=[
            pl.BlockSpec((Q_BLOCK, 2 * LANES), lambda b, g, i, s: (b * n_qb + i, g)),
            pl.BlockSpec((1, n_cmp, LANES), per_bg),
            pl.BlockSpec((1, seq, LANES), per_bg),
            pl.BlockSpec((1, seq, LANES), per_bg),
            pl.BlockSpec((1, 16, Q_BLOCK), lambda b, g, i, s: (b * NSA_KV_HEADS + g, 0, i)),
            pl.BlockSpec(ovt.shape, fix),
            pl.BlockSpec(feat.shape, fix),
            pl.BlockSpec(featc.shape, fix),
        ],
        out_specs=pl.BlockSpec((Q_BLOCK, 2 * LANES), lambda b, g, i, s: (b * n_qb + i, g)),
        scratch_shapes=[
            pltpu.VMEM((n_cmp, LANES), BF16), pltpu.VMEM((LANES, n_cmp), BF16),
            pltpu.VMEM((seq, LANES), BF16), pltpu.VMEM((LANES, seq), BF16),
            pltpu.VMEM((seq, LANES), BF16), pltpu.VMEM((LANES, seq), BF16),
            pltpu.SMEM((seq // SEL_CHUNK,), I32),
        ],
    )
    return pl.pallas_call(
        functools.partial(_nsa2_kernel, seq=seq),
        grid_spec=grid_spec,
        out_shape=jax.ShapeDtypeStruct((n, 512), BF16),
        compiler_params=_cparams(("parallel", "parallel", "arbitrary")),
        name="nsa_attention",
    )(slopes, q, kc, ks, kw, gates_t, ovt, feat, featc)


def _mlstm_kernel(q_ref, k_ref, v_ref, o_ref, sm_ref, g_ref, y_ref, c_scr, n_scr, m_scr):
    c_idx = pl.program_id(1)
    lc = q_ref.shape[0]
    d = MLSTM_HEAD_DIM

    @pl.when(c_idx == 0)
    def _():
        c_scr[...] = jnp.zeros_like(c_scr)
        n_scr[...] = jnp.zeros_like(n_scr)
        m_scr[...] = jnp.zeros_like(m_scr)

    sm = sm_ref[...]
    lf = _log_sigmoid(sm)
    ri = lax.broadcasted_iota(I32, (lc, lc), 0)
    ci = lax.broadcasted_iota(I32, (lc, lc), 1)
    tri = jnp.where(ri >= ci, 1.0, 0.0).astype(BF16)
    a_col = sum(_dot(tri, piece) for piece in _split3(lf))
    a_row = a_col.T

    for h in range(MLSTM_HEADS):
        hs = slice(h * d, (h + 1) * d)
        a_j = a_row[SM_F + h:SM_F + h + 1, :]
        gap_s = sm[:, SM_I + h:SM_I + h + 1] - a_col[:, SM_F + h:SM_F + h + 1]
        m_prev = m_scr[h:h + 1, 0:1]
        qh, kh, vh = q_ref[:, hs], k_ref[:, hs], v_ref[:, hs]
        v_t = vh.astype(F32).T.astype(BF16)
        c_prev = c_scr[h]
        n_prev = n_scr[h:h + 1, :]

        log_d = jnp.where(ri <= ci, a_j + gap_s, -jnp.inf)
        m_inter = a_j + m_prev
        m_t = jnp.maximum(m_inter, jnp.max(log_d, axis=0, keepdims=True))
        d_mat = jnp.exp(log_d - m_t)
        inter = jnp.exp(m_inter - m_t)
        s_qk = _dot_nt(kh, qh) * d_mat
        num = inter * _dot_nt(c_prev.astype(BF16), qh) + _dot(v_t, s_qk.astype(BF16))
        n_rows = jnp.broadcast_to(n_prev, (SUBLANES, d)).astype(BF16)
        den = inter * _dot_nt(n_rows, qh)[0:1, :] + jnp.sum(s_qk, axis=0, keepdims=True)
        hh = (num / jnp.maximum(jnp.abs(den), jnp.exp(-m_t))).T

        a_last = a_j[:, lc - 1:lc]
        log_w = a_last + gap_s
        m_new = jnp.maximum(a_last + m_prev, jnp.max(log_w, axis=0, keepdims=True))
        wk = jnp.exp(log_w - m_new) * kh.astype(F32)
        decay = jnp.exp(a_last + m_prev - m_new)
        c_scr[h] = decay * c_prev + _dot(v_t, wk.astype(BF16))
        n_scr[h:h + 1, :] = decay * n_prev + jnp.sum(wk, axis=0, keepdims=True)
        m_scr[h:h + 1, :] = jnp.broadcast_to(m_new, (1, LANES))

        hg = _sigmoid(o_ref[:, hs]) * hh
        hn = hg * lax.rsqrt(jnp.mean(hg * hg, axis=-1, keepdims=True) + EPS)
        y_ref[:, hs] = (hn * g_ref[:, hs]).astype(y_ref.dtype)


def _mlstm(qm, km, vm, om, sm, norm_g, batch, seq):
    n, w = qm.shape
    lc = min(MLSTM_CHUNK, seq)
    per_b = seq // lc
    row = lambda b, c: (b * per_b + c, 0)
    return pl.pallas_call(
        _mlstm_kernel,
        grid=(batch, per_b),
        in_specs=[
            pl.BlockSpec((lc, w), row),
            pl.BlockSpec((lc, w), row),
            pl.BlockSpec((lc, w), row),
            pl.BlockSpec((lc, w), row),
            pl.BlockSpec((lc, LANES), row),
            pl.BlockSpec((1, w), lambda b, c: (0, 0)),
        ],
        out_specs=pl.BlockSpec((lc, w), row),
        out_shape=jax.ShapeDtypeStruct((n, w), BF16),
        scratch_shapes=[
            pltpu.VMEM((MLSTM_HEADS, MLSTM_HEAD_DIM, MLSTM_HEAD_DIM), F32),
            pltpu.VMEM((SUBLANES, MLSTM_HEAD_DIM), F32),
            pltpu.VMEM((SUBLANES, LANES), F32),
        ],
        compiler_params=_cparams(("parallel", "arbitrary")),
        name="mlstm",
    )(qm, km, vm, om, sm, norm_g[None, :])


def _outproj_kernel(*refs, with_router):
    if with_router:
        (x_ref, ya_ref, yb_ref, w_ref, g1_ref, gn_ref, sc_ref, sh_ref, rw_ref,
         x1_ref, h_ref, lg_ref) = refs
    else:
        x_ref, ya_ref, yb_ref, w_ref, g1_ref, gn_ref, sc_ref, sh_ref, x1_ref, h_ref = refs
    half = ya_ref.shape[1]
    y = _dot(ya_ref[...], w_ref[0:half, :]) + _dot(yb_ref[...], w_ref[half:2 * half, :])
    x1 = x_ref[...] + g1_ref[0] * y
    x1_ref[...] = x1
    h = _rmsnorm_mod(x1, gn_ref[...], sc_ref[0], sh_ref[0])
    h_ref[...] = h.astype(h_ref.dtype)
    if with_router:
        lg_ref[...] = lax.dot_general(rw_ref[...], h, (((1,), (1,)), ((), ())),
                                      precision=lax.Precision.HIGHEST, preferred_element_type=F32)


def _outproj(x2, ya, yb, w_out, g1, gn, sc, sh, seq, router_wt=None):
    n, d = x2.shape
    tm = min(ROW_TILE, seq)
    per_b = seq // tm
    row = lambda i: (i, 0)
    bat = lambda i: (i // per_b, 0, 0)
    fix = lambda i: (0, 0)
    with_router = router_wt is not None
    in_specs = [
        pl.BlockSpec((tm, d), row),
        pl.BlockSpec((tm, ya.shape[1]), row),
        pl.BlockSpec((tm, yb.shape[1]), row),
        pl.BlockSpec(w_out.shape, fix),
        pl.BlockSpec((1, 1, d), bat),
        pl.BlockSpec((1, d), fix),
        pl.BlockSpec((1, 1, d), bat),
        pl.BlockSpec((1, 1, d), bat),
    ]
    out_specs = [pl.BlockSpec((tm, d), row), pl.BlockSpec((tm, d), row)]
    out_shape = [jax.ShapeDtypeStruct((n, d), F32),
                 jax.ShapeDtypeStruct((n, d), F32 if with_router else BF16)]
    args = [x2, ya, yb, w_out, g1, gn, sc, sh]
    if with_router:
        in_specs.append(pl.BlockSpec(router_wt.shape, fix))
        out_specs.append(pl.BlockSpec((N_EXPERTS, tm), lambda i: (0, i)))
        out_shape.append(jax.ShapeDtypeStruct((N_EXPERTS, n), F32))
        args.append(router_wt)
    return pl.pallas_call(
        functools.partial(_outproj_kernel, with_router=with_router),
        grid=(n // tm,),
        in_specs=in_specs,
        out_specs=tuple(out_specs),
        out_shape=tuple(out_shape),
        compiler_params=_cparams(("parallel",)),
        name="outproj_router" if with_router else "outproj",
    )(*args)


def _ffn_kernel(h_ref, x_ref, wg_ref, wu_ref, wd_ref, g2_ref, o_ref, act_scr, *, tf):
    h = h_ref[...]
    d_ff = wg_ref.shape[1]
    for j in range(d_ff // tf):
        cs = slice(j * tf, (j + 1) * tf)
        act_scr[:, cs] = (_silu(_dot(h, wg_ref[:, cs])) * _dot(h, wu_ref[:, cs])).astype(BF16)
    o_ref[...] = x_ref[...] + g2_ref[0] * _dot(act_scr[...], wd_ref[...])


def _ffn(h, x1, wg, wu, wd, g2, seq):
    n, d = x1.shape
    d_ff = wg.shape[1]
    tm = min(ROW_TILE, seq)
    per_b = seq // tm
    tf = 256
    row = lambda i: (i, 0)
    fix = lambda i: (0, 0)
    once = dict(pipeline_mode=pl.Buffered(1))
    return pl.pallas_call(
        functools.partial(_ffn_kernel, tf=tf),
        grid=(n // tm,),
        in_specs=[
            pl.BlockSpec((tm, d), row),
            pl.BlockSpec((tm, d), row),
            pl.BlockSpec(wg.shape, fix, **once),
            pl.BlockSpec(wu.shape, fix, **once),
            pl.BlockSpec(wd.shape, fix, **once),
            pl.BlockSpec((1, 1, d), lambda i: (i // per_b, 0, 0)),
        ],
        out_specs=pl.BlockSpec((tm, d), row),
        out_shape=jax.ShapeDtypeStruct((n, d), F32),
        scratch_shapes=[pltpu.VMEM((tm, d_ff), BF16)],
        compiler_params=_cparams(("parallel",)),
        name="dense_ffn",
    )(h, x1, wg, wu, wd, g2)


def _route_kernel(lg_ref, dest_ref, wt_ref, meta_ref, cnt_scr, exc_scr, *, tile):
    n_e, n = lg_ref.shape
    lg = lg_ref[...]
    e_iota = lax.broadcasted_iota(I32, (n_e, n), 0)
    m1 = jnp.max(lg, axis=0, keepdims=True)
    e0 = jnp.min(jnp.where(lg == m1, e_iota, n_e), axis=0, keepdims=True)
    lg2 = jnp.where(e_iota == e0, -jnp.inf, lg)
    m2 = jnp.max(lg2, axis=0, keepdims=True)
    e1 = jnp.min(jnp.where(lg2 == m2, e_iota, n_e), axis=0, keepdims=True)
    ex = jnp.exp(m2 - m1)
    wt_ref[0:1, :] = 1.0 / (1.0 + ex)
    wt_ref[1:2, :] = ex / (1.0 + ex)
    oh0 = e_iota == e0
    oh1 = e_iota == e1
    cnt_scr[...] = jnp.where(oh0, 1.0, 0.0) + jnp.where(oh1, 1.0, 0.0)

    ri = lax.broadcasted_iota(I32, (LANES, 2 * LANES), 0)
    ci = lax.broadcasted_iota(I32, (LANES, 2 * LANES), 1)
    prefix_total = jnp.where((ci >= LANES) | (ri < ci), 1.0, 0.0).astype(BF16)

    def block(kb, carry):
        ls = pl.ds(pl.multiple_of(kb * LANES, LANES), LANES)
        both = _dot(cnt_scr[:, ls].astype(BF16), prefix_total)
        exc_scr[:, ls] = both[:, :LANES] + carry
        return carry + both[:, LANES:]

    total = lax.fori_loop(0, n // LANES, block, jnp.zeros((n_e, LANES), F32))
    padded = jnp.floor((total + (tile - 1)) / tile) * tile
    e_col = lax.broadcasted_iota(I32, (n_e, LANES), 0)
    starts = jnp.zeros((n_e, LANES), F32)
    for e in range(n_e - 1):
        starts = starts + jnp.where(e_col > e, padded[e:e + 1, :], 0.0)
    ends = starts + padded
    slot = starts[:, 0:1] + exc_scr[...]
    dest_ref[0:1, :] = jnp.sum(jnp.where(oh0, slot, 0.0), axis=0, keepdims=True).astype(I32)
    dest_ref[1:2, :] = jnp.sum(jnp.where(oh1, slot, 0.0), axis=0, keepdims=True).astype(I32)
    blk_start = (lax.broadcasted_iota(I32, (n_e, LANES), 1) * tile).astype(F32)
    blk_exp = jnp.sum(jnp.where(ends <= blk_start, 1.0, 0.0), axis=0, keepdims=True)
    meta_ref[0:1, :] = jnp.minimum(blk_exp, n_e - 1.0).astype(I32)
    meta_ref[1:2, :] = (ends[n_e - 1:n_e, :] / tile).astype(I32)
    on_diag = lax.broadcasted_iota(I32, (n_e, LANES), 1) == e_col
    meta_ref[2:3, :] = jnp.sum(jnp.where(on_diag, ends / tile, 0.0), axis=0, keepdims=True).astype(I32)
    meta_ref[3:4, :] = jnp.sum(jnp.where(on_diag, starts / tile, 0.0), axis=0, keepdims=True).astype(I32)
    meta_ref[4:SUBLANES, :] = jnp.zeros((SUBLANES - 4, LANES), I32)


def _route(logits_t, tile):
    n_e, n = logits_t.shape
    return pl.pallas_call(
        functools.partial(_route_kernel, tile=tile),
        out_shape=(jax.ShapeDtypeStruct((2, n), I32), jax.ShapeDtypeStruct((2, n), F32),
                   jax.ShapeDtypeStruct((SUBLANES, LANES), I32)),
        scratch_shapes=[pltpu.VMEM((n_e, n), F32), pltpu.VMEM((n_e, n), F32)],
        compiler_params=pltpu.CompilerParams(vmem_limit_bytes=VMEM_LIMIT),
        name="moe_route",
    )(logits_t)


def _dispatch_kernel(dest_ref, meta_ref, h_ref, wg_ref, wu_ref, wd_ref,
                     xs_ref, wgo_ref, wuo_ref, wdo_ref, zero_scr, sem, zsem, *, tile):
    i = pl.program_id(0)
    td = h_ref.shape[0]

    @pl.when(i == 0)
    def _():
        zero_scr[...] = jnp.zeros_like(zero_scr)
        n_blk = xs_ref.shape[0] // tile

        def zero_block(b):
            return pltpu.make_async_copy(zero_scr, xs_ref.at[pl.ds(pl.multiple_of(b * tile, tile), tile)], zsem)

        def each_block(fn):
            for e in range(N_EXPERTS):
                @pl.when(meta_ref[2, e] > meta_ref[3, e])
                def _():
                    fn(zero_block(meta_ref[2, e] - 1))

            def tail(b, _):
                fn(zero_block(b))
                return 0
            lax.fori_loop(meta_ref[1, 0], n_blk, tail, 0)

        each_block(lambda c: c.start())
        each_block(lambda c: c.wait())

    def copy(r, k):
        row = dest_ref[k * (pl.num_programs(0) * td) + i * td + r]
        return pltpu.make_async_copy(h_ref.at[pl.ds(r, 1)], xs_ref.at[pl.ds(row, 1)], sem)

    def start(r, _):
        copy(r, 0).start()
        copy(r, 1).start()
        return 0

    lax.fori_loop(0, td, start, 0, unroll=DMA_UNROLL)
    wgo_ref[...] = wg_ref[...].astype(BF16)
    wuo_ref[...] = wu_ref[...].astype(BF16)
    wdo_ref[...] = wd_ref[...].astype(BF16)
    for _ in range(2):
        pltpu.make_async_copy(h_ref, xs_ref.at[pl.ds(0, td)], sem).wait()


def _dispatch(dest, meta, h, n_rows, tile, weights):
    n, d = h.shape
    td = min(GATHER_TILE, n)
    n_steps = n // td
    flat = [w.reshape(-1, w.shape[-1]) for w in weights]
    slabs = [w.shape[0] // n_steps for w in flat]
    assert all(w.shape[0] == sl * n_steps and sl % (2 * SUBLANES) == 0 for w, sl in zip(flat, slabs))
    w_specs = [pl.BlockSpec((sl, w.shape[1]), lambda i, s, m: (i, 0)) for w, sl in zip(flat, slabs)]
    grid_spec = pltpu.PrefetchScalarGridSpec(
        num_scalar_prefetch=2,
        grid=(n_steps,),
        in_specs=[pl.BlockSpec((td, d), lambda i, s, m: (i, 0))] + w_specs,
        out_specs=[pl.BlockSpec(memory_space=pl.ANY)] + w_specs,
        scratch_shapes=[pltpu.VMEM((tile, d), h.dtype), pltpu.SemaphoreType.DMA(()),
                        pltpu.SemaphoreType.DMA(())],
    )
    outs = pl.pallas_call(
        functools.partial(_dispatch_kernel, tile=tile),
        grid_spec=grid_spec,
        out_shape=[jax.ShapeDtypeStruct((n_rows, d), h.dtype)]
        + [jax.ShapeDtypeStruct(w.shape, BF16) for w in flat],
        compiler_params=_cparams(("arbitrary",)),
        name="moe_dispatch",
    )(dest.reshape(-1), meta, h, *flat)
    return outs[0], [o.reshape(w.shape) for o, w in zip(outs[1:], weights)]


def _expert_kernel(meta_ref, x_ref, wg_ref, wu_ref, wd_ref, y_ref, xb_scr):
    i = pl.program_id(0)
    j = pl.program_id(1)

    @pl.when(i < meta_ref[1, 0])
    def _():
        @pl.when(j == 0)
        def _():
            xb_scr[...] = x_ref[...].astype(BF16)

        xb = xb_scr[...]
        act = (_silu(_dot(xb, wg_ref[0])) * _dot(xb, wu_ref[0])).astype(BF16)
        part = _dot(act, wd_ref[0])

        @pl.when(j == 0)
        def _():
            y_ref[...] = part

        @pl.when(j > 0)
        def _():
            y_ref[...] += part

    @pl.when((i >= meta_ref[1, 0]) & (j == 0))
    def _():
        y_ref[...] = jnp.zeros_like(y_ref)


def _experts(meta, xs, wg, wu, wd, tile):
    n_rows, d = xs.shape
    d_ff = wg.shape[2]
    tf = d_ff // 2 if (d_ff // 2) % (2 * LANES) == 0 else d_ff
    n_blk = n_rows // tile

    def blk(i, s):
        return jnp.minimum(i, s[1, 0] - 1)

    grid_spec = pltpu.PrefetchScalarGridSpec(
        num_scalar_prefetch=1,
        grid=(n_blk, d_ff // tf),
        in_specs=[
            pl.BlockSpec((tile, d), lambda i, j, s: (blk(i, s), 0)),
            pl.BlockSpec((1, d, tf), lambda i, j, s: (s[0, blk(i, s)], 0, jnp.where(i < s[1, 0], j, 0))),
            pl.BlockSpec((1, d, tf), lambda i, j, s: (s[0, blk(i, s)], 0, jnp.where(i < s[1, 0], j, 0))),
            pl.BlockSpec((1, tf, d), lambda i, j, s: (s[0, blk(i, s)], jnp.where(i < s[1, 0], j, 0), 0)),
        ],
        out_specs=pl.BlockSpec((tile, d), lambda i, j, s: (i, 0)),
        scratch_shapes=[pltpu.VMEM((tile, d), BF16)],
    )
    return pl.pallas_call(
        _expert_kernel,
        grid_spec=grid_spec,
        out_shape=jax.ShapeDtypeStruct((n_rows, d), F32),
        compiler_params=_cparams(("arbitrary", "arbitrary")),
        name="moe_experts",
    )(meta, xs, wg, wu, wd)


def _combine_kernel(dest_ref, y_ref, x_ref, wt_ref, g2_ref, gn_ref, o_ref, buf, sem):
    i = pl.program_id(0)
    n_steps = pl.num_programs(0)
    tc = x_ref.shape[0]
    slot = i % 2

    def gather(step, to_slot):
        def start(r, _):
            for k in range(2):
                row = dest_ref[k * (n_steps * tc) + step * tc + r]
                pltpu.make_async_copy(y_ref.at[pl.ds(row, 1)], buf.at[to_slot, k, pl.ds(r, 1)],
                                      sem.at[to_slot]).start()
            return 0
        lax.fori_loop(0, tc, start, 0, unroll=DMA_UNROLL)

    @pl.when(i == 0)
    def _():
        gather(0, 0)

    @pl.when(i + 1 < n_steps)
    def _():
        gather(i + 1, 1 - slot)

    for k in range(2):
        pltpu.make_async_copy(y_ref.at[pl.ds(0, tc)], buf.at[slot, k], sem.at[slot]).wait()
    wt = wt_ref[...]
    y = buf[slot, 0] * wt[:, 0:1] + buf[slot, 1] * wt[:, 1:2]
    x = x_ref[...] + g2_ref[0] * y
    ms = jnp.mean(x * x, axis=-1, keepdims=True)
    o_ref[...] = x * lax.rsqrt(ms + EPS) * gn_ref[...]


def _combine(dest, y, x1, wt, g2, gn, seq):
    n, d = x1.shape
    tc = min(GATHER_TILE, seq)
    per_b = seq // tc
    grid_spec = pltpu.PrefetchScalarGridSpec(
        num_scalar_prefetch=1,
        grid=(n // tc,),
        in_specs=[
            pl.BlockSpec(memory_space=pl.ANY),
            pl.BlockSpec((tc, d), lambda i, s: (i, 0)),
            pl.BlockSpec((tc, 2), lambda i, s: (i, 0)),
            pl.BlockSpec((1, 1, d), lambda i, s: (i // per_b, 0, 0)),
            pl.BlockSpec((1, d), lambda i, s: (0, 0)),
        ],
        out_specs=pl.BlockSpec((tc, d), lambda i, s: (i, 0)),
        scratch_shapes=[pltpu.VMEM((2, 2, tc, d), F32), pltpu.SemaphoreType.DMA((2,))],
    )
    return pl.pallas_call(
        _combine_kernel,
        grid_spec=grid_spec,
        out_shape=jax.ShapeDtypeStruct((n, d), F32),
        compiler_params=_cparams(("arbitrary",)),
        name="moe_combine_norm",
    )(dest.reshape(-1), y, x1, wt, g2, gn)


def _final_norm_kernel(x_ref, g_ref, o_ref):
    x = x_ref[...]
    ms = jnp.mean(x * x, axis=-1, keepdims=True)
    o_ref[...] = x * lax.rsqrt(ms + EPS) * g_ref[...]


def _mixer(x2, mod, l, batch, seq, norm_mix_g, w_in, b_in, cmp_pe, cmp_w1, cmp_w2,
           conv_w, conv_b, mlstm_norm_g):
    d = x2.shape[1]
    sh1, sc1 = mod[l, :, 0:d], mod[l, :, d:2 * d]
    w, b = _inproj_weights(w_in[l], b_in[l])
    q, kvp, sm, qm, km, vm, om = _inproj(x2, norm_mix_g[l][None, :], sc1[:, None, :], sh1[:, None, :], w, b,
                                         conv_w[l], conv_b[l], seq)
    pe, w1p, w2p = _compress_weights(cmp_pe[l], cmp_w1[l], cmp_w2[l])
    kc = _compress(kvp[0:NSA_KV_HEADS], pe, w1p, w2p, batch, seq)
    gates = sm[:, SM_GATE:SM_GATE + 24].reshape(batch, seq, NSA_KV_HEADS, NSA_REP * 3)
    gates = jnp.pad(gates.transpose(0, 2, 3, 1), ((0, 0), (0, 0), (0, 16 - NSA_REP * 3), (0, 0)))
    y_nsa = _nsa2(q, kvp, kc, gates.reshape(batch * NSA_KV_HEADS, 16, seq), batch, seq)
    y_ml = _mlstm(qm, km, vm, om, sm, mlstm_norm_g[l], batch, seq)
    return y_nsa, y_ml


def kernel(x, c, ada_w, ada_b, norm_mix_g, norm_ffn_g, w_in, b_in, cmp_pe, cmp_w1, cmp_w2, conv_w, conv_b, mlstm_norm_g, w_out, ffn_w_gate, ffn_w_up, ffn_w_down, router_w, moe_w_gate, moe_w_up, moe_w_down, final_norm_g):
    batch, seq, d = x.shape
    depth = ada_w.shape[0]
    n = batch * seq
    mod = _adaln(c, ada_w, ada_b)
    x2 = x.reshape(n, d)
    for l in range(depth):
        g1 = mod[l, :, 2 * d:3 * d][:, None, :]
        sh2 = mod[l, :, 3 * d:4 * d][:, None, :]
        sc2 = mod[l, :, 4 * d:5 * d][:, None, :]
        g2 = mod[l, :, 5 * d:6 * d][:, None, :]
        y_nsa, y_ml = _mixer(x2, mod, l, batch, seq, norm_mix_g, w_in, b_in, cmp_pe, cmp_w1, cmp_w2,
                             conv_w, conv_b, mlstm_norm_g)
        gn = norm_ffn_g[l][None, :]
        i = l // 2
        last = l == depth - 1
        if l % 2 == 0:
            x1, h = _outproj(x2, y_nsa, y_ml, w_out[l].astype(BF16), g1, gn, sc2, sh2, seq)
            x2 = _ffn(h, x1, ffn_w_gate[i].astype(BF16), ffn_w_up[i].astype(BF16),
                      ffn_w_down[i].astype(BF16), g2, seq)
            if last:
                x2 = _final_norm(x2, final_norm_g)
        else:
            x1, h, logits_t = _outproj(x2, y_nsa, y_ml, w_out[l].astype(BF16), g1, gn, sc2, sh2, seq,
                                       router_wt=router_w[i].T)
            n_rows = 2 * n + N_EXPERTS * MOE_TILE
            dest, wt, meta = _route(logits_t, MOE_TILE)
            xs, (wg, wu, wd) = _dispatch(dest, meta, h, n_rows, MOE_TILE,
                                         (moe_w_gate[i], moe_w_up[i], moe_w_down[i]))
            y = _experts(meta, xs, wg, wu, wd, MOE_TILE)
            unit = jnp.ones((1, d), F32)
            x2 = _combine(dest, y, x1, wt.T, g2, final_norm_g[None, :] if last else unit, seq)
            if not last:
                raise NotImplementedError("a MoE layer that is not the last layer")
    return x2.reshape(batch, seq, d)


def _final_norm(x2, g):
    n, d = x2.shape
    tm = min(ROW_TILE, n)
    return pl.pallas_call(
        _final_norm_kernel,
        grid=(n // tm,),
        in_specs=[pl.BlockSpec((tm, d), lambda i: (i, 0)), pl.BlockSpec((1, d), lambda i: (0, 0))],
        out_specs=pl.BlockSpec((tm, d), lambda i: (i, 0)),
        out_shape=jax.ShapeDtypeStruct((n, d), F32),
        compiler_params=_cparams(("parallel",)),
        name="final_norm",
    )(x2, g[None, :])
```

```python
import functools

import numpy as np
import jax
import jax.numpy as jnp
from jax import lax
from jax.experimental import pallas as pl
from jax.experimental.pallas import tpu as pltpu

F32 = jnp.float32
BF16 = jnp.bfloat16
I32 = jnp.int32

NSA_HEADS = 8
NSA_KV_HEADS = 2
NSA_REP = NSA_HEADS // NSA_KV_HEADS
NSA_HEAD_DIM = 64
CMP_LEN = 32
CMP_STRIDE = 16
CMP_HIDDEN = 128
SEL_BLOCK = 64
SEL_SHIFT = 6
SEL_TOP = 16
WINDOW = 512
Q_BLOCK = 256
FORCE_BONUS = 1e4
NEG_INF = -1e30
MLSTM_HEADS = 4
MLSTM_HEAD_DIM = 128
CONV_WIDTH = 4
N_EXPERTS = 8
EPS = 1e-6

LANES = 128
SUBLANES = 8
VMEM_LIMIT = 56 * 1024 * 1024

ROW_TILE = 512
SEL_CHUNK = 512
MLSTM_CHUNK = 256
MOE_TILE = 512
GATHER_TILE = 256
DMA_UNROLL = 8

SM_GATE = 0
SM_I = 24
SM_F = 28


def _cparams(sem, vmem=VMEM_LIMIT):
    return pltpu.CompilerParams(dimension_semantics=sem, vmem_limit_bytes=vmem)


def _sigmoid(x):
    return 1.0 / (1.0 + jnp.exp(-x))


def _silu(x):
    return x * _sigmoid(x)


def _log_sigmoid(x):
    return jnp.minimum(x, 0.0) - jnp.log1p(jnp.exp(-jnp.abs(x)))


def _dot(a, b):
    return jnp.dot(a, b, preferred_element_type=F32)


def _dot_nt(a, b):
    return lax.dot_general(a, b, (((1,), (1,)), ((), ())), preferred_element_type=F32)


def _split3(x):
    hi = x.astype(BF16)
    r1 = x - hi.astype(F32)
    mid = r1.astype(BF16)
    lo = (r1 - mid.astype(F32)).astype(BF16)
    return hi, mid, lo


def _rmsnorm_mod(x, g, sc, sh):
    ms = jnp.mean(x * x, axis=-1, keepdims=True)
    y = x * lax.rsqrt(ms + EPS) * g
    return y * (1.0 + sc) + sh


def _adaln_kernel(c_ref, w_ref, b_ref, o_ref):
    c = c_ref[...]
    ca = _silu(c).astype(BF16)
    o_ref[0] = _dot(ca, w_ref[0].astype(BF16)) + b_ref[0]


def _adaln(c, ada_w, ada_b):
    depth, d, n6 = ada_w.shape
    b = c.shape[0]
    cp = jnp.zeros((SUBLANES, d), F32).at[:b].set(c)
    tn = n6 // 4
    out = pl.pallas_call(
        _adaln_kernel,
        grid=(depth, n6 // tn),
        in_specs=[
            pl.BlockSpec((SUBLANES, d), lambda l, j: (0, 0)),
            pl.BlockSpec((1, d, tn), lambda l, j: (l, 0, j)),
            pl.BlockSpec((1, 1, tn), lambda l, j: (l, 0, j)),
        ],
        out_specs=pl.BlockSpec((1, SUBLANES, tn), lambda l, j: (l, 0, j)),
        out_shape=jax.ShapeDtypeStruct((depth, SUBLANES, n6), F32),
        compiler_params=_cparams(("parallel", "parallel")),
        name="adaln",
    )(cp, ada_w, ada_b.reshape(depth, 1, n6))
    return out[:, :b]


def _inproj_kernel(x_ref, g_ref, sc_ref, sh_ref, w_ref, b_ref, cw_ref, cb_ref, gm_ref,
                   q_ref, kv_ref, sm_ref, y_ref,
                   qm_ref, km_ref, v_ref, o_ref, tail_scr, c_scr, n_scr, m_scr, *, per_b, lc):
    i = pl.program_id(0)

    @pl.when(i == 0)
    def _():
        tail_scr[...] = jnp.zeros_like(tail_scr)

    @pl.when(i % per_b == 0)
    def _():
        c_scr[...] = jnp.zeros_like(c_scr)
        n_scr[...] = jnp.zeros_like(n_scr)
        m_scr[...] = jnp.zeros_like(m_scr)

    h = _rmsnorm_mod(x_ref[...], g_ref[...], sc_ref[0], sh_ref[0]).astype(BF16)

    def sec(lo, width):
        return _dot(h, w_ref[:, lo:lo + width]) + b_ref[:, lo:lo + width]

    strip = 2 * LANES
    row8 = lax.broadcasted_iota(I32, (SUBLANES, 1), 0)
    first = i % per_b == 0
    half = qm_ref.shape[1]
    for c0 in range(0, 2 * half, strip):
        cs = slice(c0, c0 + strip)
        cur = sec(1408 + c0, strip)
        tm = cur.shape[0]
        prev = jnp.where(first, 0.0, tail_scr[:, cs])
        tail_scr[:, cs] = cur[tm - SUBLANES:tm, :]
        y = cb_ref[:, cs]
        for tap in range(CONV_WIDTH):
            back = CONV_WIDTH - 1 - tap
            if back:
                rolled = pltpu.roll(cur, back, axis=0)
                top = jnp.where(row8 < back, pltpu.roll(prev, back, axis=0), rolled[0:SUBLANES])
                shifted = jnp.concatenate([top, rolled[SUBLANES:]], axis=0)
            else:
                shifted = cur
            y = y + shifted * cw_ref[tap:tap + 1, cs]
        y = _silu(y)
        if c0 < half:
            qm_ref[:, cs] = y.astype(BF16)
        else:
            km_ref[:, c0 - half:c0 - half + strip] = (y * (MLSTM_HEAD_DIM ** -0.5)).astype(BF16)

    for c0 in range(0, 512, strip):
        cs = slice(c0, c0 + strip)
        v_ref[:, cs] = sec(2432 + c0, strip).astype(BF16)
        o_ref[:, cs] = sec(2944 + c0, strip)
    sm_ref[...] = sec(512 + 6 * LANES, LANES)

    for c in range(tm // lc):
        _mlstm_chunk(slice(c * lc, (c + 1) * lc), qm_ref, km_ref, v_ref, o_ref, sm_ref, gm_ref, y_ref,
                     c_scr, n_scr, m_scr)

    for c0 in range(0, 512, strip):
        cs = slice(c0, c0 + strip)
        q_ref[:, cs] = (sec(c0, strip) * (NSA_HEAD_DIM ** -0.5)).astype(BF16)
    for s in range(0, 6, 2):
        pair = sec(512 + s * LANES, strip)
        kv_ref[s] = pair[:, :LANES].astype(BF16)
        kv_ref[s + 1] = pair[:, LANES:].astype(BF16)


def _inproj_weights(w_in, b_in):
    d = w_in.shape[0]
    o_q, o_kv, o_gate, o_qk, o_v, o_o, o_i, o_f = [int(v) for v in np.cumsum((0, 512, 768, 24, 1024, 512, 512, 4))]

    def cols(a):
        kv = a[..., o_kv:o_gate].reshape(a.shape[:-1] + (3, 2, NSA_KV_HEADS, NSA_HEAD_DIM))
        kv = jnp.swapaxes(kv, -3, -2)
        kv = kv.reshape(a.shape[:-1] + (768,))
        small = jnp.concatenate(
            [a[..., o_gate:o_qk], a[..., o_i:o_f], a[..., o_f:o_f + 4],
             jnp.zeros(a.shape[:-1] + (LANES - 32,), a.dtype)], axis=-1)
        return jnp.concatenate(
            [a[..., o_q:o_kv], kv, small, a[..., o_qk:o_v], a[..., o_v:o_o], a[..., o_o:o_i]], axis=-1)

    return cols(w_in).astype(BF16), cols(b_in)[None, :].astype(F32)


def _inproj(x2, g, sc, sh, w, b, conv_w, conv_b, norm_g, seq):
    n, d = x2.shape
    tm = min(ROW_TILE, seq)
    lc = min(MLSTM_CHUNK, tm)
    per_b = seq // tm
    wm = MLSTM_HEADS * MLSTM_HEAD_DIM
    row = lambda i: (i, 0)
    bat = lambda i: (i // per_b, 0, 0)
    fix = lambda i: (0, 0)
    outs = (
        jax.ShapeDtypeStruct((n, 512), BF16),
        jax.ShapeDtypeStruct((6, n, LANES), BF16),
        jax.ShapeDtypeStruct((n, 128), F32),
        jax.ShapeDtypeStruct((n, wm), BF16),
    )
    return pl.pallas_call(
        functools.partial(_inproj_kernel, per_b=per_b, lc=lc),
        grid=(n // tm,),
        in_specs=[
            pl.BlockSpec((tm, d), row),
            pl.BlockSpec((1, d), fix),
            pl.BlockSpec((1, 1, d), bat),
            pl.BlockSpec((1, 1, d), bat),
            pl.BlockSpec(w.shape, fix),
            pl.BlockSpec(b.shape, fix),
            pl.BlockSpec(conv_w.shape, fix),
            pl.BlockSpec((1, conv_w.shape[1]), fix),
            pl.BlockSpec((1, wm), fix),
        ],
        out_specs=(
            pl.BlockSpec((tm, 512), row),
            pl.BlockSpec((6, tm, LANES), lambda i: (0, i, 0)),
            pl.BlockSpec((tm, 128), row),
            pl.BlockSpec((tm, wm), row),
        ),
        out_shape=outs,
        scratch_shapes=[
            pltpu.VMEM((tm, wm), BF16), pltpu.VMEM((tm, wm), BF16),
            pltpu.VMEM((tm, wm), BF16), pltpu.VMEM((tm, wm), F32),
            pltpu.VMEM((SUBLANES, 2 * wm), F32),
            pltpu.VMEM((MLSTM_HEADS, MLSTM_HEAD_DIM, MLSTM_HEAD_DIM), F32),
            pltpu.VMEM((SUBLANES, MLSTM_HEAD_DIM), F32),
            pltpu.VMEM((SUBLANES, LANES), F32),
        ],
        compiler_params=_cparams(("arbitrary",)),
        name="inproj_mlstm",
    )(x2, g, sc, sh, w, b, conv_w, conv_b[None, :], norm_g[None, :])


def _compress_kernel(ch_ref, pe_ref, w1_ref, w2_ref, o_ref):
    ch = ch_ref[0].astype(F32)
    a0 = (ch + pe_ref[0:1, :]).astype(BF16)
    a1 = (ch + pe_ref[1:2, :]).astype(BF16)
    half = ch.shape[1]
    h0 = _dot(a0, w1_ref[0:half, :])
    h1 = _dot(a1, w1_ref[half:2 * half, :])
    n_chunk = ch.shape[0]
    hid = h0 + pltpu.roll(h1, n_chunk - 1, axis=0)
    o_ref[0] = _dot(_silu(hid).astype(BF16), w2_ref[...]).astype(BF16)


def _compress_weights(cmp_pe, cmp_w1, cmp_w2):
    dh, hid = NSA_HEAD_DIM, CMP_HIDDEN
    pe = jnp.concatenate([cmp_pe[0], cmp_pe[1]], axis=-1)
    pe = pe.reshape(2, CMP_STRIDE * LANES)
    w1 = cmp_w1.reshape(2, CMP_LEN, dh, hid)
    z = jnp.zeros((CMP_LEN, dh, hid), cmp_w1.dtype)
    wk = jnp.concatenate([w1[0], z], axis=1)
    wv = jnp.concatenate([z, w1[1]], axis=1)
    w1p = jnp.concatenate([wk, wv], axis=2).reshape(CMP_LEN * LANES, 2 * hid)
    z2 = jnp.zeros((hid, dh), cmp_w2.dtype)
    w2p = jnp.concatenate([jnp.concatenate([cmp_w2[0], z2], axis=1),
                           jnp.concatenate([z2, cmp_w2[1]], axis=1)], axis=0)
    return pe.astype(F32), w1p.astype(BF16), w2p.astype(BF16)


def _compress(kv_cmp, pe, w1p, w2p, batch, seq):
    g = kv_cmp.shape[0]
    n_chunk = seq // CMP_STRIDE
    ch = kv_cmp.reshape(g * batch, n_chunk, CMP_STRIDE * LANES)
    return pl.pallas_call(
        _compress_kernel,
        grid=(g * batch,),
        in_specs=[
            pl.BlockSpec((1, n_chunk, CMP_STRIDE * LANES), lambda i: (i, 0, 0)),
            pl.BlockSpec(pe.shape, lambda i: (0, 0)),
            pl.BlockSpec(w1p.shape, lambda i: (0, 0)),
            pl.BlockSpec(w2p.shape, lambda i: (0, 0)),
        ],
        out_specs=pl.BlockSpec((1, n_chunk, LANES), lambda i: (i, 0, 0)),
        out_shape=jax.ShapeDtypeStruct((g * batch, n_chunk, LANES), BF16),
        compiler_params=_cparams(("parallel",)),
        name="nsa_compress",
    )(ch, pe, w1p, w2p)


def _nsa2_kernel(slope_ref, q_ref, kc_ref, ks_ref, kw_ref, gate_ref, ovt_ref, feat_ref, featc_ref,
                 o_ref, kaug_c, vt_c, kaug_s, vt_s, kaug_w, vt_w, todo_scr, *, seq):
    g = pl.program_id(1)
    qb = pl.program_id(2)
    n_cmp = seq // CMP_STRIDE
    n_sel = seq // SEL_BLOCK
    n_top = min(SEL_TOP, n_sel)
    dh = NSA_HEAD_DIM
    cols = NSA_REP * Q_BLOCK
    blocks_per_chunk = SEL_CHUNK // SEL_BLOCK

    @pl.when(qb == 0)
    def _():
        def build(src_ref, f_ref, kaug, vt, n_rows):
            step = min(SEL_CHUNK, n_rows)
            lane = lax.broadcasted_iota(I32, (step, LANES), 1)
            row = lax.broadcasted_iota(I32, (LANES, step), 0)
            for c0 in range(0, n_rows, step):
                x = src_ref[0, c0:c0 + step, :].astype(F32)
                kaug[c0:c0 + step, :] = jnp.where(lane < dh, x, f_ref[c0:c0 + step, :].astype(F32)).astype(BF16)
                vt[:, c0:c0 + step] = jnp.where(row == 0, 1.0, x.T).astype(BF16)

        build(kc_ref, featc_ref, kaug_c, vt_c, n_cmp)
        build(ks_ref, feat_ref, kaug_s, vt_s, seq)
        build(kw_ref, feat_ref, kaug_w, vt_w, seq)

    lane = lax.broadcasted_iota(I32, (Q_BLOCK, LANES), 1)
    blk_f = (lane - dh).astype(F32)
    t_lane = qb * Q_BLOCK + lax.broadcasted_iota(I32, (1, Q_BLOCK), 1)
    qf = q_ref[...].astype(F32)

    parts = []
    for r in range(NSA_REP):
        pair = qf[:, (r // 2) * LANES:(r // 2 + 1) * LANES]
        if r % 2:
            pair = pltpu.roll(pair, dh, axis=1)
        slope = slope_ref[g * NSA_REP + r]
        parts.append(jnp.where(lane < dh, pair, jnp.where(lane == dh, slope, slope * SEL_BLOCK * blk_f)))
    q_all_f = jnp.concatenate(parts, axis=0)
    q_all = q_all_f.astype(BF16)

    def tile4(x):
        return jnp.concatenate([x] * NSA_REP, axis=1)

    s = _dot_nt(kaug_c[...], q_all)
    end_c = lax.broadcasted_iota(I32, (n_cmp, Q_BLOCK), 0) * CMP_STRIDE + (CMP_LEN - 1)
    s = s + tile4(jnp.where(end_c <= t_lane, 0.0, NEG_INF))
    e = jnp.exp(s - jnp.max(s, axis=0, keepdims=True))
    p_c = e / jnp.sum(e, axis=0, keepdims=True)
    p_c = p_c * tile4((t_lane >= CMP_LEN - 1).astype(F32))
    o_c = _dot(vt_c[...], p_c.astype(BF16))

    p4 = p_c[:, 0:Q_BLOCK]
    for r in range(1, NSA_REP):
        p4 = p4 + p_c[:, r * Q_BLOCK:(r + 1) * Q_BLOCK]
    ovt = ovt_ref[...]
    imp = sum(_dot(ovt, piece) for piece in _split3(p4))
    j_col = lax.broadcasted_iota(I32, (n_sel, 1), 0)
    cur = jnp.right_shift(t_lane, SEL_SHIFT)
    forced = (j_col == 0) | (j_col == cur) | (j_col == cur - 1)
    imp = jnp.where(forced, imp + FORCE_BONUS, imp)
    imp = jnp.where(j_col <= cur, imp, -1.0)
    groups = [imp[v * SUBLANES:(v + 1) * SUBLANES, :] for v in range(n_sel // SUBLANES)]
    j_grp = lax.broadcasted_iota(I32, (SUBLANES, Q_BLOCK), 0)
    ranks = [jnp.zeros((SUBLANES, Q_BLOCK), F32) for _ in groups]
    for k in range(n_sel):
        row_k = groups[k // SUBLANES][k % SUBLANES:k % SUBLANES + 1, :]
        for v, grp in enumerate(groups):
            ge = jnp.where(row_k >= grp, 1.0, 0.0)
            gt = jnp.where(row_k > grp, 1.0, 0.0)
            if v * SUBLANES > k:
                inc = ge
            elif (v + 1) * SUBLANES - 1 < k:
                inc = gt
            else:
                inc = jnp.where(j_grp + v * SUBLANES > k, ge, gt)
            ranks[v] = ranks[v] + inc
    sel_t = jnp.where(jnp.concatenate(ranks, axis=0) < n_top, 1.0, 0.0)
    pad_lo = jnp.zeros((dh, Q_BLOCK), F32)
    pieces = [pad_lo, sel_t]
    if n_sel < dh:
        pieces.append(jnp.zeros((dh - n_sel, Q_BLOCK), F32))
    selmat = jnp.concatenate(pieces, axis=0).T
    drop = jnp.concatenate([jnp.where(lane > dh, selmat, 1.0)] * NSA_REP, axis=0) < 0.5
    q_sel = jnp.where(drop, NEG_INF, q_all_f).astype(BF16)

    def sel_scores(c):
        start = pl.multiple_of(c * SEL_CHUNK, SEL_CHUNK)
        return _dot_nt(kaug_s[pl.ds(start, SEL_CHUNK), :], q_sel)

    def sel_update(c, s, carry):
        m, acc = carry
        start = pl.multiple_of(c * SEL_CHUNK, SEL_CHUNK)
        m_new = jnp.maximum(m, jnp.max(s, axis=0, keepdims=True))
        p = jnp.exp(s - m_new).astype(BF16)
        acc = jnp.exp(m - m_new) * acc + _dot(vt_s[:, pl.ds(start, SEL_CHUNK)], p)
        return m_new, acc

    def sel_pair(i, carry):
        c0, c1 = todo_scr[2 * i], todo_scr[2 * i + 1]
        s0, s1 = sel_scores(c0), sel_scores(c1)
        return sel_update(c1, s1, sel_update(c0, s0, carry))

    def sel_single(k, carry):
        c = todo_scr[k]
        return sel_update(c, sel_scores(c), carry)

    last = (qb * Q_BLOCK) // SEL_CHUNK
    n_todo = jnp.int32(0)
    for c in range(seq // SEL_CHUNK - 1):
        rows_c = sel_t[c * blocks_per_chunk:(c + 1) * blocks_per_chunk, :]
        wanted = (jnp.max(rows_c) > 0.5) & (c < last)
        todo_scr[n_todo] = jnp.int32(c)
        n_todo = n_todo + wanted.astype(I32)
    init = (jnp.full((1, cols), NEG_INF, F32), jnp.zeros((LANES, cols), F32))
    carry = lax.fori_loop(0, n_todo // 2, sel_pair, init)
    carry = lax.fori_loop(2 * (n_todo // 2), n_todo, sel_single, carry)
    pos = last * SEL_CHUNK + lax.broadcasted_iota(I32, (SEL_CHUNK, Q_BLOCK), 0)
    s = sel_scores(last) + tile4(jnp.where(pos <= t_lane, 0.0, NEG_INF))
    _, acc = sel_update(last, s, carry)
    o_s = acc / acc[0:1, :]

    span = Q_BLOCK + WINDOW
    start = pl.multiple_of(jnp.maximum(qb * Q_BLOCK - WINDOW, 0), Q_BLOCK)
    s = _dot_nt(kaug_w[pl.ds(start, span), :], q_all)
    dist = t_lane - (start + lax.broadcasted_iota(I32, (span, Q_BLOCK), 0))
    in_band = pltpu.bitcast(dist, jnp.uint32) < WINDOW
    s = s + tile4(jnp.where(in_band, 0.0, NEG_INF))
    p = jnp.exp(s - jnp.max(s, axis=0, keepdims=True)).astype(BF16)
    acc = _dot(vt_w[:, pl.ds(start, span)], p)
    o_w = acc / acc[0:1, :]

    gates = _sigmoid(gate_ref[0])
    mixed = []
    for r in range(NSA_REP):
        cs = slice(r * Q_BLOCK, (r + 1) * Q_BLOCK)
        mixed.append(gates[3 * r:3 * r + 1, :] * o_c[dh:, cs]
                     + gates[3 * r + 1:3 * r + 2, :] * o_s[dh:, cs]
                     + gates[3 * r + 2:3 * r + 3, :] * o_w[dh:, cs])
    for pr in range(NSA_REP // 2):
        pair_t = jnp.concatenate([mixed[2 * pr], mixed[2 * pr + 1]], axis=0)
        o_ref[:, pr * LANES:(pr + 1) * LANES] = pair_t.T.astype(o_ref.dtype)


def _nsa_constants(seq):
    n_cmp, n_sel = seq // CMP_STRIDE, seq // SEL_BLOCK
    assert n_sel <= NSA_HEAD_DIM, "one feature lane per selection block"
    slopes = 2.0 ** (-8.0 * np.arange(1, NSA_HEADS + 1) / NSA_HEADS)
    far = (slopes[:, None] * SEL_BLOCK * np.arange(n_sel)[None, :]).astype(np.float32)
    assert np.array_equal(far.astype(BF16).astype(np.float32), far), "ALiBi features must be exact in bf16"
    lo_c = np.arange(n_cmp)[:, None] * CMP_STRIDE
    lo_s = np.arange(n_sel)[None, :] * SEL_BLOCK
    ov = np.clip(np.minimum(lo_c + CMP_LEN, lo_s + SEL_BLOCK) - np.maximum(lo_c, lo_s), 0, None) / CMP_LEN
    ov[n_cmp - 1] = 0.0

    def feats(pos):
        f = np.zeros((pos.shape[0], LANES), np.float32)
        f[:, NSA_HEAD_DIM] = pos % SEL_BLOCK
        blk = pos // SEL_BLOCK
        ok = (blk >= 1) & (blk < NSA_HEAD_DIM)
        f[np.nonzero(ok)[0], NSA_HEAD_DIM + blk[ok]] = 1.0
        return f

    feat = feats(np.arange(seq))
    pos_c = np.arange(n_cmp) * CMP_STRIDE + CMP_LEN - 1
    featc = feats(pos_c)
    featc[pos_c >= seq] = 0.0
    return (jnp.asarray(slopes, F32), jnp.asarray(ov.T, BF16), jnp.asarray(feat, BF16), jnp.asarray(featc, BF16))


def _nsa2(q, kvp, kc, gates_t, batch, seq):
    n = q.shape[0]
    n_qb = seq // Q_BLOCK
    n_cmp = seq // CMP_STRIDE
    slopes, ovt, feat, featc = _nsa_constants(seq)
    fix = lambda b, g, i, s: (0, 0)
    per_bg = lambda b, g, i, s: (g * batch + b, 0, 0)
    branch = lambda br: (lambda b, g, i, s: (br * NSA_KV_HEADS + g, b, 0))
    grid_spec = pltpu.PrefetchScalarGridSpec(
        num_scalar_prefetch=1,
        grid=(batch, NSA_KV_HEADS, n_qb),
        in_specs=[
            pl.BlockSpec((Q_BLOCK, 2 * LANES), lambda b, g, i, s: (b * n_qb + i, g)),
            pl.BlockSpec((1, n_cmp, LANES), per_bg),
            pl.BlockSpec((1, seq, LANES), branch(1)),
            pl.BlockSpec((1, seq, LANES), branch(2)),
            pl.BlockSpec((1, 16, Q_BLOCK), lambda b, g, i, s: (b * NSA_KV_HEADS + g, 0, i)),
            pl.BlockSpec(ovt.shape, fix),
            pl.BlockSpec(feat.shape, fix),
            pl.BlockSpec(featc.shape, fix),
        ],
        out_specs=pl.BlockSpec((Q_BLOCK, 2 * LANES), lambda b, g, i, s: (b * n_qb + i, g)),
        scratch_shapes=[
            pltpu.VMEM((n_cmp, LANES), BF16), pltpu.VMEM((LANES, n_cmp), BF16),
            pltpu.VMEM((seq, LANES), BF16), pltpu.VMEM((LANES, seq), BF16),
            pltpu.VMEM((seq, LANES), BF16), pltpu.VMEM((LANES, seq), BF16),
            pltpu.SMEM((seq // SEL_CHUNK,), I32),
        ],
    )
    return pl.pallas_call(
        functools.partial(_nsa2_kernel, seq=seq),
        grid_spec=grid_spec,
        out_shape=jax.ShapeDtypeStruct((n, 512), BF16),
        compiler_params=_cparams(("parallel", "parallel", "arbitrary")),
        name="nsa_attention",
    )(slopes, q, kc, kvp, kvp, gates_t, ovt, feat, featc)


def _mlstm_chunk(rows, q_ref, k_ref, v_ref, o_ref, sm_ref, g_ref, y_ref, c_scr, n_scr, m_scr):
    lc = rows.stop - rows.start
    d = MLSTM_HEAD_DIM
    sm = sm_ref[rows, :]
    lf = _log_sigmoid(sm)
    ri = lax.broadcasted_iota(I32, (lc, lc), 0)
    ci = lax.broadcasted_iota(I32, (lc, lc), 1)
    tri = jnp.where(ri >= ci, 1.0, 0.0).astype(BF16)
    a_col = sum(_dot(tri, piece) for piece in _split3(lf))
    a_row = a_col.T

    for h in range(MLSTM_HEADS):
        hs = slice(h * d, (h + 1) * d)
        a_j = a_row[SM_F + h:SM_F + h + 1, :]
        gap_s = sm[:, SM_I + h:SM_I + h + 1] - a_col[:, SM_F + h:SM_F + h + 1]
        m_prev = m_scr[h:h + 1, 0:1]
        qh, kh, vh = q_ref[rows, hs], k_ref[rows, hs], v_ref[rows, hs]
        v_t = vh.astype(F32).T.astype(BF16)
        c_prev = c_scr[h]
        n_prev = n_scr[h:h + 1, :]

        log_d = jnp.where(ri <= ci, a_j + gap_s, -jnp.inf)
        m_inter = a_j + m_prev
        m_t = jnp.maximum(m_inter, jnp.max(log_d, axis=0, keepdims=True))
        d_mat = jnp.exp(log_d - m_t)
        inter = jnp.exp(m_inter - m_t)
        s_qk = _dot_nt(kh, qh) * d_mat
        num = inter * _dot_nt(c_prev.astype(BF16), qh) + _dot(v_t, s_qk.astype(BF16))
        n_rows = jnp.broadcast_to(n_prev, (SUBLANES, d)).astype(BF16)
        den = inter * _dot_nt(n_rows, qh)[0:1, :] + jnp.sum(s_qk, axis=0, keepdims=True)
        hh = (num / jnp.maximum(jnp.abs(den), jnp.exp(-m_t))).T

        a_last = a_j[:, lc - 1:lc]
        log_w = a_last + gap_s
        m_new = jnp.maximum(a_last + m_prev, jnp.max(log_w, axis=0, keepdims=True))
        wk = jnp.exp(log_w - m_new) * kh.astype(F32)
        decay = jnp.exp(a_last + m_prev - m_new)
        c_scr[h] = decay * c_prev + _dot(v_t, wk.astype(BF16))
        n_scr[h:h + 1, :] = decay * n_prev + jnp.sum(wk, axis=0, keepdims=True)
        m_scr[h:h + 1, :] = jnp.broadcast_to(m_new, (1, LANES))

        hg = _sigmoid(o_ref[rows, hs]) * hh
        hn = hg * lax.rsqrt(jnp.mean(hg * hg, axis=-1, keepdims=True) + EPS)
        y_ref[rows, hs] = (hn * g_ref[:, hs]).astype(y_ref.dtype)


def _mix_out(x_ref, ya_ref, yb_ref, w_ref, g1_ref, gn_ref, sc_ref, sh_ref):
    half = ya_ref.shape[1]
    y = _dot(ya_ref[...], w_ref[0:half, :]) + _dot(yb_ref[...], w_ref[half:2 * half, :])
    x1 = x_ref[...] + g1_ref[0] * y
    return x1, _rmsnorm_mod(x1, gn_ref[...], sc_ref[0], sh_ref[0])


def _mix_specs(x2, ya, yb, w_out, tm, per_b):
    d = x2.shape[1]
    row = lambda i: (i, 0)
    bat = lambda i: (i // per_b, 0, 0)
    fix = lambda i: (0, 0)
    return [
        pl.BlockSpec((tm, d), row),
        pl.BlockSpec((tm, ya.shape[1]), row),
        pl.BlockSpec((tm, yb.shape[1]), row),
        pl.BlockSpec(w_out.shape, fix),
        pl.BlockSpec((1, 1, d), bat),
        pl.BlockSpec((1, d), fix),
        pl.BlockSpec((1, 1, d), bat),
        pl.BlockSpec((1, 1, d), bat),
    ]


def _outproj_router_kernel(x_ref, ya_ref, yb_ref, w_ref, g1_ref, gn_ref, sc_ref, sh_ref, rw_ref,
                           x1_ref, h_ref, lg_ref):
    x1, h = _mix_out(x_ref, ya_ref, yb_ref, w_ref, g1_ref, gn_ref, sc_ref, sh_ref)
    x1_ref[...] = x1
    h_ref[...] = h
    lg_ref[...] = lax.dot_general(rw_ref[...], h, (((1,), (1,)), ((), ())),
                                  precision=lax.Precision.HIGHEST, preferred_element_type=F32)


def _outproj_router(x2, ya, yb, w_out, g1, gn, sc, sh, seq, router_wt):
    n, d = x2.shape
    tm = min(ROW_TILE, seq)
    per_b = seq // tm
    row = lambda i: (i, 0)
    return pl.pallas_call(
        _outproj_router_kernel,
        grid=(n // tm,),
        in_specs=_mix_specs(x2, ya, yb, w_out, tm, per_b) + [pl.BlockSpec(router_wt.shape, lambda i: (0, 0))],
        out_specs=(pl.BlockSpec((tm, d), row), pl.BlockSpec((tm, d), row),
                   pl.BlockSpec((N_EXPERTS, tm), lambda i: (0, i))),
        out_shape=(jax.ShapeDtypeStruct((n, d), F32), jax.ShapeDtypeStruct((n, d), F32),
                   jax.ShapeDtypeStruct((N_EXPERTS, n), F32)),
        compiler_params=_cparams(("parallel",)),
        name="outproj_router",
    )(x2, ya, yb, w_out, g1, gn, sc, sh, router_wt)


def _outproj_ffn_kernel(x_ref, ya_ref, yb_ref, w_ref, g1_ref, gn_ref, sc_ref, sh_ref,
                        wg_ref, wu_ref, wd_ref, g2_ref, o_ref, act_scr, *, tf):
    x1, h = _mix_out(x_ref, ya_ref, yb_ref, w_ref, g1_ref, gn_ref, sc_ref, sh_ref)
    h = h.astype(BF16)
    d_ff = wg_ref.shape[1]
    for j in range(d_ff // tf):
        cs = slice(j * tf, (j + 1) * tf)
        act_scr[:, cs] = (_silu(_dot(h, wg_ref[:, cs])) * _dot(h, wu_ref[:, cs])).astype(BF16)
    o_ref[...] = x1 + g2_ref[0] * _dot(act_scr[...], wd_ref[...])


def _outproj_ffn(x2, ya, yb, w_out, g1, gn, sc, sh, wg, wu, wd, g2, seq):
    n, d = x2.shape
    d_ff = wg.shape[1]
    tm = min(ROW_TILE, seq)
    per_b = seq // tm
    tf = 2 * LANES
    row = lambda i: (i, 0)
    fix = lambda i: (0, 0)
    once = dict(pipeline_mode=pl.Buffered(1))
    return pl.pallas_call(
        functools.partial(_outproj_ffn_kernel, tf=tf),
        grid=(n // tm,),
        in_specs=_mix_specs(x2, ya, yb, w_out, tm, per_b) + [
            pl.BlockSpec(wg.shape, fix, **once),
            pl.BlockSpec(wu.shape, fix, **once),
            pl.BlockSpec(wd.shape, fix, **once),
            pl.BlockSpec((1, 1, d), lambda i: (i // per_b, 0, 0)),
        ],
        out_specs=pl.BlockSpec((tm, d), row),
        out_shape=jax.ShapeDtypeStruct((n, d), F32),
        scratch_shapes=[pltpu.VMEM((tm, d_ff), BF16)],
        compiler_params=_cparams(("parallel",)),
        name="outproj_ffn",
    )(x2, ya, yb, w_out, g1, gn, sc, sh, wg, wu, wd, g2)


def _route_kernel(lg_ref, dest_ref, wt_ref, meta_ref, cnt_scr, exc_scr, *, tile):
    n_e, n = lg_ref.shape
    lg = lg_ref[...]
    e_iota = lax.broadcasted_iota(I32, (n_e, n), 0)
    m1 = jnp.max(lg, axis=0, keepdims=True)
    e0 = jnp.min(jnp.where(lg == m1, e_iota, n_e), axis=0, keepdims=True)
    lg2 = jnp.where(e_iota == e0, -jnp.inf, lg)
    m2 = jnp.max(lg2, axis=0, keepdims=True)
    e1 = jnp.min(jnp.where(lg2 == m2, e_iota, n_e), axis=0, keepdims=True)
    ex = jnp.exp(m2 - m1)
    wt_ref[0:1, :] = 1.0 / (1.0 + ex)
    wt_ref[1:2, :] = ex / (1.0 + ex)
    oh0 = e_iota == e0
    oh1 = e_iota == e1
    cnt_scr[...] = jnp.where(oh0, 1.0, 0.0) + jnp.where(oh1, 1.0, 0.0)

    ri = lax.broadcasted_iota(I32, (LANES, 2 * LANES), 0)
    ci = lax.broadcasted_iota(I32, (LANES, 2 * LANES), 1)
    prefix_total = jnp.where((ci >= LANES) | (ri < ci), 1.0, 0.0).astype(BF16)

    def block(kb, carry):
        ls = pl.ds(pl.multiple_of(kb * LANES, LANES), LANES)
        both = _dot(cnt_scr[:, ls].astype(BF16), prefix_total)
        exc_scr[:, ls] = both[:, :LANES] + carry
        return carry + both[:, LANES:]

    total = lax.fori_loop(0, n // LANES, block, jnp.zeros((n_e, LANES), F32))
    padded = jnp.floor((total + (tile - 1)) / tile) * tile
    e_col = lax.broadcasted_iota(I32, (n_e, LANES), 0)
    starts = jnp.zeros((n_e, LANES), F32)
    for e in range(n_e - 1):
        starts = starts + jnp.where(e_col > e, padded[e:e + 1, :], 0.0)
    ends = starts + padded
    slot = starts[:, 0:1] + exc_scr[...]
    dest_ref[0:1, :] = jnp.sum(jnp.where(oh0, slot, 0.0), axis=0, keepdims=True).astype(I32)
    dest_ref[1:2, :] = jnp.sum(jnp.where(oh1, slot, 0.0), axis=0, keepdims=True).astype(I32)
    blk_start = (lax.broadcasted_iota(I32, (n_e, LANES), 1) * tile).astype(F32)
    blk_exp = jnp.sum(jnp.where(ends <= blk_start, 1.0, 0.0), axis=0, keepdims=True)
    meta_ref[0:1, :] = jnp.minimum(blk_exp, n_e - 1.0).astype(I32)
    meta_ref[1:2, :] = (ends[n_e - 1:n_e, :] / tile).astype(I32)
    on_diag = lax.broadcasted_iota(I32, (n_e, LANES), 1) == e_col
    meta_ref[2:3, :] = jnp.sum(jnp.where(on_diag, ends / tile, 0.0), axis=0, keepdims=True).astype(I32)
    meta_ref[3:4, :] = jnp.sum(jnp.where(on_diag, starts / tile, 0.0), axis=0, keepdims=True).astype(I32)
    meta_ref[4:SUBLANES, :] = jnp.zeros((SUBLANES - 4, LANES), I32)


def _route(logits_t, tile):
    n_e, n = logits_t.shape
    return pl.pallas_call(
        functools.partial(_route_kernel, tile=tile),
        out_shape=(jax.ShapeDtypeStruct((2, n), I32), jax.ShapeDtypeStruct((2, n), F32),
                   jax.ShapeDtypeStruct((SUBLANES, LANES), I32)),
        scratch_shapes=[pltpu.VMEM((n_e, n), F32), pltpu.VMEM((n_e, n), F32)],
        compiler_params=pltpu.CompilerParams(vmem_limit_bytes=VMEM_LIMIT),
        name="moe_route",
    )(logits_t)


def _dispatch_kernel(dest_ref, meta_ref, h_ref, wg_ref, wu_ref, wd_ref,
                     xs_ref, wgo_ref, wuo_ref, wdo_ref, zero_scr, sem, zsem, *, tile):
    i = pl.program_id(0)
    td = h_ref.shape[0]

    @pl.when(i == 0)
    def _():
        zero_scr[...] = jnp.zeros_like(zero_scr)
        n_blk = xs_ref.shape[0] // tile

        def zero_block(b):
            return pltpu.make_async_copy(zero_scr, xs_ref.at[pl.ds(pl.multiple_of(b * tile, tile), tile)], zsem)

        def each_block(fn):
            for e in range(N_EXPERTS):
                @pl.when(meta_ref[2, e] > meta_ref[3, e])
                def _():
                    fn(zero_block(meta_ref[2, e] - 1))

            def tail(b, _):
                fn(zero_block(b))
                return 0
            lax.fori_loop(meta_ref[1, 0], n_blk, tail, 0)

        each_block(lambda c: c.start())
        each_block(lambda c: c.wait())

    def copy(r, k):
        row = dest_ref[k * (pl.num_programs(0) * td) + i * td + r]
        return pltpu.make_async_copy(h_ref.at[pl.ds(r, 1)], xs_ref.at[pl.ds(row, 1)], sem)

    def start(r, _):
        copy(r, 0).start()
        copy(r, 1).start()
        return 0

    lax.fori_loop(0, td, start, 0, unroll=DMA_UNROLL)
    wgo_ref[...] = wg_ref[...].astype(BF16)
    wuo_ref[...] = wu_ref[...].astype(BF16)
    wdo_ref[...] = wd_ref[...].astype(BF16)
    for _ in range(2):
        pltpu.make_async_copy(h_ref, xs_ref.at[pl.ds(0, td)], sem).wait()


def _dispatch(dest, meta, h, n_rows, tile, weights):
    n, d = h.shape
    td = min(GATHER_TILE, n)
    n_steps = n // td
    flat = [w.reshape(-1, w.shape[-1]) for w in weights]
    slabs = [w.shape[0] // n_steps for w in flat]
    assert all(w.shape[0] == sl * n_steps and sl % (2 * SUBLANES) == 0 for w, sl in zip(flat, slabs))
    w_specs = [pl.BlockSpec((sl, w.shape[1]), lambda i, s, m: (i, 0)) for w, sl in zip(flat, slabs)]
    grid_spec = pltpu.PrefetchScalarGridSpec(
        num_scalar_prefetch=2,
        grid=(n_steps,),
        in_specs=[pl.BlockSpec((td, d), lambda i, s, m: (i, 0))] + w_specs,
        out_specs=[pl.BlockSpec(memory_space=pl.ANY)] + w_specs,
        scratch_shapes=[pltpu.VMEM((tile, d), h.dtype), pltpu.SemaphoreType.DMA(()),
                        pltpu.SemaphoreType.DMA(())],
    )
    outs = pl.pallas_call(
        functools.partial(_dispatch_kernel, tile=tile),
        grid_spec=grid_spec,
        out_shape=[jax.ShapeDtypeStruct((n_rows, d), h.dtype)]
        + [jax.ShapeDtypeStruct(w.shape, BF16) for w in flat],
        compiler_params=_cparams(("arbitrary",)),
        name="moe_dispatch",
    )(dest.reshape(-1), meta, h, *flat)
    return outs[0], [o.reshape(w.shape) for o, w in zip(outs[1:], weights)]


def _expert_kernel(meta_ref, x_ref, wg_ref, wu_ref, wd_ref, y_ref, xb_scr):
    i = pl.program_id(0)
    j = pl.program_id(1)

    @pl.when(i < meta_ref[1, 0])
    def _():
        @pl.when(j == 0)
        def _():
            xb_scr[...] = x_ref[...].astype(BF16)

        xb = xb_scr[...]
        act = (_silu(_dot(xb, wg_ref[0])) * _dot(xb, wu_ref[0])).astype(BF16)
        part = _dot(act, wd_ref[0])

        @pl.when(j == 0)
        def _():
            y_ref[...] = part

        @pl.when(j > 0)
        def _():
            y_ref[...] += part

    @pl.when((i >= meta_ref[1, 0]) & (j == 0))
    def _():
        y_ref[...] = jnp.zeros_like(y_ref)


def _experts(meta, xs, wg, wu, wd, tile):
    n_rows, d = xs.shape
    d_ff = wg.shape[2]
    tf = d_ff // 2 if (d_ff // 2) % (2 * LANES) == 0 else d_ff
    n_blk = n_rows // tile

    def blk(i, s):
        return jnp.minimum(i, s[1, 0] - 1)

    grid_spec = pltpu.PrefetchScalarGridSpec(
        num_scalar_prefetch=1,
        grid=(n_blk, d_ff // tf),
        in_specs=[
            pl.BlockSpec((tile, d), lambda i, j, s: (blk(i, s), 0)),
            pl.BlockSpec((1, d, tf), lambda i, j, s: (s[0, blk(i, s)], 0, jnp.where(i < s[1, 0], j, 0))),
            pl.BlockSpec((1, d, tf), lambda i, j, s: (s[0, blk(i, s)], 0, jnp.where(i < s[1, 0], j, 0))),
            pl.BlockSpec((1, tf, d), lambda i, j, s: (s[0, blk(i, s)], jnp.where(i < s[1, 0], j, 0), 0)),
        ],
        out_specs=pl.BlockSpec((tile, d), lambda i, j, s: (i, 0)),
        scratch_shapes=[pltpu.VMEM((tile, d), BF16)],
    )
    return pl.pallas_call(
        _expert_kernel,
        grid_spec=grid_spec,
        out_shape=jax.ShapeDtypeStruct((n_rows, d), F32),
        compiler_params=_cparams(("arbitrary", "arbitrary")),
        name="moe_experts",
    )(meta, xs, wg, wu, wd)


def _combine_kernel(dest_ref, y_ref, x_ref, wt_ref, g2_ref, gn_ref, o_ref, buf, sem):
    i = pl.program_id(0)
    n_steps = pl.num_programs(0)
    tc = x_ref.shape[0]
    slot = i % 2

    def gather(step, to_slot):
        def start(r, _):
            for k in range(2):
                row = dest_ref[k * (n_steps * tc) + step * tc + r]
                pltpu.make_async_copy(y_ref.at[pl.ds(row, 1)], buf.at[to_slot, k, pl.ds(r, 1)],
                                      sem.at[to_slot]).start()
            return 0
        lax.fori_loop(0, tc, start, 0, unroll=DMA_UNROLL)

    @pl.when(i == 0)
    def _():
        gather(0, 0)

    @pl.when(i + 1 < n_steps)
    def _():
        gather(i + 1, 1 - slot)

    for k in range(2):
        pltpu.make_async_copy(y_ref.at[pl.ds(0, tc)], buf.at[slot, k], sem.at[slot]).wait()
    wt = wt_ref[...]
    y = buf[slot, 0] * wt[:, 0:1] + buf[slot, 1] * wt[:, 1:2]
    x = x_ref[...] + g2_ref[0] * y
    ms = jnp.mean(x * x, axis=-1, keepdims=True)
    o_ref[...] = x * lax.rsqrt(ms + EPS) * gn_ref[...]


def _combine(dest, y, x1, wt, g2, gn, seq):
    n, d = x1.shape
    tc = min(GATHER_TILE, seq)
    per_b = seq // tc
    grid_spec = pltpu.PrefetchScalarGridSpec(
        num_scalar_prefetch=1,
        grid=(n // tc,),
        in_specs=[
            pl.BlockSpec(memory_space=pl.ANY),
            pl.BlockSpec((tc, d), lambda i, s: (i, 0)),
            pl.BlockSpec((tc, 2), lambda i, s: (i, 0)),
            pl.BlockSpec((1, 1, d), lambda i, s: (i // per_b, 0, 0)),
            pl.BlockSpec((1, d), lambda i, s: (0, 0)),
        ],
        out_specs=pl.BlockSpec((tc, d), lambda i, s: (i, 0)),
        scratch_shapes=[pltpu.VMEM((2, 2, tc, d), F32), pltpu.SemaphoreType.DMA((2,))],
    )
    return pl.pallas_call(
        _combine_kernel,
        grid_spec=grid_spec,
        out_shape=jax.ShapeDtypeStruct((n, d), F32),
        compiler_params=_cparams(("arbitrary",)),
        name="moe_combine_norm",
    )(dest.reshape(-1), y, x1, wt, g2, gn)


def _final_norm_kernel(x_ref, g_ref, o_ref):
    x = x_ref[...]
    ms = jnp.mean(x * x, axis=-1, keepdims=True)
    o_ref[...] = x * lax.rsqrt(ms + EPS) * g_ref[...]


def _mixer(x2, mod, l, batch, seq, norm_mix_g, w_in, b_in, cmp_pe, cmp_w1, cmp_w2,
           conv_w, conv_b, mlstm_norm_g):
    d = x2.shape[1]
    sh1, sc1 = mod[l, :, 0:d], mod[l, :, d:2 * d]
    w, b = _inproj_weights(w_in[l], b_in[l])
    q, kvp, sm, y_ml = _inproj(x2, norm_mix_g[l][None, :], sc1[:, None, :], sh1[:, None, :], w, b,
                               conv_w[l], conv_b[l], mlstm_norm_g[l], seq)
    pe, w1p, w2p = _compress_weights(cmp_pe[l], cmp_w1[l], cmp_w2[l])
    kc = _compress(kvp[0:NSA_KV_HEADS], pe, w1p, w2p, batch, seq)
    gates = sm[:, SM_GATE:SM_GATE + 24].reshape(batch, seq, NSA_KV_HEADS, NSA_REP * 3)
    gates = jnp.pad(gates.transpose(0, 2, 3, 1), ((0, 0), (0, 0), (0, 16 - NSA_REP * 3), (0, 0)))
    y_nsa = _nsa2(q, kvp, kc, gates.reshape(batch * NSA_KV_HEADS, 16, seq), batch, seq)
    return y_nsa, y_ml


def kernel(x, c, ada_w, ada_b, norm_mix_g, norm_ffn_g, w_in, b_in, cmp_pe, cmp_w1, cmp_w2, conv_w, conv_b, mlstm_norm_g, w_out, ffn_w_gate, ffn_w_up, ffn_w_down, router_w, moe_w_gate, moe_w_up, moe_w_down, final_norm_g):
    batch, seq, d = x.shape
    depth = ada_w.shape[0]
    n = batch * seq
    mod = _adaln(c, ada_w, ada_b)
    x2 = x.reshape(n, d)
    for l in range(depth):
        g1 = mod[l, :, 2 * d:3 * d][:, None, :]
        sh2 = mod[l, :, 3 * d:4 * d][:, None, :]
        sc2 = mod[l, :, 4 * d:5 * d][:, None, :]
        g2 = mod[l, :, 5 * d:6 * d][:, None, :]
        y_nsa, y_ml = _mixer(x2, mod, l, batch, seq, norm_mix_g, w_in, b_in, cmp_pe, cmp_w1, cmp_w2,
                             conv_w, conv_b, mlstm_norm_g)
        gn = norm_ffn_g[l][None, :]
        i = l // 2
        last = l == depth - 1
        if l % 2 == 0:
            x2 = _outproj_ffn(x2, y_nsa, y_ml, w_out[l].astype(BF16), g1, gn, sc2, sh2,
                              ffn_w_gate[i].astype(BF16), ffn_w_up[i].astype(BF16),
                              ffn_w_down[i].astype(BF16), g2, seq)
            if last:
                x2 = _final_norm(x2, final_norm_g)
        else:
            x1, h, logits_t = _outproj_router(x2, y_nsa, y_ml, w_out[l].astype(BF16), g1, gn, sc2, sh2, seq,
                                              router_w[i].T)
            n_rows = 2 * n + N_EXPERTS * MOE_TILE
            dest, wt, meta = _route(logits_t, MOE_TILE)
            xs, (wg, wu, wd) = _dispatch(dest, meta, h, n_rows, MOE_TILE,
                                         (moe_w_gate[i], moe_w_up[i], moe_w_down[i]))
            y = _experts(meta, xs, wg, wu, wd, MOE_TILE)
            unit = jnp.ones((1, d), F32)
            x2 = _combine(dest, y, x1, wt.T, g2, final_norm_g[None, :] if last else unit, seq)
            if not last:
                raise NotImplementedError("a MoE layer that is not the last layer")
    return x2.reshape(batch, seq, d)


def _final_norm(x2, g):
    n, d = x2.shape
    tm = min(ROW_TILE, n)
    return pl.pallas_call(
        _final_norm_kernel,
        grid=(n // tm,),
        in_specs=[pl.BlockSpec((tm, d), lambda i: (i, 0)), pl.BlockSpec((1, d), lambda i: (0, 0))],
        out_specs=pl.BlockSpec((tm, d), lambda i: (i, 0)),
        out_shape=jax.ShapeDtypeStruct((n, d), F32),
        compiler_params=_cparams(("parallel",)),
        name="final_norm",
    )(x2, g[None, :])
```

```python
import functools

import numpy as np
import jax
import jax.numpy as jnp
from jax import lax
from jax.experimental import pallas as pl
from jax.experimental.pallas import tpu as pltpu

F32 = jnp.float32
BF16 = jnp.bfloat16
I32 = jnp.int32

NSA_HEADS = 8
NSA_KV_HEADS = 2
NSA_REP = NSA_HEADS // NSA_KV_HEADS
NSA_HEAD_DIM = 64
CMP_LEN = 32
CMP_STRIDE = 16
CMP_HIDDEN = 128
SEL_BLOCK = 64
SEL_SHIFT = 6
SEL_TOP = 16
WINDOW = 512
Q_BLOCK = 256
FORCE_BONUS = 1e4
NEG_INF = -1e30
MLSTM_HEADS = 4
MLSTM_HEAD_DIM = 128
CONV_WIDTH = 4
N_EXPERTS = 8
EPS = 1e-6

LANES = 128
SUBLANES = 8
VMEM_LIMIT = 56 * 1024 * 1024

ROW_TILE = 512
SEL_CHUNK = 512
MLSTM_CHUNK = 256
MOE_TILE = 512
GATHER_TILE = 256
DMA_UNROLL = 8

SM_GATE = 0
SM_I = 24
SM_F = 28


def _cparams(sem, vmem=VMEM_LIMIT):
    return pltpu.CompilerParams(dimension_semantics=sem, vmem_limit_bytes=vmem)


def _sigmoid(x):
    return 1.0 / (1.0 + jnp.exp(-x))


def _silu(x):
    return x * _sigmoid(x)


def _log_sigmoid(x):
    return jnp.minimum(x, 0.0) - jnp.log1p(jnp.exp(-jnp.abs(x)))


def _dot(a, b):
    return jnp.dot(a, b, preferred_element_type=F32)


def _dot_nt(a, b):
    return lax.dot_general(a, b, (((1,), (1,)), ((), ())), preferred_element_type=F32)


def _split3(x):
    hi = x.astype(BF16)
    r1 = x - hi.astype(F32)
    mid = r1.astype(BF16)
    lo = (r1 - mid.astype(F32)).astype(BF16)
    return hi, mid, lo


def _rmsnorm_mod(x, g, sc, sh):
    ms = jnp.mean(x * x, axis=-1, keepdims=True)
    y = x * lax.rsqrt(ms + EPS) * g
    return y * (1.0 + sc) + sh


def _adaln_kernel(c_ref, w_ref, b_ref, o_ref):
    c = c_ref[...]
    ca = _silu(c).astype(BF16)
    o_ref[0] = _dot(ca, w_ref[0].astype(BF16)) + b_ref[0]


def _adaln(c, ada_w, ada_b):
    depth, d, n6 = ada_w.shape
    b = c.shape[0]
    cp = jnp.zeros((SUBLANES, d), F32).at[:b].set(c)
    tn = n6 // 4
    out = pl.pallas_call(
        _adaln_kernel,
        grid=(depth, n6 // tn),
        in_specs=[
            pl.BlockSpec((SUBLANES, d), lambda l, j: (0, 0)),
            pl.BlockSpec((1, d, tn), lambda l, j: (l, 0, j)),
            pl.BlockSpec((1, 1, tn), lambda l, j: (l, 0, j)),
        ],
        out_specs=pl.BlockSpec((1, SUBLANES, tn), lambda l, j: (l, 0, j)),
        out_shape=jax.ShapeDtypeStruct((depth, SUBLANES, n6), F32),
        compiler_params=_cparams(("parallel", "parallel")),
        name="adaln",
    )(cp, ada_w, ada_b.reshape(depth, 1, n6))
    return out[:, :b]


def _inproj_kernel(x_ref, g_ref, sc_ref, sh_ref, wq_ref, wkv_ref, ws_ref, wm_ref, b_ref, cw_ref, cb_ref, gm_ref,
                   q_ref, kv_ref, sm_ref, y_ref,
                   qm_ref, km_ref, v_ref, o_ref, tail_scr, c_scr, n_scr, m_scr, *, per_b, lc):
    i = pl.program_id(0)

    @pl.when(i == 0)
    def _():
        tail_scr[...] = jnp.zeros_like(tail_scr)

    @pl.when(i % per_b == 0)
    def _():
        c_scr[...] = jnp.zeros_like(c_scr)
        n_scr[...] = jnp.zeros_like(n_scr)
        m_scr[...] = jnp.zeros_like(m_scr)

    h = _rmsnorm_mod(x_ref[...], g_ref[...], sc_ref[0], sh_ref[0]).astype(BF16)

    def sec(lo, width):
        for w_ref, base in ((wq_ref, 0), (wkv_ref, 512), (ws_ref, 1280), (wm_ref, 1408)):
            if base <= lo < base + w_ref.shape[1]:
                return _dot(h, w_ref[:, lo - base:lo - base + width]) + b_ref[:, lo:lo + width]
        raise ValueError(lo)

    strip = 2 * LANES
    row8 = lax.broadcasted_iota(I32, (SUBLANES, 1), 0)
    first = i % per_b == 0
    half = qm_ref.shape[1]
    for c0 in range(0, 2 * half, strip):
        cs = slice(c0, c0 + strip)
        cur = sec(1408 + c0, strip)
        tm = cur.shape[0]
        prev = jnp.where(first, 0.0, tail_scr[:, cs])
        tail_scr[:, cs] = cur[tm - SUBLANES:tm, :]
        y = cb_ref[:, cs]
        for tap in range(CONV_WIDTH):
            back = CONV_WIDTH - 1 - tap
            if back:
                rolled = pltpu.roll(cur, back, axis=0)
                top = jnp.where(row8 < back, pltpu.roll(prev, back, axis=0), rolled[0:SUBLANES])
                shifted = jnp.concatenate([top, rolled[SUBLANES:]], axis=0)
            else:
                shifted = cur
            y = y + shifted * cw_ref[tap:tap + 1, cs]
        y = _silu(y)
        if c0 < half:
            qm_ref[:, cs] = y.astype(BF16)
        else:
            km_ref[:, c0 - half:c0 - half + strip] = (y * (MLSTM_HEAD_DIM ** -0.5)).astype(BF16)

    for c0 in range(0, 512, strip):
        cs = slice(c0, c0 + strip)
        v_ref[:, cs] = sec(2432 + c0, strip).astype(BF16)
        o_ref[:, cs] = sec(2944 + c0, strip)
    sm_ref[...] = sec(512 + 6 * LANES, LANES)

    for c in range(tm // lc):
        _mlstm_chunk(slice(c * lc, (c + 1) * lc), qm_ref, km_ref, v_ref, o_ref, sm_ref, gm_ref, y_ref,
                     c_scr, n_scr, m_scr)

    for c0 in range(0, 512, strip):
        cs = slice(c0, c0 + strip)
        q_ref[:, cs] = (sec(c0, strip) * (NSA_HEAD_DIM ** -0.5)).astype(BF16)
    for s in range(0, 6, 2):
        pair = sec(512 + s * LANES, strip)
        kv_ref[s] = pair[:, :LANES].astype(BF16)
        kv_ref[s + 1] = pair[:, LANES:].astype(BF16)


def _inproj_weights(w_in, b_in):
    o_q, o_kv, o_gate, o_qk, o_v, o_o, o_i, o_f = [int(v) for v in np.cumsum((0, 512, 768, 24, 1024, 512, 512, 4))]

    def groups(a):
        kv = a[..., o_kv:o_gate].reshape(a.shape[:-1] + (3, 2, NSA_KV_HEADS, NSA_HEAD_DIM))
        kv = jnp.swapaxes(kv, -3, -2)
        kv = kv.reshape(a.shape[:-1] + (768,))
        small = jnp.concatenate(
            [a[..., o_gate:o_qk], a[..., o_i:o_f], a[..., o_f:o_f + 4],
             jnp.zeros(a.shape[:-1] + (LANES - 32,), a.dtype)], axis=-1)
        return a[..., o_q:o_kv], kv, small, a[..., o_qk:o_i]

    return tuple(w.astype(BF16) for w in groups(w_in)), jnp.concatenate(groups(b_in))[None, :].astype(F32)


def _inproj(x2, g, sc, sh, w, b, conv_w, conv_b, norm_g, seq):
    n, d = x2.shape
    tm = min(ROW_TILE, seq)
    lc = min(MLSTM_CHUNK, tm)
    per_b = seq // tm
    wm = MLSTM_HEADS * MLSTM_HEAD_DIM
    row = lambda i: (i, 0)
    bat = lambda i: (i // per_b, 0, 0)
    fix = lambda i: (0, 0)
    outs = (
        jax.ShapeDtypeStruct((n, 512), BF16),
        jax.ShapeDtypeStruct((6, n, LANES), BF16),
        jax.ShapeDtypeStruct((n, 128), F32),
        jax.ShapeDtypeStruct((n, wm), BF16),
    )
    return pl.pallas_call(
        functools.partial(_inproj_kernel, per_b=per_b, lc=lc),
        grid=(n // tm,),
        in_specs=[
            pl.BlockSpec((tm, d), row),
            pl.BlockSpec((1, d), fix),
            pl.BlockSpec((1, 1, d), bat),
            pl.BlockSpec((1, 1, d), bat),
            *[pl.BlockSpec(wi.shape, fix) for wi in w],
            pl.BlockSpec(b.shape, fix),
            pl.BlockSpec(conv_w.shape, fix),
            pl.BlockSpec((1, conv_w.shape[1]), fix),
            pl.BlockSpec((1, wm), fix),
        ],
        out_specs=(
            pl.BlockSpec((tm, 512), row),
            pl.BlockSpec((6, tm, LANES), lambda i: (0, i, 0)),
            pl.BlockSpec((tm, 128), row),
            pl.BlockSpec((tm, wm), row),
        ),
        out_shape=outs,
        scratch_shapes=[
            pltpu.VMEM((tm, wm), BF16), pltpu.VMEM((tm, wm), BF16),
            pltpu.VMEM((tm, wm), BF16), pltpu.VMEM((tm, wm), F32),
            pltpu.VMEM((SUBLANES, 2 * wm), F32),
            pltpu.VMEM((MLSTM_HEADS, MLSTM_HEAD_DIM, MLSTM_HEAD_DIM), F32),
            pltpu.VMEM((SUBLANES, MLSTM_HEAD_DIM), F32),
            pltpu.VMEM((SUBLANES, LANES), F32),
        ],
        compiler_params=_cparams(("arbitrary",)),
        name="inproj_mlstm",
    )(x2, g, sc, sh, *w, b, conv_w, conv_b[None, :], norm_g[None, :])


def _compress_kernel(ch_ref, pe_ref, w1_ref, w2_ref, o_ref):
    ch = ch_ref[0].astype(F32)
    a0 = (ch + pe_ref[0:1, :]).astype(BF16)
    a1 = (ch + pe_ref[1:2, :]).astype(BF16)
    half = ch.shape[1]
    h0 = _dot(a0, w1_ref[0:half, :])
    h1 = _dot(a1, w1_ref[half:2 * half, :])
    n_chunk = ch.shape[0]
    hid = h0 + pltpu.roll(h1, n_chunk - 1, axis=0)
    o_ref[0] = _dot(_silu(hid).astype(BF16), w2_ref[...]).astype(BF16)


def _compress_weights(cmp_pe, cmp_w1, cmp_w2):
    dh, hid = NSA_HEAD_DIM, CMP_HIDDEN
    pe = jnp.concatenate([cmp_pe[0], cmp_pe[1]], axis=-1)
    pe = pe.reshape(2, CMP_STRIDE * LANES)
    w1 = cmp_w1.reshape(2, CMP_LEN, dh, hid)
    z = jnp.zeros((CMP_LEN, dh, hid), cmp_w1.dtype)
    wk = jnp.concatenate([w1[0], z], axis=1)
    wv = jnp.concatenate([z, w1[1]], axis=1)
    w1p = jnp.concatenate([wk, wv], axis=2).reshape(CMP_LEN * LANES, 2 * hid)
    z2 = jnp.zeros((hid, dh), cmp_w2.dtype)
    w2p = jnp.concatenate([jnp.concatenate([cmp_w2[0], z2], axis=1),
                           jnp.concatenate([z2, cmp_w2[1]], axis=1)], axis=0)
    return pe.astype(F32), w1p.astype(BF16), w2p.astype(BF16)


def _compress(kv_cmp, pe, w1p, w2p, batch, seq):
    g = kv_cmp.shape[0]
    n_chunk = seq // CMP_STRIDE
    ch = kv_cmp.reshape(g * batch, n_chunk, CMP_STRIDE * LANES)
    return pl.pallas_call(
        _compress_kernel,
        grid=(g * batch,),
        in_specs=[
            pl.BlockSpec((1, n_chunk, CMP_STRIDE * LANES), lambda i: (i, 0, 0)),
            pl.BlockSpec(pe.shape, lambda i: (0, 0)),
            pl.BlockSpec(w1p.shape, lambda i: (0, 0)),
            pl.BlockSpec(w2p.shape, lambda i: (0, 0)),
        ],
        out_specs=pl.BlockSpec((1, n_chunk, LANES), lambda i: (i, 0, 0)),
        out_shape=jax.ShapeDtypeStruct((g * batch, n_chunk, LANES), BF16),
        compiler_params=_cparams(("parallel",)),
        name="nsa_compress",
    )(ch, pe, w1p, w2p)


def _nsa2_kernel(slope_ref, q_ref, kc_ref, ks_ref, kw_ref, gate_ref, ovt_ref, feat_ref, featc_ref,
                 o_ref, kaug_c, vt_c, kaug_s, vt_s, kaug_w, vt_w, todo_scr, *, seq):
    qb = pl.program_id(1)
    n_cmp = seq // CMP_STRIDE
    n_sel = seq // SEL_BLOCK
    n_top = min(SEL_TOP, n_sel)
    n_chunk = seq // SEL_CHUNK
    dh = NSA_HEAD_DIM
    cols = NSA_REP * Q_BLOCK
    group_w = NSA_REP * dh
    blocks_per_chunk = SEL_CHUNK // SEL_BLOCK

    @pl.when(qb == 0)
    def _():
        def build(src_ref, f_ref, kaug, vt, n_rows):
            step = min(SEL_CHUNK, n_rows)
            lane = lax.broadcasted_iota(I32, (step, LANES), 1)
            row = lax.broadcasted_iota(I32, (LANES, step), 0)
            for c0 in range(0, n_rows, step):
                x = src_ref[c0:c0 + step, :].astype(F32)
                kaug[c0:c0 + step, :] = jnp.where(lane < dh, x, f_ref[c0:c0 + step, :].astype(F32)).astype(BF16)
                vt[:, c0:c0 + step] = jnp.where(row == 0, 1.0, x.T).astype(BF16)

        for g in range(NSA_KV_HEADS):
            build(kc_ref.at[g, 0], featc_ref, kaug_c.at[g], vt_c.at[g], n_cmp)
            build(ks_ref.at[g], feat_ref, kaug_s.at[g], vt_s.at[g], seq)
            build(kw_ref.at[g], feat_ref, kaug_w.at[g], vt_w.at[g], seq)

    lane = lax.broadcasted_iota(I32, (Q_BLOCK, LANES), 1)
    blk_f = (lane - dh).astype(F32)
    t_lane = qb * Q_BLOCK + lax.broadcasted_iota(I32, (1, Q_BLOCK), 1)
    last = (qb * Q_BLOCK) // SEL_CHUNK

    def tile4(x):
        return jnp.concatenate([x] * NSA_REP, axis=1)

    def sel_scores(g, q_sel, c):
        start = pl.multiple_of(c * SEL_CHUNK, SEL_CHUNK)
        return _dot_nt(kaug_s[g, pl.ds(start, SEL_CHUNK), :], q_sel)

    def sel_update(g, c, s, carry):
        m, acc = carry
        start = pl.multiple_of(c * SEL_CHUNK, SEL_CHUNK)
        m_new = jnp.maximum(m, jnp.max(s, axis=0, keepdims=True))
        p = jnp.exp(s - m_new).astype(BF16)
        acc = jnp.exp(m - m_new) * acc + _dot(vt_s[g, :, pl.ds(start, SEL_CHUNK)], p)
        return m_new, acc

    def front(g):
        qf = q_ref[:, g * group_w:(g + 1) * group_w].astype(F32)
        parts = []
        for r in range(NSA_REP):
            pair = qf[:, (r // 2) * LANES:(r // 2 + 1) * LANES]
            if r % 2:
                pair = pltpu.roll(pair, dh, axis=1)
            slope = slope_ref[g * NSA_REP + r]
            parts.append(jnp.where(lane < dh, pair, jnp.where(lane == dh, slope, slope * SEL_BLOCK * blk_f)))
        q_all_f = jnp.concatenate(parts, axis=0)
        q_all = q_all_f.astype(BF16)

        s = _dot_nt(kaug_c[g], q_all)
        end_c = lax.broadcasted_iota(I32, (n_cmp, Q_BLOCK), 0) * CMP_STRIDE + (CMP_LEN - 1)
        s = s + tile4(jnp.where(end_c <= t_lane, 0.0, NEG_INF))
        e = jnp.exp(s - jnp.max(s, axis=0, keepdims=True))
        p_c = e / jnp.sum(e, axis=0, keepdims=True)
        p_c = p_c * tile4((t_lane >= CMP_LEN - 1).astype(F32))
        o_c = _dot(vt_c[g], p_c.astype(BF16))

        p4 = p_c[:, 0:Q_BLOCK]
        for r in range(1, NSA_REP):
            p4 = p4 + p_c[:, r * Q_BLOCK:(r + 1) * Q_BLOCK]
        ovt = ovt_ref[...]
        imp = sum(_dot(ovt, piece) for piece in _split3(p4))
        j_col = lax.broadcasted_iota(I32, (n_sel, 1), 0)
        cur = jnp.right_shift(t_lane, SEL_SHIFT)
        forced = (j_col == 0) | (j_col == cur) | (j_col == cur - 1)
        imp = jnp.where(forced, imp + FORCE_BONUS, imp)
        imp = jnp.where(j_col <= cur, imp, -1.0)
        groups = [imp[v * SUBLANES:(v + 1) * SUBLANES, :] for v in range(n_sel // SUBLANES)]
        j_grp = lax.broadcasted_iota(I32, (SUBLANES, Q_BLOCK), 0)
        ranks = [jnp.zeros((SUBLANES, Q_BLOCK), F32) for _ in groups]
        for k in range(n_sel):
            row_k = groups[k // SUBLANES][k % SUBLANES:k % SUBLANES + 1, :]
            for v, grp in enumerate(groups):
                ge = jnp.where(row_k >= grp, 1.0, 0.0)
                gt = jnp.where(row_k > grp, 1.0, 0.0)
                if v * SUBLANES > k:
                    inc = ge
                elif (v + 1) * SUBLANES - 1 < k:
                    inc = gt
                else:
                    inc = jnp.where(j_grp + v * SUBLANES > k, ge, gt)
                ranks[v] = ranks[v] + inc
        sel_t = jnp.where(jnp.concatenate(ranks, axis=0) < n_top, 1.0, 0.0)
        pieces = [jnp.zeros((dh, Q_BLOCK), F32), sel_t]
        if n_sel < dh:
            pieces.append(jnp.zeros((dh - n_sel, Q_BLOCK), F32))
        selmat = jnp.concatenate(pieces, axis=0).T
        drop = jnp.concatenate([jnp.where(lane > dh, selmat, 1.0)] * NSA_REP, axis=0) < 0.5
        q_sel = jnp.where(drop, NEG_INF, q_all_f).astype(BF16)

        n_todo = jnp.int32(0)
        for c in range(n_chunk - 1):
            rows_c = sel_t[c * blocks_per_chunk:(c + 1) * blocks_per_chunk, :]
            wanted = (jnp.max(rows_c) > 0.5) & (c < last)
            todo_scr[g * n_chunk + n_todo] = jnp.int32(c)
            n_todo = n_todo + wanted.astype(I32)

        span = Q_BLOCK + WINDOW
        start = pl.multiple_of(jnp.maximum(qb * Q_BLOCK - WINDOW, 0), Q_BLOCK)
        s = _dot_nt(kaug_w[g, pl.ds(start, span), :], q_all)
        dist = t_lane - (start + lax.broadcasted_iota(I32, (span, Q_BLOCK), 0))
        in_band = pltpu.bitcast(dist, jnp.uint32) < WINDOW
        s = s + tile4(jnp.where(in_band, 0.0, NEG_INF))
        p = jnp.exp(s - jnp.max(s, axis=0, keepdims=True)).astype(BF16)
        acc = _dot(vt_w[g, :, pl.ds(start, span)], p)
        return q_sel, n_todo, o_c, acc / acc[0:1, :]

    def middle(g, q_sel, n_todo):
        def pair(i, carry):
            c0, c1 = todo_scr[g * n_chunk + 2 * i], todo_scr[g * n_chunk + 2 * i + 1]
            s0, s1 = sel_scores(g, q_sel, c0), sel_scores(g, q_sel, c1)
            return sel_update(g, c1, s1, sel_update(g, c0, s0, carry))

        def single(k, carry):
            c = todo_scr[g * n_chunk + k]
            return sel_update(g, c, sel_scores(g, q_sel, c), carry)

        init = (jnp.full((1, cols), NEG_INF, F32), jnp.zeros((LANES, cols), F32))
        carry = lax.fori_loop(0, n_todo // 2, pair, init)
        return lax.fori_loop(2 * (n_todo // 2), n_todo, single, carry)

    def back(g, q_sel, carry, o_c, o_w):
        pos = last * SEL_CHUNK + lax.broadcasted_iota(I32, (SEL_CHUNK, Q_BLOCK), 0)
        s = sel_scores(g, q_sel, last) + tile4(jnp.where(pos <= t_lane, 0.0, NEG_INF))
        _, acc = sel_update(g, last, s, carry)
        o_s = acc / acc[0:1, :]
        gates = _sigmoid(gate_ref[0, g])
        mixed = []
        for r in range(NSA_REP):
            cs = slice(r * Q_BLOCK, (r + 1) * Q_BLOCK)
            mixed.append(gates[3 * r:3 * r + 1, :] * o_c[dh:, cs]
                         + gates[3 * r + 1:3 * r + 2, :] * o_s[dh:, cs]
                         + gates[3 * r + 2:3 * r + 3, :] * o_w[dh:, cs])
        for pr in range(NSA_REP // 2):
            pair_t = jnp.concatenate([mixed[2 * pr], mixed[2 * pr + 1]], axis=0)
            lo = g * group_w + pr * LANES
            o_ref[:, lo:lo + LANES] = pair_t.T.astype(o_ref.dtype)

    fronts = [front(g) for g in range(NSA_KV_HEADS)]
    carries = [middle(g, fronts[g][0], fronts[g][1]) for g in range(NSA_KV_HEADS)]
    for g in range(NSA_KV_HEADS):
        q_sel, _, o_c, o_w = fronts[g]
        back(g, q_sel, carries[g], o_c, o_w)


def _nsa_constants(seq):
    n_cmp, n_sel = seq // CMP_STRIDE, seq // SEL_BLOCK
    assert n_sel <= NSA_HEAD_DIM, "one feature lane per selection block"
    slopes = 2.0 ** (-8.0 * np.arange(1, NSA_HEADS + 1) / NSA_HEADS)
    far = (slopes[:, None] * SEL_BLOCK * np.arange(n_sel)[None, :]).astype(np.float32)
    assert np.array_equal(far.astype(BF16).astype(np.float32), far), "ALiBi features must be exact in bf16"
    lo_c = np.arange(n_cmp)[:, None] * CMP_STRIDE
    lo_s = np.arange(n_sel)[None, :] * SEL_BLOCK
    ov = np.clip(np.minimum(lo_c + CMP_LEN, lo_s + SEL_BLOCK) - np.maximum(lo_c, lo_s), 0, None) / CMP_LEN
    ov[n_cmp - 1] = 0.0

    def feats(pos):
        f = np.zeros((pos.shape[0], LANES), np.float32)
        f[:, NSA_HEAD_DIM] = pos % SEL_BLOCK
        blk = pos // SEL_BLOCK
        ok = (blk >= 1) & (blk < NSA_HEAD_DIM)
        f[np.nonzero(ok)[0], NSA_HEAD_DIM + blk[ok]] = 1.0
        return f

    feat = feats(np.arange(seq))
    pos_c = np.arange(n_cmp) * CMP_STRIDE + CMP_LEN - 1
    featc = feats(pos_c)
    featc[pos_c >= seq] = 0.0
    return (jnp.asarray(slopes, F32), jnp.asarray(ov.T, BF16), jnp.asarray(feat, BF16), jnp.asarray(featc, BF16))


def _nsa2(q, kvp, kc, gates_t, batch, seq):
    n, width = q.shape
    g = NSA_KV_HEADS
    n_qb = seq // Q_BLOCK
    n_cmp = seq // CMP_STRIDE
    slopes, ovt, feat, featc = _nsa_constants(seq)
    fix = lambda b, i, s: (0, 0)
    branch = lambda br: (lambda b, i, s: (br, b, 0))
    grid_spec = pltpu.PrefetchScalarGridSpec(
        num_scalar_prefetch=1,
        grid=(batch, n_qb),
        in_specs=[
            pl.BlockSpec((Q_BLOCK, width), lambda b, i, s: (b * n_qb + i, 0)),
            pl.BlockSpec((g, 1, n_cmp, LANES), lambda b, i, s: (0, b, 0, 0)),
            pl.BlockSpec((g, seq, LANES), branch(1)),
            pl.BlockSpec((g, seq, LANES), branch(2)),
            pl.BlockSpec((1, g, 16, Q_BLOCK), lambda b, i, s: (b, 0, 0, i)),
            pl.BlockSpec(ovt.shape, fix),
            pl.BlockSpec(feat.shape, fix),
            pl.BlockSpec(featc.shape, fix),
        ],
        out_specs=pl.BlockSpec((Q_BLOCK, width), lambda b, i, s: (b * n_qb + i, 0)),
        scratch_shapes=[
            pltpu.VMEM((g, n_cmp, LANES), BF16), pltpu.VMEM((g, LANES, n_cmp), BF16),
            pltpu.VMEM((g, seq, LANES), BF16), pltpu.VMEM((g, LANES, seq), BF16),
            pltpu.VMEM((g, seq, LANES), BF16), pltpu.VMEM((g, LANES, seq), BF16),
            pltpu.SMEM((g * (seq // SEL_CHUNK),), I32),
        ],
    )
    return pl.pallas_call(
        functools.partial(_nsa2_kernel, seq=seq),
        grid_spec=grid_spec,
        out_shape=jax.ShapeDtypeStruct((n, width), BF16),
        compiler_params=_cparams(("parallel", "arbitrary")),
        name="nsa_attention",
    )(slopes, q, kc.reshape(g, batch, n_cmp, LANES), kvp, kvp,
      gates_t.reshape(batch, g, 16, seq), ovt, feat, featc)


def _mlstm_chunk(rows, q_ref, k_ref, v_ref, o_ref, sm_ref, g_ref, y_ref, c_scr, n_scr, m_scr):
    lc = rows.stop - rows.start
    d = MLSTM_HEAD_DIM
    sm = sm_ref[rows, :]
    lf = _log_sigmoid(sm)
    ri = lax.broadcasted_iota(I32, (lc, lc), 0)
    ci = lax.broadcasted_iota(I32, (lc, lc), 1)
    tri = jnp.where(ri >= ci, 1.0, 0.0).astype(BF16)
    a_col = sum(_dot(tri, piece) for piece in _split3(lf))
    a_row = a_col.T

    for h in range(MLSTM_HEADS):
        hs = slice(h * d, (h + 1) * d)
        a_j = a_row[SM_F + h:SM_F + h + 1, :]
        gap_s = sm[:, SM_I + h:SM_I + h + 1] - a_col[:, SM_F + h:SM_F + h + 1]
        m_prev = m_scr[h:h + 1, 0:1]
        qh, kh, vh = q_ref[rows, hs], k_ref[rows, hs], v_ref[rows, hs]
        v_t = vh.astype(F32).T.astype(BF16)
        c_prev = c_scr[h]
        n_prev = n_scr[h:h + 1, :]

        log_d = jnp.where(ri <= ci, a_j + gap_s, -jnp.inf)
        m_inter = a_j + m_prev
        m_t = jnp.maximum(m_inter, jnp.max(log_d, axis=0, keepdims=True))
        d_mat = jnp.exp(log_d - m_t)
        inter = jnp.exp(m_inter - m_t)
        s_qk = _dot_nt(kh, qh) * d_mat
        num = inter * _dot_nt(c_prev.astype(BF16), qh) + _dot(v_t, s_qk.astype(BF16))
        n_rows = jnp.broadcast_to(n_prev, (SUBLANES, d)).astype(BF16)
        den = inter * _dot_nt(n_rows, qh)[0:1, :] + jnp.sum(s_qk, axis=0, keepdims=True)
        hh = (num / jnp.maximum(jnp.abs(den), jnp.exp(-m_t))).T

        a_last = a_j[:, lc - 1:lc]
        log_w = a_last + gap_s
        m_new = jnp.maximum(a_last + m_prev, jnp.max(log_w, axis=0, keepdims=True))
        wk = jnp.exp(log_w - m_new) * kh.astype(F32)
        decay = jnp.exp(a_last + m_prev - m_new)
        c_scr[h] = decay * c_prev + _dot(v_t, wk.astype(BF16))
        n_scr[h:h + 1, :] = decay * n_prev + jnp.sum(wk, axis=0, keepdims=True)
        m_scr[h:h + 1, :] = jnp.broadcast_to(m_new, (1, LANES))

        hg = _sigmoid(o_ref[rows, hs]) * hh
        hn = hg * lax.rsqrt(jnp.mean(hg * hg, axis=-1, keepdims=True) + EPS)
        y_ref[rows, hs] = (hn * g_ref[:, hs]).astype(y_ref.dtype)


def _mix_out(x_ref, ya_ref, yb_ref, w_ref, g1_ref, gn_ref, sc_ref, sh_ref):
    half = ya_ref.shape[1]
    y = _dot(ya_ref[...], w_ref[0:half, :]) + _dot(yb_ref[...], w_ref[half:2 * half, :])
    x1 = x_ref[...] + g1_ref[0] * y
    return x1, _rmsnorm_mod(x1, gn_ref[...], sc_ref[0], sh_ref[0])


def _mix_specs(x2, ya, yb, w_out, tm, per_b):
    d = x2.shape[1]
    row = lambda i: (i, 0)
    bat = lambda i: (i // per_b, 0, 0)
    fix = lambda i: (0, 0)
    return [
        pl.BlockSpec((tm, d), row),
        pl.BlockSpec((tm, ya.shape[1]), row),
        pl.BlockSpec((tm, yb.shape[1]), row),
        pl.BlockSpec(w_out.shape, fix),
        pl.BlockSpec((1, 1, d), bat),
        pl.BlockSpec((1, d), fix),
        pl.BlockSpec((1, 1, d), bat),
        pl.BlockSpec((1, 1, d), bat),
    ]


def _outproj_router_kernel(x_ref, ya_ref, yb_ref, w_ref, g1_ref, gn_ref, sc_ref, sh_ref, rw_ref,
                           x1_ref, h_ref, lg_ref):
    x1, h = _mix_out(x_ref, ya_ref, yb_ref, w_ref, g1_ref, gn_ref, sc_ref, sh_ref)
    x1_ref[...] = x1
    h_ref[...] = h
    lg_ref[...] = lax.dot_general(rw_ref[...], h, (((1,), (1,)), ((), ())),
                                  precision=lax.Precision.HIGHEST, preferred_element_type=F32)


def _outproj_router(x2, ya, yb, w_out, g1, gn, sc, sh, seq, router_wt):
    n, d = x2.shape
    tm = min(ROW_TILE, seq)
    per_b = seq // tm
    row = lambda i: (i, 0)
    return pl.pallas_call(
        _outproj_router_kernel,
        grid=(n // tm,),
        in_specs=_mix_specs(x2, ya, yb, w_out, tm, per_b) + [pl.BlockSpec(router_wt.shape, lambda i: (0, 0))],
        out_specs=(pl.BlockSpec((tm, d), row), pl.BlockSpec((tm, d), row),
                   pl.BlockSpec((N_EXPERTS, tm), lambda i: (0, i))),
        out_shape=(jax.ShapeDtypeStruct((n, d), F32), jax.ShapeDtypeStruct((n, d), F32),
                   jax.ShapeDtypeStruct((N_EXPERTS, n), F32)),
        compiler_params=_cparams(("parallel",)),
        name="outproj_router",
    )(x2, ya, yb, w_out, g1, gn, sc, sh, router_wt)


def _outproj_ffn_kernel(x_ref, ya_ref, yb_ref, w_ref, g1_ref, gn_ref, sc_ref, sh_ref,
                        wg_ref, wu_ref, wd_ref, g2_ref, o_ref, act_scr, *, tf):
    x1, h = _mix_out(x_ref, ya_ref, yb_ref, w_ref, g1_ref, gn_ref, sc_ref, sh_ref)
    h = h.astype(BF16)
    d_ff = wg_ref.shape[1]
    for j in range(d_ff // tf):
        cs = slice(j * tf, (j + 1) * tf)
        act_scr[:, cs] = (_silu(_dot(h, wg_ref[:, cs])) * _dot(h, wu_ref[:, cs])).astype(BF16)
    o_ref[...] = x1 + g2_ref[0] * _dot(act_scr[...], wd_ref[...])


def _outproj_ffn(x2, ya, yb, w_out, g1, gn, sc, sh, wg, wu, wd, g2, seq):
    n, d = x2.shape
    d_ff = wg.shape[1]
    tm = min(ROW_TILE, seq)
    per_b = seq // tm
    tf = 2 * LANES
    row = lambda i: (i, 0)
    fix = lambda i: (0, 0)
    once = dict(pipeline_mode=pl.Buffered(1))
    return pl.pallas_call(
        functools.partial(_outproj_ffn_kernel, tf=tf),
        grid=(n // tm,),
        in_specs=_mix_specs(x2, ya, yb, w_out, tm, per_b) + [
            pl.BlockSpec(wg.shape, fix, **once),
            pl.BlockSpec(wu.shape, fix, **once),
            pl.BlockSpec(wd.shape, fix, **once),
            pl.BlockSpec((1, 1, d), lambda i: (i // per_b, 0, 0)),
        ],
        out_specs=pl.BlockSpec((tm, d), row),
        out_shape=jax.ShapeDtypeStruct((n, d), F32),
        scratch_shapes=[pltpu.VMEM((tm, d_ff), BF16)],
        compiler_params=_cparams(("parallel",)),
        name="outproj_ffn",
    )(x2, ya, yb, w_out, g1, gn, sc, sh, wg, wu, wd, g2)


def _route_kernel(lg_ref, dest_ref, wt_ref, meta_ref, cnt_scr, exc_scr, *, tile):
    n_e, n = lg_ref.shape
    lg = lg_ref[...]
    e_iota = lax.broadcasted_iota(I32, (n_e, n), 0)
    m1 = jnp.max(lg, axis=0, keepdims=True)
    e0 = jnp.min(jnp.where(lg == m1, e_iota, n_e), axis=0, keepdims=True)
    lg2 = jnp.where(e_iota == e0, -jnp.inf, lg)
    m2 = jnp.max(lg2, axis=0, keepdims=True)
    e1 = jnp.min(jnp.where(lg2 == m2, e_iota, n_e), axis=0, keepdims=True)
    ex = jnp.exp(m2 - m1)
    wt_ref[0:1, :] = 1.0 / (1.0 + ex)
    wt_ref[1:2, :] = ex / (1.0 + ex)
    oh0 = e_iota == e0
    oh1 = e_iota == e1
    cnt_scr[...] = jnp.where(oh0, 1.0, 0.0) + jnp.where(oh1, 1.0, 0.0)

    ri = lax.broadcasted_iota(I32, (LANES, 2 * LANES), 0)
    ci = lax.broadcasted_iota(I32, (LANES, 2 * LANES), 1)
    prefix_total = jnp.where((ci >= LANES) | (ri < ci), 1.0, 0.0).astype(BF16)

    def block(kb, carry):
        ls = pl.ds(pl.multiple_of(kb * LANES, LANES), LANES)
        both = _dot(cnt_scr[:, ls].astype(BF16), prefix_total)
        exc_scr[:, ls] = both[:, :LANES] + carry
        return carry + both[:, LANES:]

    total = lax.fori_loop(0, n // LANES, block, jnp.zeros((n_e, LANES), F32))
    padded = jnp.floor((total + (tile - 1)) / tile) * tile
    e_col = lax.broadcasted_iota(I32, (n_e, LANES), 0)
    starts = jnp.zeros((n_e, LANES), F32)
    for e in range(n_e - 1):
        starts = starts + jnp.where(e_col > e, padded[e:e + 1, :], 0.0)
    ends = starts + padded
    slot = starts[:, 0:1] + exc_scr[...]
    dest_ref[0:1, :] = jnp.sum(jnp.where(oh0, slot, 0.0), axis=0, keepdims=True).astype(I32)
    dest_ref[1:2, :] = jnp.sum(jnp.where(oh1, slot, 0.0), axis=0, keepdims=True).astype(I32)
    blk_start = (lax.broadcasted_iota(I32, (n_e, LANES), 1) * tile).astype(F32)
    blk_exp = jnp.sum(jnp.where(ends <= blk_start, 1.0, 0.0), axis=0, keepdims=True)
    meta_ref[0:1, :] = jnp.minimum(blk_exp, n_e - 1.0).astype(I32)
    meta_ref[1:2, :] = (ends[n_e - 1:n_e, :] / tile).astype(I32)
    on_diag = lax.broadcasted_iota(I32, (n_e, LANES), 1) == e_col
    meta_ref[2:3, :] = jnp.sum(jnp.where(on_diag, ends / tile, 0.0), axis=0, keepdims=True).astype(I32)
    meta_ref[3:4, :] = jnp.sum(jnp.where(on_diag, starts / tile, 0.0), axis=0, keepdims=True).astype(I32)
    meta_ref[4:SUBLANES, :] = jnp.zeros((SUBLANES - 4, LANES), I32)


def _route(logits_t, tile):
    n_e, n = logits_t.shape
    return pl.pallas_call(
        functools.partial(_route_kernel, tile=tile),
        out_shape=(jax.ShapeDtypeStruct((2, n), I32), jax.ShapeDtypeStruct((2, n), F32),
                   jax.ShapeDtypeStruct((SUBLANES, LANES), I32)),
        scratch_shapes=[pltpu.VMEM((n_e, n), F32), pltpu.VMEM((n_e, n), F32)],
        compiler_params=pltpu.CompilerParams(vmem_limit_bytes=VMEM_LIMIT),
        name="moe_route",
    )(logits_t)


def _dispatch_kernel(dest_ref, meta_ref, h_ref, wg_ref, wu_ref, wd_ref,
                     xs_ref, wgo_ref, wuo_ref, wdo_ref, zero_scr, sem, zsem, *, tile):
    i = pl.program_id(0)
    td = h_ref.shape[0]

    @pl.when(i == 0)
    def _():
        zero_scr[...] = jnp.zeros_like(zero_scr)
        n_blk = xs_ref.shape[0] // tile

        def zero_block(b):
            return pltpu.make_async_copy(zero_scr, xs_ref.at[pl.ds(pl.multiple_of(b * tile, tile), tile)], zsem)

        def each_block(fn):
            for e in range(N_EXPERTS):
                @pl.when(meta_ref[2, e] > meta_ref[3, e])
                def _():
                    fn(zero_block(meta_ref[2, e] - 1))

            def tail(b, _):
                fn(zero_block(b))
                return 0
            lax.fori_loop(meta_ref[1, 0], n_blk, tail, 0)

        each_block(lambda c: c.start())
        each_block(lambda c: c.wait())

    def copy(r, k):
        row = dest_ref[k * (pl.num_programs(0) * td) + i * td + r]
        return pltpu.make_async_copy(h_ref.at[pl.ds(r, 1)], xs_ref.at[pl.ds(row, 1)], sem)

    def start(r, _):
        copy(r, 0).start()
        copy(r, 1).start()
        return 0

    lax.fori_loop(0, td, start, 0, unroll=DMA_UNROLL)
    wgo_ref[...] = wg_ref[...].astype(BF16)
    wuo_ref[...] = wu_ref[...].astype(BF16)
    wdo_ref[...] = wd_ref[...].astype(BF16)
    for _ in range(2):
        pltpu.make_async_copy(h_ref, xs_ref.at[pl.ds(0, td)], sem).wait()


def _dispatch(dest, meta, h, n_rows, tile, weights):
    n, d = h.shape
    td = min(GATHER_TILE, n)
    n_steps = n // td
    flat = [w.reshape(-1, w.shape[-1]) for w in weights]
    slabs = [w.shape[0] // n_steps for w in flat]
    assert all(w.shape[0] == sl * n_steps and sl % (2 * SUBLANES) == 0 for w, sl in zip(flat, slabs))
    w_specs = [pl.BlockSpec((sl, w.shape[1]), lambda i, s, m: (i, 0)) for w, sl in zip(flat, slabs)]
    grid_spec = pltpu.PrefetchScalarGridSpec(
        num_scalar_prefetch=2,
        grid=(n_steps,),
        in_specs=[pl.BlockSpec((td, d), lambda i, s, m: (i, 0))] + w_specs,
        out_specs=[pl.BlockSpec(memory_space=pl.ANY)] + w_specs,
        scratch_shapes=[pltpu.VMEM((tile, d), h.dtype), pltpu.SemaphoreType.DMA(()),
                        pltpu.SemaphoreType.DMA(())],
    )
    outs = pl.pallas_call(
        functools.partial(_dispatch_kernel, tile=tile),
        grid_spec=grid_spec,
        out_shape=[jax.ShapeDtypeStruct((n_rows, d), h.dtype)]
        + [jax.ShapeDtypeStruct(w.shape, BF16) for w in flat],
        compiler_params=_cparams(("arbitrary",)),
        name="moe_dispatch",
    )(dest.reshape(-1), meta, h, *flat)
    return outs[0], [o.reshape(w.shape) for o, w in zip(outs[1:], weights)]


def _expert_kernel(meta_ref, x_ref, wg_ref, wu_ref, wd_ref, y_ref, xb_scr):
    i = pl.program_id(0)
    j = pl.program_id(1)

    @pl.when(i < meta_ref[1, 0])
    def _():
        @pl.when(j == 0)
        def _():
            xb_scr[...] = x_ref[...].astype(BF16)

        xb = xb_scr[...]
        act = (_silu(_dot(xb, wg_ref[0])) * _dot(xb, wu_ref[0])).astype(BF16)
        part = _dot(act, wd_ref[0])

        @pl.when(j == 0)
        def _():
            y_ref[...] = part

        @pl.when(j > 0)
        def _():
            y_ref[...] += part

    @pl.when((i >= meta_ref[1, 0]) & (j == 0))
    def _():
        y_ref[...] = jnp.zeros_like(y_ref)


def _experts(meta, xs, wg, wu, wd, tile):
    n_rows, d = xs.shape
    d_ff = wg.shape[2]
    tf = d_ff // 2 if (d_ff // 2) % (2 * LANES) == 0 else d_ff
    n_blk = n_rows // tile

    def blk(i, s):
        return jnp.minimum(i, s[1, 0] - 1)

    grid_spec = pltpu.PrefetchScalarGridSpec(
        num_scalar_prefetch=1,
        grid=(n_blk, d_ff // tf),
        in_specs=[
            pl.BlockSpec((tile, d), lambda i, j, s: (blk(i, s), 0)),
            pl.BlockSpec((1, d, tf), lambda i, j, s: (s[0, blk(i, s)], 0, jnp.where(i < s[1, 0], j, 0))),
            pl.BlockSpec((1, d, tf), lambda i, j, s: (s[0, blk(i, s)], 0, jnp.where(i < s[1, 0], j, 0))),
            pl.BlockSpec((1, tf, d), lambda i, j, s: (s[0, blk(i, s)], jnp.where(i < s[1, 0], j, 0), 0)),
        ],
        out_specs=pl.BlockSpec((tile, d), lambda i, j, s: (i, 0)),
        scratch_shapes=[pltpu.VMEM((tile, d), BF16)],
    )
    return pl.pallas_call(
        _expert_kernel,
        grid_spec=grid_spec,
        out_shape=jax.ShapeDtypeStruct((n_rows, d), F32),
        compiler_params=_cparams(("arbitrary", "arbitrary")),
        name="moe_experts",
    )(meta, xs, wg, wu, wd)


def _combine_kernel(dest_ref, y_ref, x_ref, wt_ref, g2_ref, gn_ref, o_ref, buf, sem):
    i = pl.program_id(0)
    n_steps = pl.num_programs(0)
    tc = x_ref.shape[0]
    slot = i % 2

    def gather(step, to_slot):
        def start(r, _):
            for k in range(2):
                row = dest_ref[k * (n_steps * tc) + step * tc + r]
                pltpu.make_async_copy(y_ref.at[pl.ds(row, 1)], buf.at[to_slot, k, pl.ds(r, 1)],
                                      sem.at[to_slot]).start()
            return 0
        lax.fori_loop(0, tc, start, 0, unroll=DMA_UNROLL)

    @pl.when(i == 0)
    def _():
        gather(0, 0)

    @pl.when(i + 1 < n_steps)
    def _():
        gather(i + 1, 1 - slot)

    for k in range(2):
        pltpu.make_async_copy(y_ref.at[pl.ds(0, tc)], buf.at[slot, k], sem.at[slot]).wait()
    wt = wt_ref[...]
    y = buf[slot, 0] * wt[:, 0:1] + buf[slot, 1] * wt[:, 1:2]
    x = x_ref[...] + g2_ref[0] * y
    ms = jnp.mean(x * x, axis=-1, keepdims=True)
    o_ref[...] = x * lax.rsqrt(ms + EPS) * gn_ref[...]


def _combine(dest, y, x1, wt, g2, gn, seq):
    n, d = x1.shape
    tc = min(GATHER_TILE, seq)
    per_b = seq // tc
    grid_spec = pltpu.PrefetchScalarGridSpec(
        num_scalar_prefetch=1,
        grid=(n // tc,),
        in_specs=[
            pl.BlockSpec(memory_space=pl.ANY),
            pl.BlockSpec((tc, d), lambda i, s: (i, 0)),
            pl.BlockSpec((tc, 2), lambda i, s: (i, 0)),
            pl.BlockSpec((1, 1, d), lambda i, s: (i // per_b, 0, 0)),
            pl.BlockSpec((1, d), lambda i, s: (0, 0)),
        ],
        out_specs=pl.BlockSpec((tc, d), lambda i, s: (i, 0)),
        scratch_shapes=[pltpu.VMEM((2, 2, tc, d), F32), pltpu.SemaphoreType.DMA((2,))],
    )
    return pl.pallas_call(
        _combine_kernel,
        grid_spec=grid_spec,
        out_shape=jax.ShapeDtypeStruct((n, d), F32),
        compiler_params=_cparams(("arbitrary",)),
        name="moe_combine_norm",
    )(dest.reshape(-1), y, x1, wt, g2, gn)


def _final_norm_kernel(x_ref, g_ref, o_ref):
    x = x_ref[...]
    ms = jnp.mean(x * x, axis=-1, keepdims=True)
    o_ref[...] = x * lax.rsqrt(ms + EPS) * g_ref[...]


def _mixer(x2, mod, l, batch, seq, norm_mix_g, w_in, b_in, cmp_pe, cmp_w1, cmp_w2,
           conv_w, conv_b, mlstm_norm_g):
    d = x2.shape[1]
    sh1, sc1 = mod[l, :, 0:d], mod[l, :, d:2 * d]
    w, b = _inproj_weights(w_in[l], b_in[l])
    q, kvp, sm, y_ml = _inproj(x2, norm_mix_g[l][None, :], sc1[:, None, :], sh1[:, None, :], w, b,
                               conv_w[l], conv_b[l], mlstm_norm_g[l], seq)
    pe, w1p, w2p = _compress_weights(cmp_pe[l], cmp_w1[l], cmp_w2[l])
    kc = _compress(kvp[0:NSA_KV_HEADS], pe, w1p, w2p, batch, seq)
    gates = sm[:, SM_GATE:SM_GATE + 24].reshape(batch, seq, NSA_KV_HEADS, NSA_REP * 3)
    gates = jnp.pad(gates.transpose(0, 2, 3, 1), ((0, 0), (0, 0), (0, 16 - NSA_REP * 3), (0, 0)))
    y_nsa = _nsa2(q, kvp, kc, gates.reshape(batch * NSA_KV_HEADS, 16, seq), batch, seq)
    return y_nsa, y_ml


def kernel(x, c, ada_w, ada_b, norm_mix_g, norm_ffn_g, w_in, b_in, cmp_pe, cmp_w1, cmp_w2, conv_w, conv_b, mlstm_norm_g, w_out, ffn_w_gate, ffn_w_up, ffn_w_down, router_w, moe_w_gate, moe_w_up, moe_w_down, final_norm_g):
    batch, seq, d = x.shape
    depth = ada_w.shape[0]
    n = batch * seq
    mod = _adaln(c, ada_w, ada_b)
    x2 = x.reshape(n, d)
    for l in range(depth):
        g1 = mod[l, :, 2 * d:3 * d][:, None, :]
        sh2 = mod[l, :, 3 * d:4 * d][:, None, :]
        sc2 = mod[l, :, 4 * d:5 * d][:, None, :]
        g2 = mod[l, :, 5 * d:6 * d][:, None, :]
        y_nsa, y_ml = _mixer(x2, mod, l, batch, seq, norm_mix_g, w_in, b_in, cmp_pe, cmp_w1, cmp_w2,
                             conv_w, conv_b, mlstm_norm_g)
        gn = norm_ffn_g[l][None, :]
        i = l // 2
        last = l == depth - 1
        if l % 2 == 0:
            x2 = _outproj_ffn(x2, y_nsa, y_ml, w_out[l].astype(BF16), g1, gn, sc2, sh2,
                              ffn_w_gate[i].astype(BF16), ffn_w_up[i].astype(BF16),
                              ffn_w_down[i].astype(BF16), g2, seq)
            if last:
                x2 = _final_norm(x2, final_norm_g)
        else:
            x1, h, logits_t = _outproj_router(x2, y_nsa, y_ml, w_out[l].astype(BF16), g1, gn, sc2, sh2, seq,
                                              router_w[i].T)
            n_rows = 2 * n + N_EXPERTS * MOE_TILE
            dest, wt, meta = _route(logits_t, MOE_TILE)
            xs, (wg, wu, wd) = _dispatch(dest, meta, h, n_rows, MOE_TILE,
                                         (moe_w_gate[i], moe_w_up[i], moe_w_down[i]))
            y = _experts(meta, xs, wg, wu, wd, MOE_TILE)
            unit = jnp.ones((1, d), F32)
            x2 = _combine(dest, y, x1, wt.T, g2, final_norm_g[None, :] if last else unit, seq)
            if not last:
                raise NotImplementedError("a MoE layer that is not the last layer")
    return x2.reshape(batch, seq, d)


def _final_norm(x2, g):
    n, d = x2.shape
    tm = min(ROW_TILE, n)
    return pl.pallas_call(
        _final_norm_kernel,
        grid=(n // tm,),
        in_specs=[pl.BlockSpec((tm, d), lambda i: (i, 0)), pl.BlockSpec((1, d), lambda i: (0, 0))],
        out_specs=pl.BlockSpec((tm, d), lambda i: (i, 0)),
        out_shape=jax.ShapeDtypeStruct((n, d), F32),
        compiler_params=_cparams(("parallel",)),
        name="final_norm",
    )(x2, g[None, :])
```

```python
import functools

import numpy as np
import jax
import jax.numpy as jnp
from jax import lax
from jax.experimental import pallas as pl
from jax.experimental.pallas import tpu as pltpu

F32 = jnp.float32
BF16 = jnp.bfloat16
I32 = jnp.int32

NSA_HEADS = 8
NSA_KV_HEADS = 2
NSA_REP = NSA_HEADS // NSA_KV_HEADS
NSA_HEAD_DIM = 64
CMP_LEN = 32
CMP_STRIDE = 16
CMP_HIDDEN = 128
SEL_BLOCK = 64
SEL_SHIFT = 6
SEL_TOP = 16
WINDOW = 512
Q_BLOCK = 256
FORCE_BONUS = 1e4
NEG_INF = -1e30
MLSTM_HEADS = 4
MLSTM_HEAD_DIM = 128
CONV_WIDTH = 4
N_EXPERTS = 8
EPS = 1e-6

LANES = 128
SUBLANES = 8
VMEM_LIMIT = 56 * 1024 * 1024

ROW_TILE = 512
SEL_CHUNK = 512
MLSTM_CHUNK = 256
MOE_TILE = 512
GATHER_TILE = 256
DMA_UNROLL = 8
VT_HEAD = 16

SM_GATE = 0
SM_I = 24
SM_F = 28


def _cparams(sem, vmem=VMEM_LIMIT):
    return pltpu.CompilerParams(dimension_semantics=sem, vmem_limit_bytes=vmem)


def _sigmoid(x):
    return 1.0 / (1.0 + jnp.exp(-x))


def _silu(x):
    return x * _sigmoid(x)


def _log_sigmoid(x):
    return jnp.minimum(x, 0.0) - jnp.log1p(jnp.exp(-jnp.abs(x)))


def _dot(a, b):
    return jnp.dot(a, b, preferred_element_type=F32)


def _dot_nt(a, b):
    return lax.dot_general(a, b, (((1,), (1,)), ((), ())), preferred_element_type=F32)


def _split3(x):
    hi = x.astype(BF16)
    r1 = x - hi.astype(F32)
    mid = r1.astype(BF16)
    lo = (r1 - mid.astype(F32)).astype(BF16)
    return hi, mid, lo


def _rmsnorm_mod(x, g, sc, sh):
    ms = jnp.mean(x * x, axis=-1, keepdims=True)
    y = x * lax.rsqrt(ms + EPS) * g
    return y * (1.0 + sc) + sh


def _adaln_kernel(c_ref, w_ref, b_ref, o_ref):
    c = c_ref[...]
    ca = _silu(c).astype(BF16)
    o_ref[0] = _dot(ca, w_ref[0].astype(BF16)) + b_ref[0]


def _adaln(c, ada_w, ada_b):
    depth, d, n6 = ada_w.shape
    b = c.shape[0]
    cp = jnp.zeros((SUBLANES, d), F32).at[:b].set(c)
    tn = n6 // 4
    out = pl.pallas_call(
        _adaln_kernel,
        grid=(depth, n6 // tn),
        in_specs=[
            pl.BlockSpec((SUBLANES, d), lambda l, j: (0, 0)),
            pl.BlockSpec((1, d, tn), lambda l, j: (l, 0, j)),
            pl.BlockSpec((1, 1, tn), lambda l, j: (l, 0, j)),
        ],
        out_specs=pl.BlockSpec((1, SUBLANES, tn), lambda l, j: (l, 0, j)),
        out_shape=jax.ShapeDtypeStruct((depth, SUBLANES, n6), F32),
        compiler_params=_cparams(("parallel", "parallel")),
        name="adaln",
    )(cp, ada_w, ada_b.reshape(depth, 1, n6))
    return out[:, :b]


def _inproj_kernel(x_ref, g_ref, sc_ref, sh_ref, wq_ref, wkv_ref, ws_ref, wm_ref, b_ref, cw_ref, cb_ref, gm_ref,
                   q_ref, kv_ref, sm_ref, y_ref,
                   qm_ref, km_ref, v_ref, o_ref, tail_scr, c_scr, n_scr, m_scr, *, per_b, lc):
    i = pl.program_id(0)

    @pl.when(i == 0)
    def _():
        tail_scr[...] = jnp.zeros_like(tail_scr)

    @pl.when(i % per_b == 0)
    def _():
        c_scr[...] = jnp.zeros_like(c_scr)
        n_scr[...] = jnp.zeros_like(n_scr)
        m_scr[...] = jnp.zeros_like(m_scr)

    h = _rmsnorm_mod(x_ref[...], g_ref[...], sc_ref[0], sh_ref[0]).astype(BF16)

    def sec(lo, width):
        for w_ref, base in ((wq_ref, 0), (wkv_ref, 512), (ws_ref, 1280), (wm_ref, 1408)):
            if base <= lo < base + w_ref.shape[1]:
                return _dot(h, w_ref[:, lo - base:lo - base + width]) + b_ref[:, lo:lo + width]
        raise ValueError(lo)

    strip = 2 * LANES
    row8 = lax.broadcasted_iota(I32, (SUBLANES, 1), 0)
    first = i % per_b == 0
    half = qm_ref.shape[1]
    for c0 in range(0, 2 * half, strip):
        cs = slice(c0, c0 + strip)
        cur = sec(1408 + c0, strip)
        tm = cur.shape[0]
        prev = jnp.where(first, 0.0, tail_scr[:, cs])
        tail_scr[:, cs] = cur[tm - SUBLANES:tm, :]
        y = cb_ref[:, cs]
        for tap in range(CONV_WIDTH):
            back = CONV_WIDTH - 1 - tap
            if back:
                rolled = pltpu.roll(cur, back, axis=0)
                top = jnp.where(row8 < back, pltpu.roll(prev, back, axis=0), rolled[0:SUBLANES])
                shifted = jnp.concatenate([top, rolled[SUBLANES:]], axis=0)
            else:
                shifted = cur
            y = y + shifted * cw_ref[tap:tap + 1, cs]
        y = _silu(y)
        if c0 < half:
            qm_ref[:, cs] = y.astype(BF16)
        else:
            km_ref[:, c0 - half:c0 - half + strip] = (y * (MLSTM_HEAD_DIM ** -0.5)).astype(BF16)

    for c0 in range(0, 512, strip):
        cs = slice(c0, c0 + strip)
        v_ref[:, cs] = sec(2432 + c0, strip).astype(BF16)
        o_ref[:, cs] = sec(2944 + c0, strip)
    sm_ref[...] = sec(512 + 6 * LANES, LANES)

    for c in range(tm // lc):
        _mlstm_chunk(slice(c * lc, (c + 1) * lc), qm_ref, km_ref, v_ref, o_ref, sm_ref, gm_ref, y_ref,
                     c_scr, n_scr, m_scr)

    for c0 in range(0, 512, strip):
        cs = slice(c0, c0 + strip)
        q_ref[:, cs] = (sec(c0, strip) * (NSA_HEAD_DIM ** -0.5)).astype(BF16)
    for s in range(0, 6, 2):
        pair = sec(512 + s * LANES, strip)
        kv_ref[s] = pair[:, :LANES].astype(BF16)
        kv_ref[s + 1] = pair[:, LANES:].astype(BF16)


def _inproj_weights(w_in, b_in):
    o_q, o_kv, o_gate, o_qk, o_v, o_o, o_i, o_f = [int(v) for v in np.cumsum((0, 512, 768, 24, 1024, 512, 512, 4))]

    def groups(a):
        kv = a[..., o_kv:o_gate].reshape(a.shape[:-1] + (3, 2, NSA_KV_HEADS, NSA_HEAD_DIM))
        kv = jnp.swapaxes(kv, -3, -2)
        kv = kv.reshape(a.shape[:-1] + (768,))
        small = jnp.concatenate(
            [a[..., o_gate:o_qk], a[..., o_i:o_f], a[..., o_f:o_f + 4],
             jnp.zeros(a.shape[:-1] + (LANES - 32,), a.dtype)], axis=-1)
        return a[..., o_q:o_kv], kv, small, a[..., o_qk:o_i]

    return tuple(w.astype(BF16) for w in groups(w_in)), jnp.concatenate(groups(b_in))[None, :].astype(F32)


def _inproj(x2, g, sc, sh, w, b, conv_w, conv_b, norm_g, seq):
    n, d = x2.shape
    tm = min(ROW_TILE, seq)
    lc = min(MLSTM_CHUNK, tm)
    per_b = seq // tm
    wm = MLSTM_HEADS * MLSTM_HEAD_DIM
    row = lambda i: (i, 0)
    bat = lambda i: (i // per_b, 0, 0)
    fix = lambda i: (0, 0)
    outs = (
        jax.ShapeDtypeStruct((n, 512), BF16),
        jax.ShapeDtypeStruct((6, n, LANES), BF16),
        jax.ShapeDtypeStruct((n, 128), F32),
        jax.ShapeDtypeStruct((n, wm), BF16),
    )
    return pl.pallas_call(
        functools.partial(_inproj_kernel, per_b=per_b, lc=lc),
        grid=(n // tm,),
        in_specs=[
            pl.BlockSpec((tm, d), row),
            pl.BlockSpec((1, d), fix),
            pl.BlockSpec((1, 1, d), bat),
            pl.BlockSpec((1, 1, d), bat),
            *[pl.BlockSpec(wi.shape, fix) for wi in w],
            pl.BlockSpec(b.shape, fix),
            pl.BlockSpec(conv_w.shape, fix),
            pl.BlockSpec((1, conv_w.shape[1]), fix),
            pl.BlockSpec((1, wm), fix),
        ],
        out_specs=(
            pl.BlockSpec((tm, 512), row),
            pl.BlockSpec((6, tm, LANES), lambda i: (0, i, 0)),
            pl.BlockSpec((tm, 128), row),
            pl.BlockSpec((tm, wm), row),
        ),
        out_shape=outs,
        scratch_shapes=[
            pltpu.VMEM((tm, wm), BF16), pltpu.VMEM((tm, wm), BF16),
            pltpu.VMEM((tm, wm), BF16), pltpu.VMEM((tm, wm), F32),
            pltpu.VMEM((SUBLANES, 2 * wm), F32),
            pltpu.VMEM((MLSTM_HEADS, MLSTM_HEAD_DIM, MLSTM_HEAD_DIM), F32),
            pltpu.VMEM((SUBLANES, MLSTM_HEAD_DIM), F32),
            pltpu.VMEM((SUBLANES, LANES), F32),
        ],
        compiler_params=_cparams(("arbitrary",)),
        name="inproj_mlstm",
    )(x2, g, sc, sh, *w, b, conv_w, conv_b[None, :], norm_g[None, :])


def _compress_kernel(ch_ref, pe_ref, w1_ref, w2_ref, o_ref):
    ch = ch_ref[0].astype(F32)
    a0 = (ch + pe_ref[0:1, :]).astype(BF16)
    a1 = (ch + pe_ref[1:2, :]).astype(BF16)
    half = ch.shape[1]
    h0 = _dot(a0, w1_ref[0:half, :])
    h1 = _dot(a1, w1_ref[half:2 * half, :])
    n_chunk = ch.shape[0]
    hid = h0 + pltpu.roll(h1, n_chunk - 1, axis=0)
    o_ref[0] = _dot(_silu(hid).astype(BF16), w2_ref[...]).astype(BF16)


def _compress_weights(cmp_pe, cmp_w1, cmp_w2):
    dh, hid = NSA_HEAD_DIM, CMP_HIDDEN
    pe = jnp.concatenate([cmp_pe[0], cmp_pe[1]], axis=-1)
    pe = pe.reshape(2, CMP_STRIDE * LANES)
    w1 = cmp_w1.reshape(2, CMP_LEN, dh, hid)
    z = jnp.zeros((CMP_LEN, dh, hid), cmp_w1.dtype)
    wk = jnp.concatenate([w1[0], z], axis=1)
    wv = jnp.concatenate([z, w1[1]], axis=1)
    w1p = jnp.concatenate([wk, wv], axis=2).reshape(CMP_LEN * LANES, 2 * hid)
    z2 = jnp.zeros((hid, dh), cmp_w2.dtype)
    w2p = jnp.concatenate([jnp.concatenate([cmp_w2[0], z2], axis=1),
                           jnp.concatenate([z2, cmp_w2[1]], axis=1)], axis=0)
    return pe.astype(F32), w1p.astype(BF16), w2p.astype(BF16)


def _compress(kv_cmp, pe, w1p, w2p, batch, seq):
    g = kv_cmp.shape[0]
    n_chunk = seq // CMP_STRIDE
    ch = kv_cmp.reshape(g * batch, n_chunk, CMP_STRIDE * LANES)
    return pl.pallas_call(
        _compress_kernel,
        grid=(g * batch,),
        in_specs=[
            pl.BlockSpec((1, n_chunk, CMP_STRIDE * LANES), lambda i: (i, 0, 0)),
            pl.BlockSpec(pe.shape, lambda i: (0, 0)),
            pl.BlockSpec(w1p.shape, lambda i: (0, 0)),
            pl.BlockSpec(w2p.shape, lambda i: (0, 0)),
        ],
        out_specs=pl.BlockSpec((1, n_chunk, LANES), lambda i: (i, 0, 0)),
        out_shape=jax.ShapeDtypeStruct((g * batch, n_chunk, LANES), BF16),
        compiler_params=_cparams(("parallel",)),
        name="nsa_compress",
    )(ch, pe, w1p, w2p)


def _nsa2_kernel(slope_ref, q_ref, kc_ref, ks_ref, kw_ref, gate_ref, ovt_ref, feat_ref, featc_ref,
                 o_ref, kaug_c, vt_c, kaug_s, vt_s, kaug_w, vt_w, todo_scr, *, seq):
    qb = pl.program_id(1)
    n_cmp = seq // CMP_STRIDE
    n_sel = seq // SEL_BLOCK
    n_top = min(SEL_TOP, n_sel)
    n_chunk = seq // SEL_CHUNK
    dh = NSA_HEAD_DIM
    cols = NSA_REP * Q_BLOCK
    group_w = NSA_REP * dh
    blocks_per_chunk = SEL_CHUNK // SEL_BLOCK

    @pl.when(qb == 0)
    def _():
        def build(src_ref, f_ref, kaug, vt, n_rows):
            step = min(SEL_CHUNK, n_rows)
            lane = lax.broadcasted_iota(I32, (step, LANES), 1)
            ones_row = jnp.where(lax.broadcasted_iota(I32, (VT_HEAD, step), 0) == 0, 1.0, 0.0)
            for c0 in range(0, n_rows, step):
                x = src_ref[c0:c0 + step, :].astype(F32)
                kaug[c0:c0 + step, :] = jnp.where(lane < dh, x, f_ref[c0:c0 + step, :].astype(F32)).astype(BF16)
                vt[:, c0:c0 + step] = jnp.concatenate([ones_row, x.T[dh:]], axis=0).astype(BF16)

        for g in range(NSA_KV_HEADS):
            build(kc_ref.at[g, 0], featc_ref, kaug_c.at[g], vt_c.at[g], n_cmp)
            build(ks_ref.at[g], feat_ref, kaug_s.at[g], vt_s.at[g], seq)
            build(kw_ref.at[g], feat_ref, kaug_w.at[g], vt_w.at[g], seq)

    lane = lax.broadcasted_iota(I32, (Q_BLOCK, LANES), 1)
    blk_f = (lane - dh).astype(F32)
    t_lane = qb * Q_BLOCK + lax.broadcasted_iota(I32, (1, Q_BLOCK), 1)
    last = (qb * Q_BLOCK) // SEL_CHUNK

    def tile4(x):
        return jnp.concatenate([x] * NSA_REP, axis=1)

    def sel_scores(g, q_sel, c, size=SEL_CHUNK):
        start = pl.multiple_of(c * SEL_CHUNK, SEL_CHUNK)
        return _dot_nt(kaug_s[g, pl.ds(start, size), :], q_sel)

    def sel_update(g, c, s, carry):
        m, acc = carry
        start = pl.multiple_of(c * SEL_CHUNK, SEL_CHUNK)
        m_new = jnp.maximum(m, jnp.max(s, axis=0, keepdims=True))
        p = jnp.exp(s - m_new).astype(BF16)
        acc = jnp.exp(m - m_new) * acc + _dot(vt_s[g, :, pl.ds(start, s.shape[0])], p)
        return m_new, acc

    def front(g):
        qf = q_ref[:, g * group_w:(g + 1) * group_w].astype(F32)
        parts = []
        for r in range(NSA_REP):
            pair = qf[:, (r // 2) * LANES:(r // 2 + 1) * LANES]
            if r % 2:
                pair = pltpu.roll(pair, dh, axis=1)
            slope = slope_ref[g * NSA_REP + r]
            parts.append(jnp.where(lane < dh, pair, jnp.where(lane == dh, slope, slope * SEL_BLOCK * blk_f)))
        q_all_f = jnp.concatenate(parts, axis=0)
        q_all = q_all_f.astype(BF16)

        s = _dot_nt(kaug_c[g], q_all)
        end_c = lax.broadcasted_iota(I32, (n_cmp, Q_BLOCK), 0) * CMP_STRIDE + (CMP_LEN - 1)
        s = s + tile4(jnp.where(end_c <= t_lane, 0.0, NEG_INF))
        e = jnp.exp(s - jnp.max(s, axis=0, keepdims=True))
        any_valid = tile4((t_lane >= CMP_LEN - 1).astype(F32))
        p_c = e * (any_valid / jnp.sum(e, axis=0, keepdims=True))
        o_c = _dot(vt_c[g], p_c.astype(BF16))

        p4 = p_c[:, 0:Q_BLOCK]
        for r in range(1, NSA_REP):
            p4 = p4 + p_c[:, r * Q_BLOCK:(r + 1) * Q_BLOCK]
        ovt = ovt_ref[...]
        imp = sum(_dot(ovt, piece) for piece in _split3(p4))
        j_col = lax.broadcasted_iota(I32, (n_sel, 1), 0)
        cur = jnp.right_shift(t_lane, SEL_SHIFT)
        forced = (j_col == 0) | (j_col == cur) | (j_col == cur - 1)
        imp = jnp.where(forced, imp + FORCE_BONUS, imp)
        imp = jnp.where(j_col <= cur, imp, -1.0)
        groups = [imp[v * SUBLANES:(v + 1) * SUBLANES, :] for v in range(n_sel // SUBLANES)]
        j_grp = lax.broadcasted_iota(I32, (SUBLANES, Q_BLOCK), 0)
        ranks = [jnp.zeros((SUBLANES, Q_BLOCK), F32) for _ in groups]
        for k in range(n_sel):
            row_k = groups[k // SUBLANES][k % SUBLANES:k % SUBLANES + 1, :]
            for v, grp in enumerate(groups):
                ge = jnp.where(row_k >= grp, 1.0, 0.0)
                gt = jnp.where(row_k > grp, 1.0, 0.0)
                if v * SUBLANES > k:
                    inc = ge
                elif (v + 1) * SUBLANES - 1 < k:
                    inc = gt
                else:
                    inc = jnp.where(j_grp + v * SUBLANES > k, ge, gt)
                ranks[v] = ranks[v] + inc
        sel_t = jnp.where(jnp.concatenate(ranks, axis=0) < n_top, 1.0, 0.0)
        pieces = [jnp.zeros((dh, Q_BLOCK), F32), sel_t]
        if n_sel < dh:
            pieces.append(jnp.zeros((dh - n_sel, Q_BLOCK), F32))
        selmat = jnp.concatenate(pieces, axis=0).T
        drop = jnp.concatenate([jnp.where(lane > dh, selmat, 1.0)] * NSA_REP, axis=0) < 0.5
        q_sel = jnp.where(drop, NEG_INF, q_all_f).astype(BF16)

        n_todo = jnp.int32(0)
        for c in range(n_chunk - 1):
            rows_c = sel_t[c * blocks_per_chunk:(c + 1) * blocks_per_chunk, :]
            wanted = (jnp.max(rows_c) > 0.5) & (c < last)
            todo_scr[g * n_chunk + n_todo] = jnp.int32(c)
            n_todo = n_todo + wanted.astype(I32)

        span = Q_BLOCK + WINDOW
        start = pl.multiple_of(jnp.maximum(qb * Q_BLOCK - WINDOW, 0), Q_BLOCK)
        s = _dot_nt(kaug_w[g, pl.ds(start, span), :], q_all)
        dist = t_lane - (start + lax.broadcasted_iota(I32, (span, Q_BLOCK), 0))
        in_band = pltpu.bitcast(dist, jnp.uint32) < WINDOW
        s = s + tile4(jnp.where(in_band, 0.0, NEG_INF))
        p = jnp.exp(s - jnp.max(s, axis=0, keepdims=True)).astype(BF16)
        acc = _dot(vt_w[g, :, pl.ds(start, span)], p)
        return q_sel, n_todo, o_c, acc / acc[0:1, :]

    def middle(g, q_sel, n_todo):
        def pair(i, carry):
            c0, c1 = todo_scr[g * n_chunk + 2 * i], todo_scr[g * n_chunk + 2 * i + 1]
            s0, s1 = sel_scores(g, q_sel, c0), sel_scores(g, q_sel, c1)
            return sel_update(g, c1, s1, sel_update(g, c0, s0, carry))

        def single(k, carry):
            c = todo_scr[g * n_chunk + k]
            return sel_update(g, c, sel_scores(g, q_sel, c), carry)

        init = (jnp.full((1, cols), NEG_INF, F32), jnp.zeros((VT_HEAD + dh, cols), F32))
        carry = lax.fori_loop(0, n_todo // 2, pair, init)
        return lax.fori_loop(2 * (n_todo // 2), n_todo, single, carry)

    def back(g, q_sel, carry, o_c, o_w, n_keys):
        pos = last * SEL_CHUNK + lax.broadcasted_iota(I32, (n_keys, Q_BLOCK), 0)
        s = sel_scores(g, q_sel, last, n_keys) + tile4(jnp.where(pos <= t_lane, 0.0, NEG_INF))
        _, acc = sel_update(g, last, s, carry)
        o_s = acc / acc[0:1, :]
        gates = _sigmoid(gate_ref[0, g])
        mixed = []
        for r in range(NSA_REP):
            cs = slice(r * Q_BLOCK, (r + 1) * Q_BLOCK)
            mixed.append(gates[3 * r:3 * r + 1, :] * o_c[VT_HEAD:, cs]
                         + gates[3 * r + 1:3 * r + 2, :] * o_s[VT_HEAD:, cs]
                         + gates[3 * r + 2:3 * r + 3, :] * o_w[VT_HEAD:, cs])
        for pr in range(NSA_REP // 2):
            pair_t = jnp.concatenate([mixed[2 * pr], mixed[2 * pr + 1]], axis=0)
            lo = g * group_w + pr * LANES
            o_ref[:, lo:lo + LANES] = pair_t.T.astype(o_ref.dtype)

    fronts = [front(g) for g in range(NSA_KV_HEADS)]
    carries = [middle(g, fronts[g][0], fronts[g][1]) for g in range(NSA_KV_HEADS)]
    per_chunk = SEL_CHUNK // Q_BLOCK
    for v in range(per_chunk):
        @pl.when(qb % per_chunk == v)
        def _():
            for g in range(NSA_KV_HEADS):
                q_sel, _, o_c, o_w = fronts[g]
                back(g, q_sel, carries[g], o_c, o_w, (v + 1) * Q_BLOCK)


def _nsa_constants(seq):
    n_cmp, n_sel = seq // CMP_STRIDE, seq // SEL_BLOCK
    assert n_sel <= NSA_HEAD_DIM, "one feature lane per selection block"
    slopes = 2.0 ** (-8.0 * np.arange(1, NSA_HEADS + 1) / NSA_HEADS)
    far = (slopes[:, None] * SEL_BLOCK * np.arange(n_sel)[None, :]).astype(np.float32)
    assert np.array_equal(far.astype(BF16).astype(np.float32), far), "ALiBi features must be exact in bf16"
    lo_c = np.arange(n_cmp)[:, None] * CMP_STRIDE
    lo_s = np.arange(n_sel)[None, :] * SEL_BLOCK
    ov = np.clip(np.minimum(lo_c + CMP_LEN, lo_s + SEL_BLOCK) - np.maximum(lo_c, lo_s), 0, None) / CMP_LEN
    ov[n_cmp - 1] = 0.0

    def feats(pos):
        f = np.zeros((pos.shape[0], LANES), np.float32)
        f[:, NSA_HEAD_DIM] = pos % SEL_BLOCK
        blk = pos // SEL_BLOCK
        ok = (blk >= 1) & (blk < NSA_HEAD_DIM)
        f[np.nonzero(ok)[0], NSA_HEAD_DIM + blk[ok]] = 1.0
        return f

    feat = feats(np.arange(seq))
    pos_c = np.arange(n_cmp) * CMP_STRIDE + CMP_LEN - 1
    featc = feats(pos_c)
    featc[pos_c >= seq] = 0.0
    return (jnp.asarray(slopes, F32), jnp.asarray(ov.T, BF16), jnp.asarray(feat, BF16), jnp.asarray(featc, BF16))


def _nsa2(q, kvp, kc, gates_t, batch, seq):
    n, width = q.shape
    g = NSA_KV_HEADS
    n_qb = seq // Q_BLOCK
    n_cmp = seq // CMP_STRIDE
    vt_rows = VT_HEAD + NSA_HEAD_DIM
    slopes, ovt, feat, featc = _nsa_constants(seq)
    fix = lambda b, i, s: (0, 0)
    branch = lambda br: (lambda b, i, s: (br, b, 0))
    grid_spec = pltpu.PrefetchScalarGridSpec(
        num_scalar_prefetch=1,
        grid=(batch, n_qb),
        in_specs=[
            pl.BlockSpec((Q_BLOCK, width), lambda b, i, s: (b * n_qb + i, 0)),
            pl.BlockSpec((g, 1, n_cmp, LANES), lambda b, i, s: (0, b, 0, 0)),
            pl.BlockSpec((g, seq, LANES), branch(1)),
            pl.BlockSpec((g, seq, LANES), branch(2)),
            pl.BlockSpec((1, g, 16, Q_BLOCK), lambda b, i, s: (b, 0, 0, i)),
            pl.BlockSpec(ovt.shape, fix),
            pl.BlockSpec(feat.shape, fix),
            pl.BlockSpec(featc.shape, fix),
        ],
        out_specs=pl.BlockSpec((Q_BLOCK, width), lambda b, i, s: (b * n_qb + i, 0)),
        scratch_shapes=[
            pltpu.VMEM((g, n_cmp, LANES), BF16), pltpu.VMEM((g, vt_rows, n_cmp), BF16),
            pltpu.VMEM((g, seq, LANES), BF16), pltpu.VMEM((g, vt_rows, seq), BF16),
            pltpu.VMEM((g, seq, LANES), BF16), pltpu.VMEM((g, vt_rows, seq), BF16),
            pltpu.SMEM((g * (seq // SEL_CHUNK),), I32),
        ],
    )
    return pl.pallas_call(
        functools.partial(_nsa2_kernel, seq=seq),
        grid_spec=grid_spec,
        out_shape=jax.ShapeDtypeStruct((n, width), BF16),
        compiler_params=_cparams(("parallel", "arbitrary")),
        name="nsa_attention",
    )(slopes, q, kc.reshape(g, batch, n_cmp, LANES), kvp, kvp,
      gates_t.reshape(batch, g, 16, seq), ovt, feat, featc)


def _mlstm_chunk(rows, q_ref, k_ref, v_ref, o_ref, sm_ref, g_ref, y_ref, c_scr, n_scr, m_scr):
    lc = rows.stop - rows.start
    d = MLSTM_HEAD_DIM
    sm = sm_ref[rows, :]
    lf = _log_sigmoid(sm)
    ri = lax.broadcasted_iota(I32, (lc, lc), 0)
    ci = lax.broadcasted_iota(I32, (lc, lc), 1)
    tri = jnp.where(ri >= ci, 1.0, 0.0).astype(BF16)
    a_col = sum(_dot(tri, piece) for piece in _split3(lf))
    a_row = a_col.T

    for h in range(MLSTM_HEADS):
        hs = slice(h * d, (h + 1) * d)
        a_j = a_row[SM_F + h:SM_F + h + 1, :]
        gap_s = sm[:, SM_I + h:SM_I + h + 1] - a_col[:, SM_F + h:SM_F + h + 1]
        m_prev = m_scr[h:h + 1, 0:1]
        qh, kh, vh = q_ref[rows, hs], k_ref[rows, hs], v_ref[rows, hs]
        v_t = vh.astype(F32).T.astype(BF16)
        c_prev = c_scr[h]
        n_prev = n_scr[h:h + 1, :]

        log_d = jnp.where(ri <= ci, a_j + gap_s, -jnp.inf)
        m_inter = a_j + m_prev
        m_t = jnp.maximum(m_inter, jnp.max(log_d, axis=0, keepdims=True))
        d_mat = jnp.exp(log_d - m_t)
        inter = jnp.exp(m_inter - m_t)
        s_qk = _dot_nt(kh, qh) * d_mat
        num = inter * _dot_nt(c_prev.astype(BF16), qh) + _dot(v_t, s_qk.astype(BF16))
        n_rows = jnp.broadcast_to(n_prev, (SUBLANES, d)).astype(BF16)
        den = inter * _dot_nt(n_rows, qh)[0:1, :] + jnp.sum(s_qk, axis=0, keepdims=True)
        hh = (num / jnp.maximum(jnp.abs(den), jnp.exp(-m_t))).T

        a_last = a_j[:, lc - 1:lc]
        log_w = a_last + gap_s
        m_new = jnp.maximum(a_last + m_prev, jnp.max(log_w, axis=0, keepdims=True))
        wk = jnp.exp(log_w - m_new) * kh.astype(F32)
        decay = jnp.exp(a_last + m_prev - m_new)
        c_scr[h] = decay * c_prev + _dot(v_t, wk.astype(BF16))
        n_scr[h:h + 1, :] = decay * n_prev + jnp.sum(wk, axis=0, keepdims=True)
        m_scr[h:h + 1, :] = jnp.broadcast_to(m_new, (1, LANES))

        hg = _sigmoid(o_ref[rows, hs]) * hh
        hn = hg * lax.rsqrt(jnp.mean(hg * hg, axis=-1, keepdims=True) + EPS)
        y_ref[rows, hs] = (hn * g_ref[:, hs]).astype(y_ref.dtype)


def _mix_out(x_ref, ya_ref, yb_ref, w_ref, g1_ref, gn_ref, sc_ref, sh_ref):
    half = ya_ref.shape[1]
    y = _dot(ya_ref[...], w_ref[0:half, :]) + _dot(yb_ref[...], w_ref[half:2 * half, :])
    x1 = x_ref[...] + g1_ref[0] * y
    return x1, _rmsnorm_mod(x1, gn_ref[...], sc_ref[0], sh_ref[0])


def _mix_specs(x2, ya, yb, w_out, tm, per_b):
    d = x2.shape[1]
    row = lambda i: (i, 0)
    bat = lambda i: (i // per_b, 0, 0)
    fix = lambda i: (0, 0)
    return [
        pl.BlockSpec((tm, d), row),
        pl.BlockSpec((tm, ya.shape[1]), row),
        pl.BlockSpec((tm, yb.shape[1]), row),
        pl.BlockSpec(w_out.shape, fix),
        pl.BlockSpec((1, 1, d), bat),
        pl.BlockSpec((1, d), fix),
        pl.BlockSpec((1, 1, d), bat),
        pl.BlockSpec((1, 1, d), bat),
    ]


def _outproj_router_kernel(x_ref, ya_ref, yb_ref, w_ref, g1_ref, gn_ref, sc_ref, sh_ref, rw_ref,
                           x1_ref, h_ref, lg_ref):
    x1, h = _mix_out(x_ref, ya_ref, yb_ref, w_ref, g1_ref, gn_ref, sc_ref, sh_ref)
    x1_ref[...] = x1
    h_ref[...] = h
    lg_ref[...] = lax.dot_general(rw_ref[...], h, (((1,), (1,)), ((), ())),
                                  precision=lax.Precision.HIGHEST, preferred_element_type=F32)


def _outproj_router(x2, ya, yb, w_out, g1, gn, sc, sh, seq, router_wt):
    n, d = x2.shape
    tm = min(ROW_TILE, seq)
    per_b = seq // tm
    row = lambda i: (i, 0)
    return pl.pallas_call(
        _outproj_router_kernel,
        grid=(n // tm,),
        in_specs=_mix_specs(x2, ya, yb, w_out, tm, per_b) + [pl.BlockSpec(router_wt.shape, lambda i: (0, 0))],
        out_specs=(pl.BlockSpec((tm, d), row), pl.BlockSpec((tm, d), row),
                   pl.BlockSpec((N_EXPERTS, tm), lambda i: (0, i))),
        out_shape=(jax.ShapeDtypeStruct((n, d), F32), jax.ShapeDtypeStruct((n, d), F32),
                   jax.ShapeDtypeStruct((N_EXPERTS, n), F32)),
        compiler_params=_cparams(("parallel",)),
        name="outproj_router",
    )(x2, ya, yb, w_out, g1, gn, sc, sh, router_wt)


def _outproj_ffn_kernel(x_ref, ya_ref, yb_ref, w_ref, g1_ref, gn_ref, sc_ref, sh_ref,
                        wg_ref, wu_ref, wd_ref, g2_ref, o_ref, act_scr, *, tf):
    x1, h = _mix_out(x_ref, ya_ref, yb_ref, w_ref, g1_ref, gn_ref, sc_ref, sh_ref)
    h = h.astype(BF16)
    d_ff = wg_ref.shape[1]
    for j in range(d_ff // tf):
        cs = slice(j * tf, (j + 1) * tf)
        act_scr[:, cs] = (_silu(_dot(h, wg_ref[:, cs])) * _dot(h, wu_ref[:, cs])).astype(BF16)
    o_ref[...] = x1 + g2_ref[0] * _dot(act_scr[...], wd_ref[...])


def _outproj_ffn(x2, ya, yb, w_out, g1, gn, sc, sh, wg, wu, wd, g2, seq):
    n, d = x2.shape
    d_ff = wg.shape[1]
    tm = min(ROW_TILE, seq)
    per_b = seq // tm
    tf = 2 * LANES
    row = lambda i: (i, 0)
    fix = lambda i: (0, 0)
    once = dict(pipeline_mode=pl.Buffered(1))
    return pl.pallas_call(
        functools.partial(_outproj_ffn_kernel, tf=tf),
        grid=(n // tm,),
        in_specs=_mix_specs(x2, ya, yb, w_out, tm, per_b) + [
            pl.BlockSpec(wg.shape, fix, **once),
            pl.BlockSpec(wu.shape, fix, **once),
            pl.BlockSpec(wd.shape, fix, **once),
            pl.BlockSpec((1, 1, d), lambda i: (i // per_b, 0, 0)),
        ],
        out_specs=pl.BlockSpec((tm, d), row),
        out_shape=jax.ShapeDtypeStruct((n, d), F32),
        scratch_shapes=[pltpu.VMEM((tm, d_ff), BF16)],
        compiler_params=_cparams(("parallel",)),
        name="outproj_ffn",
    )(x2, ya, yb, w_out, g1, gn, sc, sh, wg, wu, wd, g2)


def _route_kernel(lg_ref, dest_ref, wt_ref, meta_ref, cnt_scr, exc_scr, *, tile):
    n_e, n = lg_ref.shape
    lg = lg_ref[...]
    e_iota = lax.broadcasted_iota(I32, (n_e, n), 0)
    m1 = jnp.max(lg, axis=0, keepdims=True)
    e0 = jnp.min(jnp.where(lg == m1, e_iota, n_e), axis=0, keepdims=True)
    lg2 = jnp.where(e_iota == e0, -jnp.inf, lg)
    m2 = jnp.max(lg2, axis=0, keepdims=True)
    e1 = jnp.min(jnp.where(lg2 == m2, e_iota, n_e), axis=0, keepdims=True)
    ex = jnp.exp(m2 - m1)
    wt_ref[0:1, :] = 1.0 / (1.0 + ex)
    wt_ref[1:2, :] = ex / (1.0 + ex)
    oh0 = e_iota == e0
    oh1 = e_iota == e1
    cnt_scr[...] = jnp.where(oh0, 1.0, 0.0) + jnp.where(oh1, 1.0, 0.0)

    ri = lax.broadcasted_iota(I32, (LANES, 2 * LANES), 0)
    ci = lax.broadcasted_iota(I32, (LANES, 2 * LANES), 1)
    prefix_total = jnp.where((ci >= LANES) | (ri < ci), 1.0, 0.0).astype(BF16)

    def block(kb, carry):
        ls = pl.ds(pl.multiple_of(kb * LANES, LANES), LANES)
        both = _dot(cnt_scr[:, ls].astype(BF16), prefix_total)
        exc_scr[:, ls] = both[:, :LANES] + carry
        return carry + both[:, LANES:]

    total = lax.fori_loop(0, n // LANES, block, jnp.zeros((n_e, LANES), F32))
    padded = jnp.floor((total + (tile - 1)) / tile) * tile
    e_col = lax.broadcasted_iota(I32, (n_e, LANES), 0)
    starts = jnp.zeros((n_e, LANES), F32)
    for e in range(n_e - 1):
        starts = starts + jnp.where(e_col > e, padded[e:e + 1, :], 0.0)
    ends = starts + padded
    slot = starts[:, 0:1] + exc_scr[...]
    dest_ref[0:1, :] = jnp.sum(jnp.where(oh0, slot, 0.0), axis=0, keepdims=True).astype(I32)
    dest_ref[1:2, :] = jnp.sum(jnp.where(oh1, slot, 0.0), axis=0, keepdims=True).astype(I32)
    blk_start = (lax.broadcasted_iota(I32, (n_e, LANES), 1) * tile).astype(F32)
    blk_exp = jnp.sum(jnp.where(ends <= blk_start, 1.0, 0.0), axis=0, keepdims=True)
    meta_ref[0:1, :] = jnp.minimum(blk_exp, n_e - 1.0).astype(I32)
    meta_ref[1:2, :] = (ends[n_e - 1:n_e, :] / tile).astype(I32)
    on_diag = lax.broadcasted_iota(I32, (n_e, LANES), 1) == e_col
    meta_ref[2:3, :] = jnp.sum(jnp.where(on_diag, ends / tile, 0.0), axis=0, keepdims=True).astype(I32)
    meta_ref[3:4, :] = jnp.sum(jnp.where(on_diag, starts / tile, 0.0), axis=0, keepdims=True).astype(I32)
    meta_ref[4:SUBLANES, :] = jnp.zeros((SUBLANES - 4, LANES), I32)


def _route(logits_t, tile):
    n_e, n = logits_t.shape
    return pl.pallas_call(
        functools.partial(_route_kernel, tile=tile),
        out_shape=(jax.ShapeDtypeStruct((2, n), I32), jax.ShapeDtypeStruct((2, n), F32),
                   jax.ShapeDtypeStruct((SUBLANES, LANES), I32)),
        scratch_shapes=[pltpu.VMEM((n_e, n), F32), pltpu.VMEM((n_e, n), F32)],
        compiler_params=pltpu.CompilerParams(vmem_limit_bytes=VMEM_LIMIT),
        name="moe_route",
    )(logits_t)


def _dispatch_kernel(dest_ref, meta_ref, h_ref, wg_ref, wu_ref, wd_ref,
                     xs_ref, wgo_ref, wuo_ref, wdo_ref, zero_scr, sem, zsem, *, tile):
    i = pl.program_id(0)
    td = h_ref.shape[0]

    @pl.when(i == 0)
    def _():
        zero_scr[...] = jnp.zeros_like(zero_scr)
        n_blk = xs_ref.shape[0] // tile

        def zero_block(b):
            return pltpu.make_async_copy(zero_scr, xs_ref.at[pl.ds(pl.multiple_of(b * tile, tile), tile)], zsem)

        def each_block(fn):
            for e in range(N_EXPERTS):
                @pl.when(meta_ref[2, e] > meta_ref[3, e])
                def _():
                    fn(zero_block(meta_ref[2, e] - 1))

            def tail(b, _):
                fn(zero_block(b))
                return 0
            lax.fori_loop(meta_ref[1, 0], n_blk, tail, 0)

        each_block(lambda c: c.start())
        each_block(lambda c: c.wait())

    def copy(r, k):
        row = dest_ref[k * (pl.num_programs(0) * td) + i * td + r]
        return pltpu.make_async_copy(h_ref.at[pl.ds(r, 1)], xs_ref.at[pl.ds(row, 1)], sem)

    def start(r, _):
        copy(r, 0).start()
        copy(r, 1).start()
        return 0

    lax.fori_loop(0, td, start, 0, unroll=DMA_UNROLL)
    wgo_ref[...] = wg_ref[...].astype(BF16)
    wuo_ref[...] = wu_ref[...].astype(BF16)
    wdo_ref[...] = wd_ref[...].astype(BF16)
    for _ in range(2):
        pltpu.make_async_copy(h_ref, xs_ref.at[pl.ds(0, td)], sem).wait()


def _dispatch(dest, meta, h, n_rows, tile, weights):
    n, d = h.shape
    td = min(GATHER_TILE, n)
    n_steps = n // td
    flat = [w.reshape(-1, w.shape[-1]) for w in weights]
    slabs = [w.shape[0] // n_steps for w in flat]
    assert all(w.shape[0] == sl * n_steps and sl % (2 * SUBLANES) == 0 for w, sl in zip(flat, slabs))
    w_specs = [pl.BlockSpec((sl, w.shape[1]), lambda i, s, m: (i, 0)) for w, sl in zip(flat, slabs)]
    grid_spec = pltpu.PrefetchScalarGridSpec(
        num_scalar_prefetch=2,
        grid=(n_steps,),
        in_specs=[pl.BlockSpec((td, d), lambda i, s, m: (i, 0))] + w_specs,
        out_specs=[pl.BlockSpec(memory_space=pl.ANY)] + w_specs,
        scratch_shapes=[pltpu.VMEM((tile, d), h.dtype), pltpu.SemaphoreType.DMA(()),
                        pltpu.SemaphoreType.DMA(())],
    )
    outs = pl.pallas_call(
        functools.partial(_dispatch_kernel, tile=tile),
        grid_spec=grid_spec,
        out_shape=[jax.ShapeDtypeStruct((n_rows, d), h.dtype)]
        + [jax.ShapeDtypeStruct(w.shape, BF16) for w in flat],
        compiler_params=_cparams(("arbitrary",)),
        name="moe_dispatch",
    )(dest.reshape(-1), meta, h, *flat)
    return outs[0], [o.reshape(w.shape) for o, w in zip(outs[1:], weights)]


def _expert_kernel(meta_ref, x_ref, wg_ref, wu_ref, wd_ref, y_ref, xb_scr):
    i = pl.program_id(0)
    j = pl.program_id(1)

    @pl.when(i < meta_ref[1, 0])
    def _():
        @pl.when(j == 0)
        def _():
            xb_scr[...] = x_ref[...].astype(BF16)

        xb = xb_scr[...]
        act = (_silu(_dot(xb, wg_ref[0])) * _dot(xb, wu_ref[0])).astype(BF16)
        part = _dot(act, wd_ref[0])

        @pl.when(j == 0)
        def _():
            y_ref[...] = part

        @pl.when(j > 0)
        def _():
            y_ref[...] += part

    @pl.when((i >= meta_ref[1, 0]) & (j == 0))
    def _():
        y_ref[...] = jnp.zeros_like(y_ref)


def _experts(meta, xs, wg, wu, wd, tile):
    n_rows, d = xs.shape
    d_ff = wg.shape[2]
    tf = d_ff // 2 if (d_ff // 2) % (2 * LANES) == 0 else d_ff
    n_blk = n_rows // tile

    def blk(i, s):
        return jnp.minimum(i, s[1, 0] - 1)

    grid_spec = pltpu.PrefetchScalarGridSpec(
        num_scalar_prefetch=1,
        grid=(n_blk, d_ff // tf),
        in_specs=[
            pl.BlockSpec((tile, d), lambda i, j, s: (blk(i, s), 0)),
            pl.BlockSpec((1, d, tf), lambda i, j, s: (s[0, blk(i, s)], 0, jnp.where(i < s[1, 0], j, 0))),
            pl.BlockSpec((1, d, tf), lambda i, j, s: (s[0, blk(i, s)], 0, jnp.where(i < s[1, 0], j, 0))),
            pl.BlockSpec((1, tf, d), lambda i, j, s: (s[0, blk(i, s)], jnp.where(i < s[1, 0], j, 0), 0)),
        ],
        out_specs=pl.BlockSpec((tile, d), lambda i, j, s: (i, 0)),
        scratch_shapes=[pltpu.VMEM((tile, d), BF16)],
    )
    return pl.pallas_call(
        _expert_kernel,
        grid_spec=grid_spec,
        out_shape=jax.ShapeDtypeStruct((n_rows, d), F32),
        compiler_params=_cparams(("arbitrary", "arbitrary")),
        name="moe_experts",
    )(meta, xs, wg, wu, wd)


def _combine_kernel(dest_ref, y_ref, x_ref, wt_ref, g2_ref, gn_ref, o_ref, buf, sem):
    i = pl.program_id(0)
    n_steps = pl.num_programs(0)
    tc = x_ref.shape[0]
    slot = i % 2

    def gather(step, to_slot):
        def start(r, _):
            for k in range(2):
                row = dest_ref[k * (n_steps * tc) + step * tc + r]
                pltpu.make_async_copy(y_ref.at[pl.ds(row, 1)], buf.at[to_slot, k, pl.ds(r, 1)],
                                      sem.at[to_slot]).start()
            return 0
        lax.fori_loop(0, tc, start, 0, unroll=DMA_UNROLL)

    @pl.when(i == 0)
    def _():
        gather(0, 0)

    @pl.when(i + 1 < n_steps)
    def _():
        gather(i + 1, 1 - slot)

    for k in range(2):
        pltpu.make_async_copy(y_ref.at[pl.ds(0, tc)], buf.at[slot, k], sem.at[slot]).wait()
    wt = wt_ref[...]
    y = buf[slot, 0] * wt[:, 0:1] + buf[slot, 1] * wt[:, 1:2]
    x = x_ref[...] + g2_ref[0] * y
    ms = jnp.mean(x * x, axis=-1, keepdims=True)
    o_ref[...] = x * lax.rsqrt(ms + EPS) * gn_ref[...]


def _combine(dest, y, x1, wt, g2, gn, seq):
    n, d = x1.shape
    tc = min(GATHER_TILE, seq)
    per_b = seq // tc
    grid_spec = pltpu.PrefetchScalarGridSpec(
        num_scalar_prefetch=1,
        grid=(n // tc,),
        in_specs=[
            pl.BlockSpec(memory_space=pl.ANY),
            pl.BlockSpec((tc, d), lambda i, s: (i, 0)),
            pl.BlockSpec((tc, 2), lambda i, s: (i, 0)),
            pl.BlockSpec((1, 1, d), lambda i, s: (i // per_b, 0, 0)),
            pl.BlockSpec((1, d), lambda i, s: (0, 0)),
        ],
        out_specs=pl.BlockSpec((tc, d), lambda i, s: (i, 0)),
        scratch_shapes=[pltpu.VMEM((2, 2, tc, d), F32), pltpu.SemaphoreType.DMA((2,))],
    )
    return pl.pallas_call(
        _combine_kernel,
        grid_spec=grid_spec,
        out_shape=jax.ShapeDtypeStruct((n, d), F32),
        compiler_params=_cparams(("arbitrary",)),
        name="moe_combine_norm",
    )(dest.reshape(-1), y, x1, wt, g2, gn)


def _final_norm_kernel(x_ref, g_ref, o_ref):
    x = x_ref[...]
    ms = jnp.mean(x * x, axis=-1, keepdims=True)
    o_ref[...] = x * lax.rsqrt(ms + EPS) * g_ref[...]


def _mixer(x2, mod, l, batch, seq, norm_mix_g, w_in, b_in, cmp_pe, cmp_w1, cmp_w2,
           conv_w, conv_b, mlstm_norm_g):
    d = x2.shape[1]
    sh1, sc1 = mod[l, :, 0:d], mod[l, :, d:2 * d]
    w, b = _inproj_weights(w_in[l], b_in[l])
    q, kvp, sm, y_ml = _inproj(x2, norm_mix_g[l][None, :], sc1[:, None, :], sh1[:, None, :], w, b,
                               conv_w[l], conv_b[l], mlstm_norm_g[l], seq)
    pe, w1p, w2p = _compress_weights(cmp_pe[l], cmp_w1[l], cmp_w2[l])
    kc = _compress(kvp[0:NSA_KV_HEADS], pe, w1p, w2p, batch, seq)
    gates = sm[:, SM_GATE:SM_GATE + 24].reshape(batch, seq, NSA_KV_HEADS, NSA_REP * 3)
    gates = jnp.pad(gates.transpose(0, 2, 3, 1), ((0, 0), (0, 0), (0, 16 - NSA_REP * 3), (0, 0)))
    y_nsa = _nsa2(q, kvp, kc, gates.reshape(batch * NSA_KV_HEADS, 16, seq), batch, seq)
    return y_nsa, y_ml


def kernel(x, c, ada_w, ada_b, norm_mix_g, norm_ffn_g, w_in, b_in, cmp_pe, cmp_w1, cmp_w2, conv_w, conv_b, mlstm_norm_g, w_out, ffn_w_gate, ffn_w_up, ffn_w_down, router_w, moe_w_gate, moe_w_up, moe_w_down, final_norm_g):
    batch, seq, d = x.shape
    depth = ada_w.shape[0]
    n = batch * seq
    mod = _adaln(c, ada_w, ada_b)
    x2 = x.reshape(n, d)
    for l in range(depth):
        g1 = mod[l, :, 2 * d:3 * d][:, None, :]
        sh2 = mod[l, :, 3 * d:4 * d][:, None, :]
        sc2 = mod[l, :, 4 * d:5 * d][:, None, :]
        g2 = mod[l, :, 5 * d:6 * d][:, None, :]
        y_nsa, y_ml = _mixer(x2, mod, l, batch, seq, norm_mix_g, w_in, b_in, cmp_pe, cmp_w1, cmp_w2,
                             conv_w, conv_b, mlstm_norm_g)
        gn = norm_ffn_g[l][None, :]
        i = l // 2
        last = l == depth - 1
        if l % 2 == 0:
            x2 = _outproj_ffn(x2, y_nsa, y_ml, w_out[l].astype(BF16), g1, gn, sc2, sh2,
                              ffn_w_gate[i].astype(BF16), ffn_w_up[i].astype(BF16),
                              ffn_w_down[i].astype(BF16), g2, seq)
            if last:
                x2 = _final_norm(x2, final_norm_g)
        else:
            x1, h, logits_t = _outproj_router(x2, y_nsa, y_ml, w_out[l].astype(BF16), g1, gn, sc2, sh2, seq,
                                              router_w[i].T)
            n_rows = 2 * n + N_EXPERTS * MOE_TILE
            dest, wt, meta = _route(logits_t, MOE_TILE)
            xs, (wg, wu, wd) = _dispatch(dest, meta, h, n_rows, MOE_TILE,
                                         (moe_w_gate[i], moe_w_up[i], moe_w_down[i]))
            y = _experts(meta, xs, wg, wu, wd, MOE_TILE)
            unit = jnp.ones((1, d), F32)
            x2 = _combine(dest, y, x1, wt.T, g2, final_norm_g[None, :] if last else unit, seq)
            if not last:
                raise NotImplementedError("a MoE layer that is not the last layer")
    return x2.reshape(batch, seq, d)


def _final_norm(x2, g):
    n, d = x2.shape
    tm = min(ROW_TILE, n)
    return pl.pallas_call(
        _final_norm_kernel,
        grid=(n // tm,),
        in_specs=[pl.BlockSpec((tm, d), lambda i: (i, 0)), pl.BlockSpec((1, d), lambda i: (0, 0))],
        out_specs=pl.BlockSpec((tm, d), lambda i: (i, 0)),
        out_shape=jax.ShapeDtypeStruct((n, d), F32),
        compiler_params=_cparams(("parallel",)),
        name="final_norm",
    )(x2, g[None, :])
```

```python
import functools

import numpy as np
import jax
import jax.numpy as jnp
from jax import lax
from jax.experimental import pallas as pl
from jax.experimental.pallas import tpu as pltpu

F32 = jnp.float32
BF16 = jnp.bfloat16
I32 = jnp.int32

NSA_HEADS = 8
NSA_KV_HEADS = 2
NSA_REP = NSA_HEADS // NSA_KV_HEADS
NSA_HEAD_DIM = 64
CMP_LEN = 32
CMP_STRIDE = 16
CMP_HIDDEN = 128
SEL_BLOCK = 64
SEL_SHIFT = 6
SEL_TOP = 16
WINDOW = 512
Q_BLOCK = 256
FORCE_BONUS = 1e4
NEG_INF = -1e30
MLSTM_HEADS = 4
MLSTM_HEAD_DIM = 128
CONV_WIDTH = 4
N_EXPERTS = 8
EPS = 1e-6

LANES = 128
SUBLANES = 8
VMEM_LIMIT = 56 * 1024 * 1024

ROW_TILE = 512
SEL_CHUNK = 512
MLSTM_CHUNK = 256
MOE_TILE = 512
GATHER_TILE = 256
DMA_UNROLL = 8
VT_HEAD = 16

SM_GATE = 0
SM_I = 24
SM_F = 28


def _cparams(sem, vmem=VMEM_LIMIT):
    return pltpu.CompilerParams(dimension_semantics=sem, vmem_limit_bytes=vmem)


def _sigmoid(x):
    return 1.0 / (1.0 + jnp.exp(-x))


def _silu(x):
    return x * _sigmoid(x)


def _log_sigmoid(x):
    return jnp.minimum(x, 0.0) - jnp.log1p(jnp.exp(-jnp.abs(x)))


def _dot(a, b):
    return jnp.dot(a, b, preferred_element_type=F32)


def _dot_nt(a, b):
    return lax.dot_general(a, b, (((1,), (1,)), ((), ())), preferred_element_type=F32)


def _split3(x):
    hi = x.astype(BF16)
    r1 = x - hi.astype(F32)
    mid = r1.astype(BF16)
    lo = (r1 - mid.astype(F32)).astype(BF16)
    return hi, mid, lo


def _rmsnorm_mod(x, g, sc, sh):
    ms = jnp.mean(x * x, axis=-1, keepdims=True)
    y = x * lax.rsqrt(ms + EPS) * g
    return y * (1.0 + sc) + sh


def _adaln_kernel(c_ref, w_ref, b_ref, o_ref):
    c = c_ref[...]
    ca = _silu(c).astype(BF16)
    o_ref[0] = _dot(ca, w_ref[0].astype(BF16)) + b_ref[0]


def _adaln(c, ada_w, ada_b):
    depth, d, n6 = ada_w.shape
    b = c.shape[0]
    cp = jnp.zeros((SUBLANES, d), F32).at[:b].set(c)
    tn = n6 // 4
    out = pl.pallas_call(
        _adaln_kernel,
        grid=(depth, n6 // tn),
        in_specs=[
            pl.BlockSpec((SUBLANES, d), lambda l, j: (0, 0)),
            pl.BlockSpec((1, d, tn), lambda l, j: (l, 0, j)),
            pl.BlockSpec((1, 1, tn), lambda l, j: (l, 0, j)),
        ],
        out_specs=pl.BlockSpec((1, SUBLANES, tn), lambda l, j: (l, 0, j)),
        out_shape=jax.ShapeDtypeStruct((depth, SUBLANES, n6), F32),
        compiler_params=_cparams(("parallel", "parallel")),
        name="adaln",
    )(cp, ada_w, ada_b.reshape(depth, 1, n6))
    return out[:, :b]


def _inproj_kernel(x_ref, g_ref, sc_ref, sh_ref, wq_ref, wkv_ref, ws_ref, wm_ref, b_ref, cw_ref, cb_ref, gm_ref,
                   q_ref, kv_ref, sm_ref, y_ref,
                   qm_ref, km_ref, v_ref, o_ref, tail_scr, c_scr, n_scr, m_scr, *, per_b, lc):
    i = pl.program_id(0)

    @pl.when(i == 0)
    def _():
        tail_scr[...] = jnp.zeros_like(tail_scr)

    @pl.when(i % per_b == 0)
    def _():
        c_scr[...] = jnp.zeros_like(c_scr)
        n_scr[...] = jnp.zeros_like(n_scr)
        m_scr[...] = jnp.zeros_like(m_scr)

    h = _rmsnorm_mod(x_ref[...], g_ref[...], sc_ref[0], sh_ref[0]).astype(BF16)

    def sec(lo, width):
        for w_ref, base in ((wq_ref, 0), (wkv_ref, 512), (ws_ref, 1280), (wm_ref, 1408)):
            if base <= lo < base + w_ref.shape[1]:
                return _dot(h, w_ref[:, lo - base:lo - base + width]) + b_ref[:, lo:lo + width]
        raise ValueError(lo)

    strip = 2 * LANES
    row8 = lax.broadcasted_iota(I32, (SUBLANES, 1), 0)
    first = i % per_b == 0
    half = qm_ref.shape[1]
    for c0 in range(0, 2 * half, strip):
        cs = slice(c0, c0 + strip)
        cur = sec(1408 + c0, strip)
        tm = cur.shape[0]
        prev = jnp.where(first, 0.0, tail_scr[:, cs])
        tail_scr[:, cs] = cur[tm - SUBLANES:tm, :]
        y = cb_ref[:, cs]
        for tap in range(CONV_WIDTH):
            back = CONV_WIDTH - 1 - tap
            if back:
                rolled = pltpu.roll(cur, back, axis=0)
                top = jnp.where(row8 < back, pltpu.roll(prev, back, axis=0), rolled[0:SUBLANES])
                shifted = jnp.concatenate([top, rolled[SUBLANES:]], axis=0)
            else:
                shifted = cur
            y = y + shifted * cw_ref[tap:tap + 1, cs]
        y = _silu(y)
        if c0 < half:
            qm_ref[:, cs] = y.astype(BF16)
        else:
            km_ref[:, c0 - half:c0 - half + strip] = (y * (MLSTM_HEAD_DIM ** -0.5)).astype(BF16)

    for c0 in range(0, 512, strip):
        cs = slice(c0, c0 + strip)
        v_ref[:, cs] = sec(2432 + c0, strip).astype(BF16)
        o_ref[:, cs] = sec(2944 + c0, strip)
    sm_ref[...] = sec(512 + 6 * LANES, LANES)

    for c in range(tm // lc):
        _mlstm_chunk(slice(c * lc, (c + 1) * lc), qm_ref, km_ref, v_ref, o_ref, sm_ref, gm_ref, y_ref,
                     c_scr, n_scr, m_scr)

    for c0 in range(0, 512, strip):
        cs = slice(c0, c0 + strip)
        q_ref[:, cs] = (sec(c0, strip) * (NSA_HEAD_DIM ** -0.5)).astype(BF16)
    for s in range(0, 6, 2):
        pair = sec(512 + s * LANES, strip)
        kv_ref[s] = pair[:, :LANES].astype(BF16)
        kv_ref[s + 1] = pair[:, LANES:].astype(BF16)


def _inproj_weights(w_in, b_in):
    o_q, o_kv, o_gate, o_qk, o_v, o_o, o_i, o_f = [int(v) for v in np.cumsum((0, 512, 768, 24, 1024, 512, 512, 4))]

    def groups(a):
        kv = a[..., o_kv:o_gate].reshape(a.shape[:-1] + (3, 2, NSA_KV_HEADS, NSA_HEAD_DIM))
        kv = jnp.swapaxes(kv, -3, -2)
        kv = kv.reshape(a.shape[:-1] + (768,))
        small = jnp.concatenate(
            [a[..., o_gate:o_qk], a[..., o_i:o_f], a[..., o_f:o_f + 4],
             jnp.zeros(a.shape[:-1] + (LANES - 32,), a.dtype)], axis=-1)
        return a[..., o_q:o_kv], kv, small, a[..., o_qk:o_i]

    return tuple(w.astype(BF16) for w in groups(w_in)), jnp.concatenate(groups(b_in))[None, :].astype(F32)


def _inproj(x2, g, sc, sh, w, b, conv_w, conv_b, norm_g, seq):
    n, d = x2.shape
    tm = min(2 * ROW_TILE, seq)
    lc = min(MLSTM_CHUNK, tm)
    per_b = seq // tm
    wm = MLSTM_HEADS * MLSTM_HEAD_DIM
    row = lambda i: (i, 0)
    bat = lambda i: (i // per_b, 0, 0)
    fix = lambda i: (0, 0)
    outs = (
        jax.ShapeDtypeStruct((n, 512), BF16),
        jax.ShapeDtypeStruct((6, n, LANES), BF16),
        jax.ShapeDtypeStruct((n, 128), F32),
        jax.ShapeDtypeStruct((n, wm), BF16),
    )
    return pl.pallas_call(
        functools.partial(_inproj_kernel, per_b=per_b, lc=lc),
        grid=(n // tm,),
        in_specs=[
            pl.BlockSpec((tm, d), row),
            pl.BlockSpec((1, d), fix),
            pl.BlockSpec((1, 1, d), bat),
            pl.BlockSpec((1, 1, d), bat),
            *[pl.BlockSpec(wi.shape, fix) for wi in w],
            pl.BlockSpec(b.shape, fix),
            pl.BlockSpec(conv_w.shape, fix),
            pl.BlockSpec((1, conv_w.shape[1]), fix),
            pl.BlockSpec((1, wm), fix),
        ],
        out_specs=(
            pl.BlockSpec((tm, 512), row),
            pl.BlockSpec((6, tm, LANES), lambda i: (0, i, 0)),
            pl.BlockSpec((tm, 128), row),
            pl.BlockSpec((tm, wm), row),
        ),
        out_shape=outs,
        scratch_shapes=[
            pltpu.VMEM((tm, wm), BF16), pltpu.VMEM((tm, wm), BF16),
            pltpu.VMEM((tm, wm), BF16), pltpu.VMEM((tm, wm), F32),
            pltpu.VMEM((SUBLANES, 2 * wm), F32),
            pltpu.VMEM((MLSTM_HEADS, MLSTM_HEAD_DIM, MLSTM_HEAD_DIM), F32),
            pltpu.VMEM((SUBLANES, MLSTM_HEAD_DIM), F32),
            pltpu.VMEM((SUBLANES, LANES), F32),
        ],
        compiler_params=_cparams(("arbitrary",)),
        name="inproj_mlstm",
    )(x2, g, sc, sh, *w, b, conv_w, conv_b[None, :], norm_g[None, :])


def _compress_kernel(ch_ref, pe_ref, w1_ref, w2_ref, o_ref):
    ch = ch_ref[0].astype(F32)
    a0 = (ch + pe_ref[0:1, :]).astype(BF16)
    a1 = (ch + pe_ref[1:2, :]).astype(BF16)
    half = ch.shape[1]
    h0 = _dot(a0, w1_ref[0:half, :])
    h1 = _dot(a1, w1_ref[half:2 * half, :])
    n_chunk = ch.shape[0]
    hid = h0 + pltpu.roll(h1, n_chunk - 1, axis=0)
    o_ref[0] = _dot(_silu(hid).astype(BF16), w2_ref[...]).astype(BF16)


def _compress_weights(cmp_pe, cmp_w1, cmp_w2):
    dh, hid = NSA_HEAD_DIM, CMP_HIDDEN
    pe = jnp.concatenate([cmp_pe[0], cmp_pe[1]], axis=-1)
    pe = pe.reshape(2, CMP_STRIDE * LANES)
    w1 = cmp_w1.reshape(2, CMP_LEN, dh, hid)
    z = jnp.zeros((CMP_LEN, dh, hid), cmp_w1.dtype)
    wk = jnp.concatenate([w1[0], z], axis=1)
    wv = jnp.concatenate([z, w1[1]], axis=1)
    w1p = jnp.concatenate([wk, wv], axis=2).reshape(CMP_LEN * LANES, 2 * hid)
    z2 = jnp.zeros((hid, dh), cmp_w2.dtype)
    w2p = jnp.concatenate([jnp.concatenate([cmp_w2[0], z2], axis=1),
                           jnp.concatenate([z2, cmp_w2[1]], axis=1)], axis=0)
    return pe.astype(F32), w1p.astype(BF16), w2p.astype(BF16)


def _compress(kv_cmp, pe, w1p, w2p, batch, seq):
    g = kv_cmp.shape[0]
    n_chunk = seq // CMP_STRIDE
    ch = kv_cmp.reshape(g * batch, n_chunk, CMP_STRIDE * LANES)
    return pl.pallas_call(
        _compress_kernel,
        grid=(g * batch,),
        in_specs=[
            pl.BlockSpec((1, n_chunk, CMP_STRIDE * LANES), lambda i: (i, 0, 0)),
            pl.BlockSpec(pe.shape, lambda i: (0, 0)),
            pl.BlockSpec(w1p.shape, lambda i: (0, 0)),
            pl.BlockSpec(w2p.shape, lambda i: (0, 0)),
        ],
        out_specs=pl.BlockSpec((1, n_chunk, LANES), lambda i: (i, 0, 0)),
        out_shape=jax.ShapeDtypeStruct((g * batch, n_chunk, LANES), BF16),
        compiler_params=_cparams(("parallel",)),
        name="nsa_compress",
    )(ch, pe, w1p, w2p)


def _nsa2_kernel(slope_ref, q_ref, kc_ref, ks_ref, kw_ref, gate_ref, ovt_ref, feat_ref, featc_ref,
                 o_ref, kaug_c, vt_c, kaug_s, vt_s, kaug_w, vt_w, todo_scr, *, seq):
    qb = pl.program_id(1)
    n_cmp = seq // CMP_STRIDE
    n_sel = seq // SEL_BLOCK
    n_top = min(SEL_TOP, n_sel)
    n_chunk = seq // SEL_CHUNK
    dh = NSA_HEAD_DIM
    cols = NSA_REP * Q_BLOCK
    group_w = NSA_REP * dh
    blocks_per_chunk = SEL_CHUNK // SEL_BLOCK

    @pl.when(qb == 0)
    def _():
        def build(src_ref, f_ref, kaug, vt, n_rows):
            step = min(SEL_CHUNK, n_rows)
            lane = lax.broadcasted_iota(I32, (step, LANES), 1)
            ones_row = jnp.where(lax.broadcasted_iota(I32, (VT_HEAD, step), 0) == 0, 1.0, 0.0)
            for c0 in range(0, n_rows, step):
                x = src_ref[c0:c0 + step, :].astype(F32)
                kaug[c0:c0 + step, :] = jnp.where(lane < dh, x, f_ref[c0:c0 + step, :].astype(F32)).astype(BF16)
                vt[:, c0:c0 + step] = jnp.concatenate([ones_row, x.T[dh:]], axis=0).astype(BF16)

        for g in range(NSA_KV_HEADS):
            build(kc_ref.at[g, 0], featc_ref, kaug_c.at[g], vt_c.at[g], n_cmp)
            build(ks_ref.at[g], feat_ref, kaug_s.at[g], vt_s.at[g], seq)
            build(kw_ref.at[g], feat_ref, kaug_w.at[g], vt_w.at[g], seq)

    lane = lax.broadcasted_iota(I32, (Q_BLOCK, LANES), 1)
    blk_f = (lane - dh).astype(F32)
    t_lane = qb * Q_BLOCK + lax.broadcasted_iota(I32, (1, Q_BLOCK), 1)
    last = (qb * Q_BLOCK) // SEL_CHUNK

    def tile4(x):
        return jnp.concatenate([x] * NSA_REP, axis=1)

    def sel_scores(g, q_sel, c, size=SEL_CHUNK):
        start = pl.multiple_of(c * SEL_CHUNK, SEL_CHUNK)
        return _dot_nt(kaug_s[g, pl.ds(start, size), :], q_sel)

    def sel_update(g, c, s, carry):
        m, acc = carry
        start = pl.multiple_of(c * SEL_CHUNK, SEL_CHUNK)
        m_new = jnp.maximum(m, jnp.max(s, axis=0, keepdims=True))
        p = jnp.exp(s - m_new).astype(BF16)
        acc = jnp.exp(m - m_new) * acc + _dot(vt_s[g, :, pl.ds(start, s.shape[0])], p)
        return m_new, acc

    def front(g):
        qf = q_ref[:, g * group_w:(g + 1) * group_w].astype(F32)
        parts = []
        for r in range(NSA_REP):
            pair = qf[:, (r // 2) * LANES:(r // 2 + 1) * LANES]
            if r % 2:
                pair = pltpu.roll(pair, dh, axis=1)
            slope = slope_ref[g * NSA_REP + r]
            parts.append(jnp.where(lane < dh, pair, jnp.where(lane == dh, slope, slope * SEL_BLOCK * blk_f)))
        q_all_f = jnp.concatenate(parts, axis=0)
        q_all = q_all_f.astype(BF16)

        s = _dot_nt(kaug_c[g], q_all)
        end_c = lax.broadcasted_iota(I32, (n_cmp, Q_BLOCK), 0) * CMP_STRIDE + (CMP_LEN - 1)
        s = s + tile4(jnp.where(end_c <= t_lane, 0.0, NEG_INF))
        e = jnp.exp(s - jnp.max(s, axis=0, keepdims=True))
        any_valid = tile4((t_lane >= CMP_LEN - 1).astype(F32))
        p_c = e * (any_valid / jnp.sum(e, axis=0, keepdims=True))
        o_c = _dot(vt_c[g], p_c.astype(BF16))

        p4 = p_c[:, 0:Q_BLOCK]
        for r in range(1, NSA_REP):
            p4 = p4 + p_c[:, r * Q_BLOCK:(r + 1) * Q_BLOCK]
        ovt = ovt_ref[...]
        imp = sum(_dot(ovt, piece) for piece in _split3(p4))
        j_col = lax.broadcasted_iota(I32, (n_sel, 1), 0)
        cur = jnp.right_shift(t_lane, SEL_SHIFT)
        forced = (j_col == 0) | (j_col == cur) | (j_col == cur - 1)
        imp = jnp.where(forced, imp + FORCE_BONUS, imp)
        imp = jnp.where(j_col <= cur, imp, -1.0)
        groups = [imp[v * SUBLANES:(v + 1) * SUBLANES, :] for v in range(n_sel // SUBLANES)]
        j_grp = lax.broadcasted_iota(I32, (SUBLANES, Q_BLOCK), 0)
        ranks = [jnp.zeros((SUBLANES, Q_BLOCK), F32) for _ in groups]
        for k in range(n_sel):
            row_k = groups[k // SUBLANES][k % SUBLANES:k % SUBLANES + 1, :]
            for v, grp in enumerate(groups):
                ge = jnp.where(row_k >= grp, 1.0, 0.0)
                gt = jnp.where(row_k > grp, 1.0, 0.0)
                if v * SUBLANES > k:
                    inc = ge
                elif (v + 1) * SUBLANES - 1 < k:
                    inc = gt
                else:
                    inc = jnp.where(j_grp + v * SUBLANES > k, ge, gt)
                ranks[v] = ranks[v] + inc
        sel_t = jnp.where(jnp.concatenate(ranks, axis=0) < n_top, 1.0, 0.0)
        pieces = [jnp.zeros((dh, Q_BLOCK), F32), sel_t]
        if n_sel < dh:
            pieces.append(jnp.zeros((dh - n_sel, Q_BLOCK), F32))
        selmat = jnp.concatenate(pieces, axis=0).T
        drop = jnp.concatenate([jnp.where(lane > dh, selmat, 1.0)] * NSA_REP, axis=0) < 0.5
        q_sel = jnp.where(drop, NEG_INF, q_all_f).astype(BF16)

        n_todo = jnp.int32(0)
        for c in range(n_chunk - 1):
            rows_c = sel_t[c * blocks_per_chunk:(c + 1) * blocks_per_chunk, :]
            wanted = (jnp.max(rows_c) > 0.5) & (c < last)
            todo_scr[g * n_chunk + n_todo] = jnp.int32(c)
            n_todo = n_todo + wanted.astype(I32)

        span = Q_BLOCK + WINDOW
        start = pl.multiple_of(jnp.maximum(qb * Q_BLOCK - WINDOW, 0), Q_BLOCK)
        s = _dot_nt(kaug_w[g, pl.ds(start, span), :], q_all)
        dist = t_lane - (start + lax.broadcasted_iota(I32, (span, Q_BLOCK), 0))
        in_band = pltpu.bitcast(dist, jnp.uint32) < WINDOW
        s = s + tile4(jnp.where(in_band, 0.0, NEG_INF))
        p = jnp.exp(s - jnp.max(s, axis=0, keepdims=True)).astype(BF16)
        acc = _dot(vt_w[g, :, pl.ds(start, span)], p)
        return q_sel, n_todo, o_c, acc / acc[0:1, :]

    def middle(g, q_sel, n_todo):
        def pair(i, carry):
            c0, c1 = todo_scr[g * n_chunk + 2 * i], todo_scr[g * n_chunk + 2 * i + 1]
            s0, s1 = sel_scores(g, q_sel, c0), sel_scores(g, q_sel, c1)
            return sel_update(g, c1, s1, sel_update(g, c0, s0, carry))

        def single(k, carry):
            c = todo_scr[g * n_chunk + k]
            return sel_update(g, c, sel_scores(g, q_sel, c), carry)

        init = (jnp.full((1, cols), NEG_INF, F32), jnp.zeros((VT_HEAD + dh, cols), F32))
        carry = lax.fori_loop(0, n_todo // 2, pair, init)
        return lax.fori_loop(2 * (n_todo // 2), n_todo, single, carry)

    def back(g, q_sel, carry, o_c, o_w, n_keys):
        pos = last * SEL_CHUNK + lax.broadcasted_iota(I32, (n_keys, Q_BLOCK), 0)
        s = sel_scores(g, q_sel, last, n_keys) + tile4(jnp.where(pos <= t_lane, 0.0, NEG_INF))
        _, acc = sel_update(g, last, s, carry)
        o_s = acc / acc[0:1, :]
        gates = _sigmoid(gate_ref[0, g])
        mixed = []
        for r in range(NSA_REP):
            cs = slice(r * Q_BLOCK, (r + 1) * Q_BLOCK)
            mixed.append(gates[3 * r:3 * r + 1, :] * o_c[VT_HEAD:, cs]
                         + gates[3 * r + 1:3 * r + 2, :] * o_s[VT_HEAD:, cs]
                         + gates[3 * r + 2:3 * r + 3, :] * o_w[VT_HEAD:, cs])
        for pr in range(NSA_REP // 2):
            pair_t = jnp.concatenate([mixed[2 * pr], mixed[2 * pr + 1]], axis=0)
            lo = g * group_w + pr * LANES
            o_ref[:, lo:lo + LANES] = pair_t.T.astype(o_ref.dtype)

    fronts = [front(g) for g in range(NSA_KV_HEADS)]
    carries = [middle(g, fronts[g][0], fronts[g][1]) for g in range(NSA_KV_HEADS)]
    per_chunk = SEL_CHUNK // Q_BLOCK
    for v in range(per_chunk):
        @pl.when(qb % per_chunk == v)
        def _():
            for g in range(NSA_KV_HEADS):
                q_sel, _, o_c, o_w = fronts[g]
                back(g, q_sel, carries[g], o_c, o_w, (v + 1) * Q_BLOCK)


def _nsa_constants(seq):
    n_cmp, n_sel = seq // CMP_STRIDE, seq // SEL_BLOCK
    assert n_sel <= NSA_HEAD_DIM, "one feature lane per selection block"
    slopes = 2.0 ** (-8.0 * np.arange(1, NSA_HEADS + 1) / NSA_HEADS)
    far = (slopes[:, None] * SEL_BLOCK * np.arange(n_sel)[None, :]).astype(np.float32)
    assert np.array_equal(far.astype(BF16).astype(np.float32), far), "ALiBi features must be exact in bf16"
    lo_c = np.arange(n_cmp)[:, None] * CMP_STRIDE
    lo_s = np.arange(n_sel)[None, :] * SEL_BLOCK
    ov = np.clip(np.minimum(lo_c + CMP_LEN, lo_s + SEL_BLOCK) - np.maximum(lo_c, lo_s), 0, None) / CMP_LEN
    ov[n_cmp - 1] = 0.0

    def feats(pos):
        f = np.zeros((pos.shape[0], LANES), np.float32)
        f[:, NSA_HEAD_DIM] = pos % SEL_BLOCK
        blk = pos // SEL_BLOCK
        ok = (blk >= 1) & (blk < NSA_HEAD_DIM)
        f[np.nonzero(ok)[0], NSA_HEAD_DIM + blk[ok]] = 1.0
        return f

    feat = feats(np.arange(seq))
    pos_c = np.arange(n_cmp) * CMP_STRIDE + CMP_LEN - 1
    featc = feats(pos_c)
    featc[pos_c >= seq] = 0.0
    return (jnp.asarray(slopes, F32), jnp.asarray(ov.T, BF16), jnp.asarray(feat, BF16), jnp.asarray(featc, BF16))


def _nsa2(q, kvp, kc, gates_t, batch, seq):
    n, width = q.shape
    g = NSA_KV_HEADS
    n_qb = seq // Q_BLOCK
    n_cmp = seq // CMP_STRIDE
    vt_rows = VT_HEAD + NSA_HEAD_DIM
    slopes, ovt, feat, featc = _nsa_constants(seq)
    fix = lambda b, i, s: (0, 0)
    branch = lambda br: (lambda b, i, s: (br, b, 0))
    grid_spec = pltpu.PrefetchScalarGridSpec(
        num_scalar_prefetch=1,
        grid=(batch, n_qb),
        in_specs=[
            pl.BlockSpec((Q_BLOCK, width), lambda b, i, s: (b * n_qb + i, 0)),
            pl.BlockSpec((g, 1, n_cmp, LANES), lambda b, i, s: (0, b, 0, 0)),
            pl.BlockSpec((g, seq, LANES), branch(1)),
            pl.BlockSpec((g, seq, LANES), branch(2)),
            pl.BlockSpec((1, g, 16, Q_BLOCK), lambda b, i, s: (b, 0, 0, i)),
            pl.BlockSpec(ovt.shape, fix),
            pl.BlockSpec(feat.shape, fix),
            pl.BlockSpec(featc.shape, fix),
        ],
        out_specs=pl.BlockSpec((Q_BLOCK, width), lambda b, i, s: (b * n_qb + i, 0)),
        scratch_shapes=[
            pltpu.VMEM((g, n_cmp, LANES), BF16), pltpu.VMEM((g, vt_rows, n_cmp), BF16),
            pltpu.VMEM((g, seq, LANES), BF16), pltpu.VMEM((g, vt_rows, seq), BF16),
            pltpu.VMEM((g, seq, LANES), BF16), pltpu.VMEM((g, vt_rows, seq), BF16),
            pltpu.SMEM((g * (seq // SEL_CHUNK),), I32),
        ],
    )
    return pl.pallas_call(
        functools.partial(_nsa2_kernel, seq=seq),
        grid_spec=grid_spec,
        out_shape=jax.ShapeDtypeStruct((n, width), BF16),
        compiler_params=_cparams(("parallel", "arbitrary")),
        name="nsa_attention",
    )(slopes, q, kc.reshape(g, batch, n_cmp, LANES), kvp, kvp,
      gates_t.reshape(batch, g, 16, seq), ovt, feat, featc)


def _mlstm_chunk(rows, q_ref, k_ref, v_ref, o_ref, sm_ref, g_ref, y_ref, c_scr, n_scr, m_scr):
    lc = rows.stop - rows.start
    d = MLSTM_HEAD_DIM
    sm = sm_ref[rows, :]
    lf = _log_sigmoid(sm)
    ri = lax.broadcasted_iota(I32, (lc, lc), 0)
    ci = lax.broadcasted_iota(I32, (lc, lc), 1)
    tri = jnp.where(ri >= ci, 1.0, 0.0).astype(BF16)
    a_col = sum(_dot(tri, piece) for piece in _split3(lf))
    a_row = a_col.T

    for h in range(MLSTM_HEADS):
        hs = slice(h * d, (h + 1) * d)
        a_j = a_row[SM_F + h:SM_F + h + 1, :]
        gap_s = sm[:, SM_I + h:SM_I + h + 1] - a_col[:, SM_F + h:SM_F + h + 1]
        m_prev = m_scr[h:h + 1, 0:1]
        qh, kh, vh = q_ref[rows, hs], k_ref[rows, hs], v_ref[rows, hs]
        v_t = vh.astype(F32).T.astype(BF16)
        c_prev = c_scr[h]
        n_prev = n_scr[h:h + 1, :]

        log_d = jnp.where(ri <= ci, a_j + gap_s, -jnp.inf)
        m_inter = a_j + m_prev
        m_t = jnp.maximum(m_inter, jnp.max(log_d, axis=0, keepdims=True))
        d_mat = jnp.exp(log_d - m_t)
        inter = jnp.exp(m_inter - m_t)
        s_qk = _dot_nt(kh, qh) * d_mat
        num = inter * _dot_nt(c_prev.astype(BF16), qh) + _dot(v_t, s_qk.astype(BF16))
        n_rows = jnp.broadcast_to(n_prev, (SUBLANES, d)).astype(BF16)
        den = inter * _dot_nt(n_rows, qh)[0:1, :] + jnp.sum(s_qk, axis=0, keepdims=True)
        hh = (num / jnp.maximum(jnp.abs(den), jnp.exp(-m_t))).T

        a_last = a_j[:, lc - 1:lc]
        log_w = a_last + gap_s
        m_new = jnp.maximum(a_last + m_prev, jnp.max(log_w, axis=0, keepdims=True))
        wk = jnp.exp(log_w - m_new) * kh.astype(F32)
        decay = jnp.exp(a_last + m_prev - m_new)
        c_scr[h] = decay * c_prev + _dot(v_t, wk.astype(BF16))
        n_scr[h:h + 1, :] = decay * n_prev + jnp.sum(wk, axis=0, keepdims=True)
        m_scr[h:h + 1, :] = jnp.broadcast_to(m_new, (1, LANES))

        hg = _sigmoid(o_ref[rows, hs]) * hh
        hn = hg * lax.rsqrt(jnp.mean(hg * hg, axis=-1, keepdims=True) + EPS)
        y_ref[rows, hs] = (hn * g_ref[:, hs]).astype(y_ref.dtype)


def _mix_out(x_ref, ya_ref, yb_ref, w_ref, g1_ref, gn_ref, sc_ref, sh_ref):
    half = ya_ref.shape[1]
    y = _dot(ya_ref[...], w_ref[0:half, :]) + _dot(yb_ref[...], w_ref[half:2 * half, :])
    x1 = x_ref[...] + g1_ref[0] * y
    return x1, _rmsnorm_mod(x1, gn_ref[...], sc_ref[0], sh_ref[0])


def _mix_specs(x2, ya, yb, w_out, tm, per_b):
    d = x2.shape[1]
    row = lambda i: (i, 0)
    bat = lambda i: (i // per_b, 0, 0)
    fix = lambda i: (0, 0)
    return [
        pl.BlockSpec((tm, d), row),
        pl.BlockSpec((tm, ya.shape[1]), row),
        pl.BlockSpec((tm, yb.shape[1]), row),
        pl.BlockSpec(w_out.shape, fix),
        pl.BlockSpec((1, 1, d), bat),
        pl.BlockSpec((1, d), fix),
        pl.BlockSpec((1, 1, d), bat),
        pl.BlockSpec((1, 1, d), bat),
    ]


def _outproj_router_kernel(x_ref, ya_ref, yb_ref, w_ref, g1_ref, gn_ref, sc_ref, sh_ref, rw_ref,
                           x1_ref, h_ref, lg_ref):
    x1, h = _mix_out(x_ref, ya_ref, yb_ref, w_ref, g1_ref, gn_ref, sc_ref, sh_ref)
    x1_ref[...] = x1
    h_ref[...] = h
    lg_ref[...] = lax.dot_general(rw_ref[...], h, (((1,), (1,)), ((), ())),
                                  precision=lax.Precision.HIGHEST, preferred_element_type=F32)


def _outproj_router(x2, ya, yb, w_out, g1, gn, sc, sh, seq, router_wt):
    n, d = x2.shape
    tm = min(ROW_TILE, seq)
    per_b = seq // tm
    row = lambda i: (i, 0)
    return pl.pallas_call(
        _outproj_router_kernel,
        grid=(n // tm,),
        in_specs=_mix_specs(x2, ya, yb, w_out, tm, per_b) + [pl.BlockSpec(router_wt.shape, lambda i: (0, 0))],
        out_specs=(pl.BlockSpec((tm, d), row), pl.BlockSpec((tm, d), row),
                   pl.BlockSpec((N_EXPERTS, tm), lambda i: (0, i))),
        out_shape=(jax.ShapeDtypeStruct((n, d), F32), jax.ShapeDtypeStruct((n, d), F32),
                   jax.ShapeDtypeStruct((N_EXPERTS, n), F32)),
        compiler_params=_cparams(("parallel",)),
        name="outproj_router",
    )(x2, ya, yb, w_out, g1, gn, sc, sh, router_wt)


def _outproj_ffn_kernel(x_ref, ya_ref, yb_ref, w_ref, g1_ref, gn_ref, sc_ref, sh_ref,
                        wg_ref, wu_ref, wd_ref, g2_ref, o_ref, act_scr, *, tf):
    x1, h = _mix_out(x_ref, ya_ref, yb_ref, w_ref, g1_ref, gn_ref, sc_ref, sh_ref)
    h = h.astype(BF16)
    d_ff = wg_ref.shape[1]
    for j in range(d_ff // tf):
        cs = slice(j * tf, (j + 1) * tf)
        act_scr[:, cs] = (_silu(_dot(h, wg_ref[:, cs])) * _dot(h, wu_ref[:, cs])).astype(BF16)
    o_ref[...] = x1 + g2_ref[0] * _dot(act_scr[...], wd_ref[...])


def _outproj_ffn(x2, ya, yb, w_out, g1, gn, sc, sh, wg, wu, wd, g2, seq):
    n, d = x2.shape
    d_ff = wg.shape[1]
    tm = min(ROW_TILE, seq)
    per_b = seq // tm
    tf = 2 * LANES
    row = lambda i: (i, 0)
    fix = lambda i: (0, 0)
    once = dict(pipeline_mode=pl.Buffered(1))
    return pl.pallas_call(
        functools.partial(_outproj_ffn_kernel, tf=tf),
        grid=(n // tm,),
        in_specs=_mix_specs(x2, ya, yb, w_out, tm, per_b) + [
            pl.BlockSpec(wg.shape, fix, **once),
            pl.BlockSpec(wu.shape, fix, **once),
            pl.BlockSpec(wd.shape, fix, **once),
            pl.BlockSpec((1, 1, d), lambda i: (i // per_b, 0, 0)),
        ],
        out_specs=pl.BlockSpec((tm, d), row),
        out_shape=jax.ShapeDtypeStruct((n, d), F32),
        scratch_shapes=[pltpu.VMEM((tm, d_ff), BF16)],
        compiler_params=_cparams(("parallel",)),
        name="outproj_ffn",
    )(x2, ya, yb, w_out, g1, gn, sc, sh, wg, wu, wd, g2)


def _route_kernel(lg_ref, dest_ref, wt_ref, meta_ref, cnt_scr, exc_scr, *, tile):
    n_e, n = lg_ref.shape
    lg = lg_ref[...]
    e_iota = lax.broadcasted_iota(I32, (n_e, n), 0)
    m1 = jnp.max(lg, axis=0, keepdims=True)
    e0 = jnp.min(jnp.where(lg == m1, e_iota, n_e), axis=0, keepdims=True)
    lg2 = jnp.where(e_iota == e0, -jnp.inf, lg)
    m2 = jnp.max(lg2, axis=0, keepdims=True)
    e1 = jnp.min(jnp.where(lg2 == m2, e_iota, n_e), axis=0, keepdims=True)
    ex = jnp.exp(m2 - m1)
    wt_ref[0:1, :] = 1.0 / (1.0 + ex)
    wt_ref[1:2, :] = ex / (1.0 + ex)
    oh0 = e_iota == e0
    oh1 = e_iota == e1
    cnt_scr[...] = jnp.where(oh0, 1.0, 0.0) + jnp.where(oh1, 1.0, 0.0)

    ri = lax.broadcasted_iota(I32, (LANES, 2 * LANES), 0)
    ci = lax.broadcasted_iota(I32, (LANES, 2 * LANES), 1)
    prefix_total = jnp.where((ci >= LANES) | (ri < ci), 1.0, 0.0).astype(BF16)

    def block(kb, carry):
        ls = pl.ds(pl.multiple_of(kb * LANES, LANES), LANES)
        both = _dot(cnt_scr[:, ls].astype(BF16), prefix_total)
        exc_scr[:, ls] = both[:, :LANES] + carry
        return carry + both[:, LANES:]

    total = lax.fori_loop(0, n // LANES, block, jnp.zeros((n_e, LANES), F32))
    padded = jnp.floor((total + (tile - 1)) / tile) * tile
    e_col = lax.broadcasted_iota(I32, (n_e, LANES), 0)
    starts = jnp.zeros((n_e, LANES), F32)
    for e in range(n_e - 1):
        starts = starts + jnp.where(e_col > e, padded[e:e + 1, :], 0.0)
    ends = starts + padded
    slot = starts[:, 0:1] + exc_scr[...]
    dest_ref[0:1, :] = jnp.sum(jnp.where(oh0, slot, 0.0), axis=0, keepdims=True).astype(I32)
    dest_ref[1:2, :] = jnp.sum(jnp.where(oh1, slot, 0.0), axis=0, keepdims=True).astype(I32)
    blk_start = (lax.broadcasted_iota(I32, (n_e, LANES), 1) * tile).astype(F32)
    blk_exp = jnp.sum(jnp.where(ends <= blk_start, 1.0, 0.0), axis=0, keepdims=True)
    meta_ref[0:1, :] = jnp.minimum(blk_exp, n_e - 1.0).astype(I32)
    meta_ref[1:2, :] = (ends[n_e - 1:n_e, :] / tile).astype(I32)
    on_diag = lax.broadcasted_iota(I32, (n_e, LANES), 1) == e_col
    meta_ref[2:3, :] = jnp.sum(jnp.where(on_diag, ends / tile, 0.0), axis=0, keepdims=True).astype(I32)
    meta_ref[3:4, :] = jnp.sum(jnp.where(on_diag, starts / tile, 0.0), axis=0, keepdims=True).astype(I32)
    meta_ref[4:SUBLANES, :] = jnp.zeros((SUBLANES - 4, LANES), I32)


def _route(logits_t, tile):
    n_e, n = logits_t.shape
    return pl.pallas_call(
        functools.partial(_route_kernel, tile=tile),
        out_shape=(jax.ShapeDtypeStruct((2, n), I32), jax.ShapeDtypeStruct((2, n), F32),
                   jax.ShapeDtypeStruct((SUBLANES, LANES), I32)),
        scratch_shapes=[pltpu.VMEM((n_e, n), F32), pltpu.VMEM((n_e, n), F32)],
        compiler_params=pltpu.CompilerParams(vmem_limit_bytes=VMEM_LIMIT),
        name="moe_route",
    )(logits_t)


def _dispatch_kernel(dest_ref, meta_ref, h_ref, wg_ref, wu_ref, wd_ref,
                     xs_ref, wgo_ref, wuo_ref, wdo_ref, zero_scr, sem, zsem, *, tile):
    i = pl.program_id(0)
    td = h_ref.shape[0]

    @pl.when(i == 0)
    def _():
        zero_scr[...] = jnp.zeros_like(zero_scr)
        n_blk = xs_ref.shape[0] // tile

        def zero_block(b):
            return pltpu.make_async_copy(zero_scr, xs_ref.at[pl.ds(pl.multiple_of(b * tile, tile), tile)], zsem)

        def each_block(fn):
            for e in range(N_EXPERTS):
                @pl.when(meta_ref[2, e] > meta_ref[3, e])
                def _():
                    fn(zero_block(meta_ref[2, e] - 1))

            def tail(b, _):
                fn(zero_block(b))
                return 0
            lax.fori_loop(meta_ref[1, 0], n_blk, tail, 0)

        each_block(lambda c: c.start())
        each_block(lambda c: c.wait())

    def copy(r, k):
        row = dest_ref[k * (pl.num_programs(0) * td) + i * td + r]
        return pltpu.make_async_copy(h_ref.at[pl.ds(r, 1)], xs_ref.at[pl.ds(row, 1)], sem)

    def start(r, _):
        copy(r, 0).start()
        copy(r, 1).start()
        return 0

    lax.fori_loop(0, td, start, 0, unroll=DMA_UNROLL)
    wgo_ref[...] = wg_ref[...].astype(BF16)
    wuo_ref[...] = wu_ref[...].astype(BF16)
    wdo_ref[...] = wd_ref[...].astype(BF16)
    for _ in range(2):
        pltpu.make_async_copy(h_ref, xs_ref.at[pl.ds(0, td)], sem).wait()


def _dispatch(dest, meta, h, n_rows, tile, weights):
    n, d = h.shape
    td = min(GATHER_TILE, n)
    n_steps = n // td
    flat = [w.reshape(-1, w.shape[-1]) for w in weights]
    slabs = [w.shape[0] // n_steps for w in flat]
    assert all(w.shape[0] == sl * n_steps and sl % (2 * SUBLANES) == 0 for w, sl in zip(flat, slabs))
    w_specs = [pl.BlockSpec((sl, w.shape[1]), lambda i, s, m: (i, 0)) for w, sl in zip(flat, slabs)]
    grid_spec = pltpu.PrefetchScalarGridSpec(
        num_scalar_prefetch=2,
        grid=(n_steps,),
        in_specs=[pl.BlockSpec((td, d), lambda i, s, m: (i, 0))] + w_specs,
        out_specs=[pl.BlockSpec(memory_space=pl.ANY)] + w_specs,
        scratch_shapes=[pltpu.VMEM((tile, d), h.dtype), pltpu.SemaphoreType.DMA(()),
                        pltpu.SemaphoreType.DMA(())],
    )
    outs = pl.pallas_call(
        functools.partial(_dispatch_kernel, tile=tile),
        grid_spec=grid_spec,
        out_shape=[jax.ShapeDtypeStruct((n_rows, d), h.dtype)]
        + [jax.ShapeDtypeStruct(w.shape, BF16) for w in flat],
        compiler_params=_cparams(("arbitrary",)),
        name="moe_dispatch",
    )(dest.reshape(-1), meta, h, *flat)
    return outs[0], [o.reshape(w.shape) for o, w in zip(outs[1:], weights)]


def _expert_kernel(meta_ref, x_ref, wg_ref, wu_ref, wd_ref, y_ref, xb_scr):
    i = pl.program_id(0)
    j = pl.program_id(1)

    @pl.when(i < meta_ref[1, 0])
    def _():
        @pl.when(j == 0)
        def _():
            xb_scr[...] = x_ref[...].astype(BF16)

        xb = xb_scr[...]
        act = (_silu(_dot(xb, wg_ref[0])) * _dot(xb, wu_ref[0])).astype(BF16)
        part = _dot(act, wd_ref[0])

        @pl.when(j == 0)
        def _():
            y_ref[...] = part

        @pl.when(j > 0)
        def _():
            y_ref[...] += part

    @pl.when((i >= meta_ref[1, 0]) & (j == 0))
    def _():
        y_ref[...] = jnp.zeros_like(y_ref)


def _experts(meta, xs, wg, wu, wd, tile):
    n_rows, d = xs.shape
    d_ff = wg.shape[2]
    tf = d_ff // 2 if (d_ff // 2) % (2 * LANES) == 0 else d_ff
    n_blk = n_rows // tile

    def blk(i, s):
        return jnp.minimum(i, s[1, 0] - 1)

    grid_spec = pltpu.PrefetchScalarGridSpec(
        num_scalar_prefetch=1,
        grid=(n_blk, d_ff // tf),
        in_specs=[
            pl.BlockSpec((tile, d), lambda i, j, s: (blk(i, s), 0)),
            pl.BlockSpec((1, d, tf), lambda i, j, s: (s[0, blk(i, s)], 0, jnp.where(i < s[1, 0], j, 0))),
            pl.BlockSpec((1, d, tf), lambda i, j, s: (s[0, blk(i, s)], 0, jnp.where(i < s[1, 0], j, 0))),
            pl.BlockSpec((1, tf, d), lambda i, j, s: (s[0, blk(i, s)], jnp.where(i < s[1, 0], j, 0), 0)),
        ],
        out_specs=pl.BlockSpec((tile, d), lambda i, j, s: (i, 0)),
        scratch_shapes=[pltpu.VMEM((tile, d), BF16)],
    )
    return pl.pallas_call(
        _expert_kernel,
        grid_spec=grid_spec,
        out_shape=jax.ShapeDtypeStruct((n_rows, d), F32),
        compiler_params=_cparams(("arbitrary", "arbitrary")),
        name="moe_experts",
    )(meta, xs, wg, wu, wd)


def _combine_kernel(dest_ref, y_ref, x_ref, wt_ref, g2_ref, gn_ref, o_ref, buf, sem):
    i = pl.program_id(0)
    n_steps = pl.num_programs(0)
    tc = x_ref.shape[0]
    slot = i % 2

    def gather(step, to_slot):
        def start(r, _):
            for k in range(2):
                row = dest_ref[k * (n_steps * tc) + step * tc + r]
                pltpu.make_async_copy(y_ref.at[pl.ds(row, 1)], buf.at[to_slot, k, pl.ds(r, 1)],
                                      sem.at[to_slot]).start()
            return 0
        lax.fori_loop(0, tc, start, 0, unroll=DMA_UNROLL)

    @pl.when(i == 0)
    def _():
        gather(0, 0)

    @pl.when(i + 1 < n_steps)
    def _():
        gather(i + 1, 1 - slot)

    for k in range(2):
        pltpu.make_async_copy(y_ref.at[pl.ds(0, tc)], buf.at[slot, k], sem.at[slot]).wait()
    wt = wt_ref[...]
    y = buf[slot, 0] * wt[:, 0:1] + buf[slot, 1] * wt[:, 1:2]
    x = x_ref[...] + g2_ref[0] * y
    ms = jnp.mean(x * x, axis=-1, keepdims=True)
    o_ref[...] = x * lax.rsqrt(ms + EPS) * gn_ref[...]


def _combine(dest, y, x1, wt, g2, gn, seq):
    n, d = x1.shape
    tc = min(GATHER_TILE, seq)
    per_b = seq // tc
    grid_spec = pltpu.PrefetchScalarGridSpec(
        num_scalar_prefetch=1,
        grid=(n // tc,),
        in_specs=[
            pl.BlockSpec(memory_space=pl.ANY),
            pl.BlockSpec((tc, d), lambda i, s: (i, 0)),
            pl.BlockSpec((tc, 2), lambda i, s: (i, 0)),
            pl.BlockSpec((1, 1, d), lambda i, s: (i // per_b, 0, 0)),
            pl.BlockSpec((1, d), lambda i, s: (0, 0)),
        ],
        out_specs=pl.BlockSpec((tc, d), lambda i, s: (i, 0)),
        scratch_shapes=[pltpu.VMEM((2, 2, tc, d), F32), pltpu.SemaphoreType.DMA((2,))],
    )
    return pl.pallas_call(
        _combine_kernel,
        grid_spec=grid_spec,
        out_shape=jax.ShapeDtypeStruct((n, d), F32),
        compiler_params=_cparams(("arbitrary",)),
        name="moe_combine_norm",
    )(dest.reshape(-1), y, x1, wt, g2, gn)


def _final_norm_kernel(x_ref, g_ref, o_ref):
    x = x_ref[...]
    ms = jnp.mean(x * x, axis=-1, keepdims=True)
    o_ref[...] = x * lax.rsqrt(ms + EPS) * g_ref[...]


def _mixer(x2, mod, l, batch, seq, norm_mix_g, w_in, b_in, cmp_pe, cmp_w1, cmp_w2,
           conv_w, conv_b, mlstm_norm_g):
    d = x2.shape[1]
    sh1, sc1 = mod[l, :, 0:d], mod[l, :, d:2 * d]
    w, b = _inproj_weights(w_in[l], b_in[l])
    q, kvp, sm, y_ml = _inproj(x2, norm_mix_g[l][None, :], sc1[:, None, :], sh1[:, None, :], w, b,
                               conv_w[l], conv_b[l], mlstm_norm_g[l], seq)
    pe, w1p, w2p = _compress_weights(cmp_pe[l], cmp_w1[l], cmp_w2[l])
    kc = _compress(kvp[0:NSA_KV_HEADS], pe, w1p, w2p, batch, seq)
    gates = sm[:, SM_GATE:SM_GATE + 24].reshape(batch, seq, NSA_KV_HEADS, NSA_REP * 3)
    gates = jnp.pad(gates.transpose(0, 2, 3, 1), ((0, 0), (0, 0), (0, 16 - NSA_REP * 3), (0, 0)))
    y_nsa = _nsa2(q, kvp, kc, gates.reshape(batch * NSA_KV_HEADS, 16, seq), batch, seq)
    return y_nsa, y_ml


def kernel(x, c, ada_w, ada_b, norm_mix_g, norm_ffn_g, w_in, b_in, cmp_pe, cmp_w1, cmp_w2, conv_w, conv_b, mlstm_norm_g, w_out, ffn_w_gate, ffn_w_up, ffn_w_down, router_w, moe_w_gate, moe_w_up, moe_w_down, final_norm_g):
    batch, seq, d = x.shape
    depth = ada_w.shape[0]
    n = batch * seq
    mod = _adaln(c, ada_w, ada_b)
    x2 = x.reshape(n, d)
    for l in range(depth):
        g1 = mod[l, :, 2 * d:3 * d][:, None, :]
        sh2 = mod[l, :, 3 * d:4 * d][:, None, :]
        sc2 = mod[l, :, 4 * d:5 * d][:, None, :]
        g2 = mod[l, :, 5 * d:6 * d][:, None, :]
        y_nsa, y_ml = _mixer(x2, mod, l, batch, seq, norm_mix_g, w_in, b_in, cmp_pe, cmp_w1, cmp_w2,
                             conv_w, conv_b, mlstm_norm_g)
        gn = norm_ffn_g[l][None, :]
        i = l // 2
        last = l == depth - 1
        if l % 2 == 0:
            x2 = _outproj_ffn(x2, y_nsa, y_ml, w_out[l].astype(BF16), g1, gn, sc2, sh2,
                              ffn_w_gate[i].astype(BF16), ffn_w_up[i].astype(BF16),
                              ffn_w_down[i].astype(BF16), g2, seq)
            if last:
                x2 = _final_norm(x2, final_norm_g)
        else:
            x1, h, logits_t = _outproj_router(x2, y_nsa, y_ml, w_out[l].astype(BF16), g1, gn, sc2, sh2, seq,
                                              router_w[i].T)
            n_rows = 2 * n + N_EXPERTS * MOE_TILE
            dest, wt, meta = _route(logits_t, MOE_TILE)
            xs, (wg, wu, wd) = _dispatch(dest, meta, h, n_rows, MOE_TILE,
                                         (moe_w_gate[i], moe_w_up[i], moe_w_down[i]))
            y = _experts(meta, xs, wg, wu, wd, MOE_TILE)
            unit = jnp.ones((1, d), F32)
            x2 = _combine(dest, y, x1, wt.T, g2, final_norm_g[None, :] if last else unit, seq)
            if not last:
                raise NotImplementedError("a MoE layer that is not the last layer")
    return x2.reshape(batch, seq, d)


def _final_norm(x2, g):
    n, d = x2.shape
    tm = min(ROW_TILE, n)
    return pl.pallas_call(
        _final_norm_kernel,
        grid=(n // tm,),
        in_specs=[pl.BlockSpec((tm, d), lambda i: (i, 0)), pl.BlockSpec((1, d), lambda i: (0, 0))],
        out_specs=pl.BlockSpec((tm, d), lambda i: (i, 0)),
        out_shape=jax.ShapeDtypeStruct((n, d), F32),
        compiler_params=_cparams(("parallel",)),
        name="final_norm",
    )(x2, g[None, :])
```

```python
import functools

import numpy as np
import jax
import jax.numpy as jnp
from jax import lax
from jax.experimental import pallas as pl
from jax.experimental.pallas import tpu as pltpu

F32 = jnp.float32
BF16 = jnp.bfloat16
I32 = jnp.int32

NSA_HEADS = 8
NSA_KV_HEADS = 2
NSA_REP = NSA_HEADS // NSA_KV_HEADS
NSA_HEAD_DIM = 64
CMP_LEN = 32
CMP_STRIDE = 16
CMP_HIDDEN = 128
SEL_BLOCK = 64
SEL_SHIFT = 6
SEL_TOP = 16
WINDOW = 512
Q_BLOCK = 256
FORCE_BONUS = 1e4
NEG_INF = -1e30
MLSTM_HEADS = 4
MLSTM_HEAD_DIM = 128
CONV_WIDTH = 4
N_EXPERTS = 8
EPS = 1e-6

LANES = 128
SUBLANES = 8
VMEM_LIMIT = 56 * 1024 * 1024

ROW_TILE = 512
SEL_CHUNK = 512
MLSTM_CHUNK = 256
MOE_TILE = 512
GATHER_TILE = 256
DMA_UNROLL = 8
VT_HEAD = 16

SM_GATE = 0
SM_I = 24
SM_F = 28


def _cparams(sem, vmem=VMEM_LIMIT):
    return pltpu.CompilerParams(dimension_semantics=sem, vmem_limit_bytes=vmem)


def _sigmoid(x):
    return 1.0 / (1.0 + jnp.exp(-x))


def _silu(x):
    return x * _sigmoid(x)


def _log_sigmoid(x):
    return jnp.minimum(x, 0.0) - jnp.log1p(jnp.exp(-jnp.abs(x)))


def _dot(a, b):
    return jnp.dot(a, b, preferred_element_type=F32)


def _dot_nt(a, b):
    return lax.dot_general(a, b, (((1,), (1,)), ((), ())), preferred_element_type=F32)


def _split3(x):
    hi = x.astype(BF16)
    r1 = x - hi.astype(F32)
    mid = r1.astype(BF16)
    lo = (r1 - mid.astype(F32)).astype(BF16)
    return hi, mid, lo


def _rmsnorm_mod(x, g, sc, sh):
    ms = jnp.mean(x * x, axis=-1, keepdims=True)
    y = x * lax.rsqrt(ms + EPS) * g
    return y * (1.0 + sc) + sh


def _adaln_kernel(c_ref, w_ref, b_ref, o_ref):
    c = c_ref[...]
    ca = _silu(c).astype(BF16)
    o_ref[0] = _dot(ca, w_ref[0].astype(BF16)) + b_ref[0]


def _adaln(c, ada_w, ada_b):
    depth, d, n6 = ada_w.shape
    b = c.shape[0]
    cp = jnp.zeros((SUBLANES, d), F32).at[:b].set(c)
    tn = n6 // 4
    out = pl.pallas_call(
        _adaln_kernel,
        grid=(depth, n6 // tn),
        in_specs=[
            pl.BlockSpec((SUBLANES, d), lambda l, j: (0, 0)),
            pl.BlockSpec((1, d, tn), lambda l, j: (l, 0, j)),
            pl.BlockSpec((1, 1, tn), lambda l, j: (l, 0, j)),
        ],
        out_specs=pl.BlockSpec((1, SUBLANES, tn), lambda l, j: (l, 0, j)),
        out_shape=jax.ShapeDtypeStruct((depth, SUBLANES, n6), F32),
        compiler_params=_cparams(("parallel", "parallel")),
        name="adaln",
    )(cp, ada_w, ada_b.reshape(depth, 1, n6))
    return out[:, :b]


def _inproj_kernel(x_ref, g_ref, sc_ref, sh_ref, wq_ref, wkv_ref, ws_ref, wm_ref, b_ref, cw_ref, cb_ref, gm_ref,
                   q_ref, kv_ref, sm_ref, y_ref,
                   qm_ref, km_ref, v_ref, o_ref, tail_scr, c_scr, n_scr, m_scr, *, per_b, lc):
    i = pl.program_id(0)

    @pl.when(i == 0)
    def _():
        tail_scr[...] = jnp.zeros_like(tail_scr)

    @pl.when(i % per_b == 0)
    def _():
        c_scr[...] = jnp.zeros_like(c_scr)
        n_scr[...] = jnp.zeros_like(n_scr)
        m_scr[...] = jnp.zeros_like(m_scr)

    h = _rmsnorm_mod(x_ref[...], g_ref[...], sc_ref[0], sh_ref[0]).astype(BF16)

    def sec(lo, width):
        for w_ref, base in ((wq_ref, 0), (wkv_ref, 512), (ws_ref, 1280), (wm_ref, 1408)):
            if base <= lo < base + w_ref.shape[1]:
                return _dot(h, w_ref[:, lo - base:lo - base + width]) + b_ref[:, lo:lo + width]
        raise ValueError(lo)

    strip = 2 * LANES
    row8 = lax.broadcasted_iota(I32, (SUBLANES, 1), 0)
    first = i % per_b == 0
    half = qm_ref.shape[1]
    for c0 in range(0, 2 * half, strip):
        cs = slice(c0, c0 + strip)
        cur = sec(1408 + c0, strip)
        tm = cur.shape[0]
        prev = jnp.where(first, 0.0, tail_scr[:, cs])
        tail_scr[:, cs] = cur[tm - SUBLANES:tm, :]
        y = cb_ref[:, cs]
        for tap in range(CONV_WIDTH):
            back = CONV_WIDTH - 1 - tap
            if back:
                rolled = pltpu.roll(cur, back, axis=0)
                top = jnp.where(row8 < back, pltpu.roll(prev, back, axis=0), rolled[0:SUBLANES])
                shifted = jnp.concatenate([top, rolled[SUBLANES:]], axis=0)
            else:
                shifted = cur
            y = y + shifted * cw_ref[tap:tap + 1, cs]
        y = _silu(y)
        if c0 < half:
            qm_ref[:, cs] = y.astype(BF16)
        else:
            km_ref[:, c0 - half:c0 - half + strip] = (y * (MLSTM_HEAD_DIM ** -0.5)).astype(BF16)

    for c0 in range(0, 512, strip):
        cs = slice(c0, c0 + strip)
        v_ref[:, cs] = sec(2432 + c0, strip).astype(BF16)
        o_ref[:, cs] = sec(2944 + c0, strip)
    sm_ref[...] = sec(512 + 6 * LANES, LANES)

    def q_strip(c0):
        q_ref[:, c0:c0 + strip] = (sec(c0, strip) * (NSA_HEAD_DIM ** -0.5)).astype(BF16)

    def kv_strip(s):
        pair = sec(512 + s * LANES, strip)
        kv_ref[s] = pair[:, :LANES].astype(BF16)
        kv_ref[s + 1] = pair[:, LANES:].astype(BF16)

    strips = [functools.partial(q_strip, c0) for c0 in range(0, 512, strip)]
    strips += [functools.partial(kv_strip, s) for s in range(0, 6, 2)]
    n_chunks = tm // lc
    for c in range(n_chunks):
        _mlstm_chunk(slice(c * lc, (c + 1) * lc), qm_ref, km_ref, v_ref, o_ref, sm_ref, gm_ref, y_ref,
                     c_scr, n_scr, m_scr)
        for emit in strips[c * len(strips) // n_chunks:(c + 1) * len(strips) // n_chunks]:
            emit()


def _inproj_weights(w_in, b_in):
    o_q, o_kv, o_gate, o_qk, o_v, o_o, o_i, o_f = [int(v) for v in np.cumsum((0, 512, 768, 24, 1024, 512, 512, 4))]

    def groups(a):
        kv = a[..., o_kv:o_gate].reshape(a.shape[:-1] + (3, 2, NSA_KV_HEADS, NSA_HEAD_DIM))
        kv = jnp.swapaxes(kv, -3, -2)
        kv = kv.reshape(a.shape[:-1] + (768,))
        small = jnp.concatenate(
            [a[..., o_gate:o_qk], a[..., o_i:o_f], a[..., o_f:o_f + 4],
             jnp.zeros(a.shape[:-1] + (LANES - 32,), a.dtype)], axis=-1)
        return a[..., o_q:o_kv], kv, small, a[..., o_qk:o_i]

    return tuple(w.astype(BF16) for w in groups(w_in)), jnp.concatenate(groups(b_in))[None, :].astype(F32)


def _inproj(x2, g, sc, sh, w, b, conv_w, conv_b, norm_g, seq):
    n, d = x2.shape
    tm = min(2 * ROW_TILE, seq)
    lc = min(MLSTM_CHUNK, tm)
    per_b = seq // tm
    wm = MLSTM_HEADS * MLSTM_HEAD_DIM
    row = lambda i: (i, 0)
    bat = lambda i: (i // per_b, 0, 0)
    fix = lambda i: (0, 0)
    outs = (
        jax.ShapeDtypeStruct((n, 512), BF16),
        jax.ShapeDtypeStruct((6, n, LANES), BF16),
        jax.ShapeDtypeStruct((n, 128), F32),
        jax.ShapeDtypeStruct((n, wm), BF16),
    )
    return pl.pallas_call(
        functools.partial(_inproj_kernel, per_b=per_b, lc=lc),
        grid=(n // tm,),
        in_specs=[
            pl.BlockSpec((tm, d), row),
            pl.BlockSpec((1, d), fix),
            pl.BlockSpec((1, 1, d), bat),
            pl.BlockSpec((1, 1, d), bat),
            *[pl.BlockSpec(wi.shape, fix) for wi in w],
            pl.BlockSpec(b.shape, fix),
            pl.BlockSpec(conv_w.shape, fix),
            pl.BlockSpec((1, conv_w.shape[1]), fix),
            pl.BlockSpec((1, wm), fix),
        ],
        out_specs=(
            pl.BlockSpec((tm, 512), row),
            pl.BlockSpec((6, tm, LANES), lambda i: (0, i, 0)),
            pl.BlockSpec((tm, 128), row),
            pl.BlockSpec((tm, wm), row),
        ),
        out_shape=outs,
        scratch_shapes=[
            pltpu.VMEM((tm, wm), BF16), pltpu.VMEM((tm, wm), BF16),
            pltpu.VMEM((tm, wm), BF16), pltpu.VMEM((tm, wm), F32),
            pltpu.VMEM((SUBLANES, 2 * wm), F32),
            pltpu.VMEM((MLSTM_HEADS, MLSTM_HEAD_DIM, MLSTM_HEAD_DIM), F32),
            pltpu.VMEM((SUBLANES, MLSTM_HEAD_DIM), F32),
            pltpu.VMEM((SUBLANES, LANES), F32),
        ],
        compiler_params=_cparams(("arbitrary",)),
        name="inproj_mlstm",
    )(x2, g, sc, sh, *w, b, conv_w, conv_b[None, :], norm_g[None, :])


def _compress_kernel(ch_ref, pe_ref, w1_ref, w2_ref, o_ref):
    ch = ch_ref[0].astype(F32)
    a0 = (ch + pe_ref[0:1, :]).astype(BF16)
    a1 = (ch + pe_ref[1:2, :]).astype(BF16)
    half = ch.shape[1]
    h0 = _dot(a0, w1_ref[0:half, :])
    h1 = _dot(a1, w1_ref[half:2 * half, :])
    n_chunk = ch.shape[0]
    hid = h0 + pltpu.roll(h1, n_chunk - 1, axis=0)
    o_ref[0] = _dot(_silu(hid).astype(BF16), w2_ref[...]).astype(BF16)


def _compress_weights(cmp_pe, cmp_w1, cmp_w2):
    dh, hid = NSA_HEAD_DIM, CMP_HIDDEN
    pe = jnp.concatenate([cmp_pe[0], cmp_pe[1]], axis=-1)
    pe = pe.reshape(2, CMP_STRIDE * LANES)
    w1 = cmp_w1.reshape(2, CMP_LEN, dh, hid)
    z = jnp.zeros((CMP_LEN, dh, hid), cmp_w1.dtype)
    wk = jnp.concatenate([w1[0], z], axis=1)
    wv = jnp.concatenate([z, w1[1]], axis=1)
    w1p = jnp.concatenate([wk, wv], axis=2).reshape(CMP_LEN * LANES, 2 * hid)
    z2 = jnp.zeros((hid, dh), cmp_w2.dtype)
    w2p = jnp.concatenate([jnp.concatenate([cmp_w2[0], z2], axis=1),
                           jnp.concatenate([z2, cmp_w2[1]], axis=1)], axis=0)
    return pe.astype(F32), w1p.astype(BF16), w2p.astype(BF16)


def _compress(kv_cmp, pe, w1p, w2p, batch, seq):
    g = kv_cmp.shape[0]
    n_chunk = seq // CMP_STRIDE
    ch = kv_cmp.reshape(g * batch, n_chunk, CMP_STRIDE * LANES)
    return pl.pallas_call(
        _compress_kernel,
        grid=(g * batch,),
        in_specs=[
            pl.BlockSpec((1, n_chunk, CMP_STRIDE * LANES), lambda i: (i, 0, 0)),
            pl.BlockSpec(pe.shape, lambda i: (0, 0)),
            pl.BlockSpec(w1p.shape, lambda i: (0, 0)),
            pl.BlockSpec(w2p.shape, lambda i: (0, 0)),
        ],
        out_specs=pl.BlockSpec((1, n_chunk, LANES), lambda i: (i, 0, 0)),
        out_shape=jax.ShapeDtypeStruct((g * batch, n_chunk, LANES), BF16),
        compiler_params=_cparams(("parallel",)),
        name="nsa_compress",
    )(ch, pe, w1p, w2p)


def _nsa2_kernel(slope_ref, q_ref, kc_ref, ks_ref, kw_ref, gate_ref, ovt_ref, feat_ref, featc_ref,
                 o_ref, kaug_c, vt_c, kaug_s, vt_s, kaug_w, vt_w, todo_scr, *, seq):
    qb = pl.program_id(1)
    n_cmp = seq // CMP_STRIDE
    n_sel = seq // SEL_BLOCK
    n_top = min(SEL_TOP, n_sel)
    n_chunk = seq // SEL_CHUNK
    dh = NSA_HEAD_DIM
    cols = NSA_REP * Q_BLOCK
    group_w = NSA_REP * dh
    blocks_per_chunk = SEL_CHUNK // SEL_BLOCK

    @pl.when(qb == 0)
    def _():
        def build(src_ref, f_ref, kaug, vt, n_rows):
            step = min(SEL_CHUNK, n_rows)
            lane = lax.broadcasted_iota(I32, (step, LANES), 1)
            ones_row = jnp.where(lax.broadcasted_iota(I32, (VT_HEAD, step), 0) == 0, 1.0, 0.0)
            for c0 in range(0, n_rows, step):
                x = src_ref[c0:c0 + step, :].astype(F32)
                kaug[c0:c0 + step, :] = jnp.where(lane < dh, x, f_ref[c0:c0 + step, :].astype(F32)).astype(BF16)
                vt[:, c0:c0 + step] = jnp.concatenate([ones_row, x.T[dh:]], axis=0).astype(BF16)

        for g in range(NSA_KV_HEADS):
            build(kc_ref.at[g, 0], featc_ref, kaug_c.at[g], vt_c.at[g], n_cmp)
            build(ks_ref.at[g], feat_ref, kaug_s.at[g], vt_s.at[g], seq)
            build(kw_ref.at[g], feat_ref, kaug_w.at[g], vt_w.at[g], seq)

    lane = lax.broadcasted_iota(I32, (Q_BLOCK, LANES), 1)
    blk_f = (lane - dh).astype(F32)
    t_lane = qb * Q_BLOCK + lax.broadcasted_iota(I32, (1, Q_BLOCK), 1)
    last = (qb * Q_BLOCK) // SEL_CHUNK

    def tile4(x):
        return jnp.concatenate([x] * NSA_REP, axis=1)

    def sel_scores(g, q_sel, c, size=SEL_CHUNK):
        start = pl.multiple_of(c * SEL_CHUNK, SEL_CHUNK)
        return _dot_nt(kaug_s[g, pl.ds(start, size), :], q_sel)

    def sel_update(g, c, s, carry):
        m, acc = carry
        start = pl.multiple_of(c * SEL_CHUNK, SEL_CHUNK)
        m_new = jnp.maximum(m, jnp.max(s, axis=0, keepdims=True))
        p = jnp.exp(s - m_new).astype(BF16)
        acc = jnp.exp(m - m_new) * acc + _dot(vt_s[g, :, pl.ds(start, s.shape[0])], p)
        return m_new, acc

    def front_scores(g):
        qf = q_ref[:, g * group_w:(g + 1) * group_w].astype(F32)
        parts = []
        for r in range(NSA_REP):
            pair = qf[:, (r // 2) * LANES:(r // 2 + 1) * LANES]
            if r % 2:
                pair = pltpu.roll(pair, dh, axis=1)
            slope = slope_ref[g * NSA_REP + r]
            parts.append(jnp.where(lane < dh, pair, jnp.where(lane == dh, slope, slope * SEL_BLOCK * blk_f)))
        q_all_f = jnp.concatenate(parts, axis=0)
        q_all = q_all_f.astype(BF16)

        s = _dot_nt(kaug_c[g], q_all)
        end_c = lax.broadcasted_iota(I32, (n_cmp, Q_BLOCK), 0) * CMP_STRIDE + (CMP_LEN - 1)
        return q_all_f, q_all, s + tile4(jnp.where(end_c <= t_lane, 0.0, NEG_INF))

    def front(g, q_all_f, q_all, s):
        e = jnp.exp(s - jnp.max(s, axis=0, keepdims=True))
        any_valid = tile4((t_lane >= CMP_LEN - 1).astype(F32))
        p_c = e * (any_valid / jnp.sum(e, axis=0, keepdims=True))
        o_c = _dot(vt_c[g], p_c.astype(BF16))

        p4 = p_c[:, 0:Q_BLOCK]
        for r in range(1, NSA_REP):
            p4 = p4 + p_c[:, r * Q_BLOCK:(r + 1) * Q_BLOCK]
        ovt = ovt_ref[...]
        imp = sum(_dot(ovt, piece) for piece in _split3(p4))
        j_col = lax.broadcasted_iota(I32, (n_sel, 1), 0)
        cur = jnp.right_shift(t_lane, SEL_SHIFT)
        forced = (j_col == 0) | (j_col == cur) | (j_col == cur - 1)
        imp = jnp.where(forced, imp + FORCE_BONUS, imp)
        imp = jnp.where(j_col <= cur, imp, -1.0)
        groups = [imp[v * SUBLANES:(v + 1) * SUBLANES, :] for v in range(n_sel // SUBLANES)]
        j_grp = lax.broadcasted_iota(I32, (SUBLANES, Q_BLOCK), 0)
        ranks = [jnp.zeros((SUBLANES, Q_BLOCK), F32) for _ in groups]
        for k in range(n_sel):
            row_k = groups[k // SUBLANES][k % SUBLANES:k % SUBLANES + 1, :]
            for v, grp in enumerate(groups):
                ge = jnp.where(row_k >= grp, 1.0, 0.0)
                gt = jnp.where(row_k > grp, 1.0, 0.0)
                if v * SUBLANES > k:
                    inc = ge
                elif (v + 1) * SUBLANES - 1 < k:
                    inc = gt
                else:
                    inc = jnp.where(j_grp + v * SUBLANES > k, ge, gt)
                ranks[v] = ranks[v] + inc
        sel_t = jnp.where(jnp.concatenate(ranks, axis=0) < n_top, 1.0, 0.0)
        pieces = [jnp.zeros((dh, Q_BLOCK), F32), sel_t]
        if n_sel < dh:
            pieces.append(jnp.zeros((dh - n_sel, Q_BLOCK), F32))
        selmat = jnp.concatenate(pieces, axis=0).T
        drop = jnp.concatenate([jnp.where(lane > dh, selmat, 1.0)] * NSA_REP, axis=0) < 0.5
        q_sel = jnp.where(drop, NEG_INF, q_all_f).astype(BF16)

        n_todo = jnp.int32(0)
        for c in range(n_chunk - 1):
            rows_c = sel_t[c * blocks_per_chunk:(c + 1) * blocks_per_chunk, :]
            wanted = (jnp.max(rows_c) > 0.5) & (c < last)
            todo_scr[g * n_chunk + n_todo] = jnp.int32(c)
            n_todo = n_todo + wanted.astype(I32)

        span = Q_BLOCK + WINDOW
        start = pl.multiple_of(jnp.maximum(qb * Q_BLOCK - WINDOW, 0), Q_BLOCK)
        s = _dot_nt(kaug_w[g, pl.ds(start, span), :], q_all)
        dist = t_lane - (start + lax.broadcasted_iota(I32, (span, Q_BLOCK), 0))
        in_band = pltpu.bitcast(dist, jnp.uint32) < WINDOW
        s = s + tile4(jnp.where(in_band, 0.0, NEG_INF))
        p = jnp.exp(s - jnp.max(s, axis=0, keepdims=True)).astype(BF16)
        acc = _dot(vt_w[g, :, pl.ds(start, span)], p)
        return q_sel, n_todo, o_c, acc / acc[0:1, :]

    def middle(g, q_sel, n_todo):
        def pair(i, carry):
            c0, c1 = todo_scr[g * n_chunk + 2 * i], todo_scr[g * n_chunk + 2 * i + 1]
            s0, s1 = sel_scores(g, q_sel, c0), sel_scores(g, q_sel, c1)
            return sel_update(g, c1, s1, sel_update(g, c0, s0, carry))

        def single(k, carry):
            c = todo_scr[g * n_chunk + k]
            return sel_update(g, c, sel_scores(g, q_sel, c), carry)

        init = (jnp.full((1, cols), NEG_INF, F32), jnp.zeros((VT_HEAD + dh, cols), F32))
        carry = lax.fori_loop(0, n_todo // 2, pair, init)
        return lax.fori_loop(2 * (n_todo // 2), n_todo, single, carry)

    def back_scores(g, q_sel, n_keys):
        pos = last * SEL_CHUNK + lax.broadcasted_iota(I32, (n_keys, Q_BLOCK), 0)
        return sel_scores(g, q_sel, last, n_keys) + tile4(jnp.where(pos <= t_lane, 0.0, NEG_INF))

    def back(g, s, carry, o_c, o_w):
        _, acc = sel_update(g, last, s, carry)
        o_s = acc / acc[0:1, :]
        gates = _sigmoid(gate_ref[0, g])
        mixed = []
        for r in range(NSA_REP):
            cs = slice(r * Q_BLOCK, (r + 1) * Q_BLOCK)
            mixed.append(gates[3 * r:3 * r + 1, :] * o_c[VT_HEAD:, cs]
                         + gates[3 * r + 1:3 * r + 2, :] * o_s[VT_HEAD:, cs]
                         + gates[3 * r + 2:3 * r + 3, :] * o_w[VT_HEAD:, cs])
        for pr in range(NSA_REP // 2):
            pair_t = jnp.concatenate([mixed[2 * pr], mixed[2 * pr + 1]], axis=0)
            lo = g * group_w + pr * LANES
            o_ref[:, lo:lo + LANES] = pair_t.T.astype(o_ref.dtype)

    starts = [front_scores(g) for g in range(NSA_KV_HEADS)]
    fronts = [front(g, *starts[g]) for g in range(NSA_KV_HEADS)]
    carries = [middle(g, fronts[g][0], fronts[g][1]) for g in range(NSA_KV_HEADS)]
    per_chunk = SEL_CHUNK // Q_BLOCK
    for v in range(per_chunk):
        @pl.when(qb % per_chunk == v)
        def _():
            scores = [back_scores(g, fronts[g][0], (v + 1) * Q_BLOCK) for g in range(NSA_KV_HEADS)]
            for g in range(NSA_KV_HEADS):
                _, _, o_c, o_w = fronts[g]
                back(g, scores[g], carries[g], o_c, o_w)


def _nsa_constants(seq):
    n_cmp, n_sel = seq // CMP_STRIDE, seq // SEL_BLOCK
    assert n_sel <= NSA_HEAD_DIM, "one feature lane per selection block"
    slopes = 2.0 ** (-8.0 * np.arange(1, NSA_HEADS + 1) / NSA_HEADS)
    far = (slopes[:, None] * SEL_BLOCK * np.arange(n_sel)[None, :]).astype(np.float32)
    assert np.array_equal(far.astype(BF16).astype(np.float32), far), "ALiBi features must be exact in bf16"
    lo_c = np.arange(n_cmp)[:, None] * CMP_STRIDE
    lo_s = np.arange(n_sel)[None, :] * SEL_BLOCK
    ov = np.clip(np.minimum(lo_c + CMP_LEN, lo_s + SEL_BLOCK) - np.maximum(lo_c, lo_s), 0, None) / CMP_LEN
    ov[n_cmp - 1] = 0.0

    def feats(pos):
        f = np.zeros((pos.shape[0], LANES), np.float32)
        f[:, NSA_HEAD_DIM] = pos % SEL_BLOCK
        blk = pos // SEL_BLOCK
        ok = (blk >= 1) & (blk < NSA_HEAD_DIM)
        f[np.nonzero(ok)[0], NSA_HEAD_DIM + blk[ok]] = 1.0
        return f

    feat = feats(np.arange(seq))
    pos_c = np.arange(n_cmp) * CMP_STRIDE + CMP_LEN - 1
    featc = feats(pos_c)
    featc[pos_c >= seq] = 0.0
    return (jnp.asarray(slopes, F32), jnp.asarray(ov.T, BF16), jnp.asarray(feat, BF16), jnp.asarray(featc, BF16))


def _nsa2(q, kvp, kc, gates_t, batch, seq):
    n, width = q.shape
    g = NSA_KV_HEADS
    n_qb = seq // Q_BLOCK
    n_cmp = seq // CMP_STRIDE
    vt_rows = VT_HEAD + NSA_HEAD_DIM
    slopes, ovt, feat, featc = _nsa_constants(seq)
    fix = lambda b, i, s: (0, 0)
    branch = lambda br: (lambda b, i, s: (br, b, 0))
    grid_spec = pltpu.PrefetchScalarGridSpec(
        num_scalar_prefetch=1,
        grid=(batch, n_qb),
        in_specs=[
            pl.BlockSpec((Q_BLOCK, width), lambda b, i, s: (b * n_qb + i, 0)),
            pl.BlockSpec((g, 1, n_cmp, LANES), lambda b, i, s: (0, b, 0, 0)),
            pl.BlockSpec((g, seq, LANES), branch(1)),
            pl.BlockSpec((g, seq, LANES), branch(2)),
            pl.BlockSpec((1, g, 16, Q_BLOCK), lambda b, i, s: (b, 0, 0, i)),
            pl.BlockSpec(ovt.shape, fix),
            pl.BlockSpec(feat.shape, fix),
            pl.BlockSpec(featc.shape, fix),
        ],
        out_specs=pl.BlockSpec((Q_BLOCK, width), lambda b, i, s: (b * n_qb + i, 0)),
        scratch_shapes=[
            pltpu.VMEM((g, n_cmp, LANES), BF16), pltpu.VMEM((g, vt_rows, n_cmp), BF16),
            pltpu.VMEM((g, seq, LANES), BF16), pltpu.VMEM((g, vt_rows, seq), BF16),
            pltpu.VMEM((g, seq, LANES), BF16), pltpu.VMEM((g, vt_rows, seq), BF16),
            pltpu.SMEM((g * (seq // SEL_CHUNK),), I32),
        ],
    )
    return pl.pallas_call(
        functools.partial(_nsa2_kernel, seq=seq),
        grid_spec=grid_spec,
        out_shape=jax.ShapeDtypeStruct((n, width), BF16),
        compiler_params=_cparams(("parallel", "arbitrary")),
        name="nsa_attention",
    )(slopes, q, kc.reshape(g, batch, n_cmp, LANES), kvp, kvp,
      gates_t.reshape(batch, g, 16, seq), ovt, feat, featc)


def _mlstm_chunk(rows, q_ref, k_ref, v_ref, o_ref, sm_ref, g_ref, y_ref, c_scr, n_scr, m_scr):
    lc = rows.stop - rows.start
    d = MLSTM_HEAD_DIM
    sm = sm_ref[rows, :]
    lf = _log_sigmoid(sm)
    ri = lax.broadcasted_iota(I32, (lc, lc), 0)
    ci = lax.broadcasted_iota(I32, (lc, lc), 1)
    tri = jnp.where(ri >= ci, 1.0, 0.0).astype(BF16)
    a_col = sum(_dot(tri, piece) for piece in _split3(lf))
    a_row = a_col.T

    for h in range(MLSTM_HEADS):
        hs = slice(h * d, (h + 1) * d)
        a_j = a_row[SM_F + h:SM_F + h + 1, :]
        gap_s = sm[:, SM_I + h:SM_I + h + 1] - a_col[:, SM_F + h:SM_F + h + 1]
        m_prev = m_scr[h:h + 1, 0:1]
        qh, kh, vh = q_ref[rows, hs], k_ref[rows, hs], v_ref[rows, hs]
        v_t = vh.astype(F32).T.astype(BF16)
        c_prev = c_scr[h]
        n_prev = n_scr[h:h + 1, :]

        log_d = jnp.where(ri <= ci, a_j + gap_s, -jnp.inf)
        m_inter = a_j + m_prev
        m_t = jnp.maximum(m_inter, jnp.max(log_d, axis=0, keepdims=True))
        d_mat = jnp.exp(log_d - m_t)
        inter = jnp.exp(m_inter - m_t)
        s_qk = _dot_nt(kh, qh) * d_mat
        num = inter * _dot_nt(c_prev.astype(BF16), qh) + _dot(v_t, s_qk.astype(BF16))
        n_rows = jnp.broadcast_to(n_prev, (SUBLANES, d)).astype(BF16)
        den = inter * _dot_nt(n_rows, qh)[0:1, :] + jnp.sum(s_qk, axis=0, keepdims=True)
        hh = (num / jnp.maximum(jnp.abs(den), jnp.exp(-m_t))).T

        a_last = a_j[:, lc - 1:lc]
        log_w = a_last + gap_s
        m_new = jnp.maximum(a_last + m_prev, jnp.max(log_w, axis=0, keepdims=True))
        wk = jnp.exp(log_w - m_new) * kh.astype(F32)
        decay = jnp.exp(a_last + m_prev - m_new)
        c_scr[h] = decay * c_prev + _dot(v_t, wk.astype(BF16))
        n_scr[h:h + 1, :] = decay * n_prev + jnp.sum(wk, axis=0, keepdims=True)
        m_scr[h:h + 1, :] = jnp.broadcast_to(m_new, (1, LANES))

        hg = _sigmoid(o_ref[rows, hs]) * hh
        hn = hg * lax.rsqrt(jnp.mean(hg * hg, axis=-1, keepdims=True) + EPS)
        y_ref[rows, hs] = (hn * g_ref[:, hs]).astype(y_ref.dtype)


def _mix_out(x_ref, ya_ref, yb_ref, w_ref, g1_ref, gn_ref, sc_ref, sh_ref):
    half = ya_ref.shape[1]
    y = _dot(ya_ref[...], w_ref[0:half, :]) + _dot(yb_ref[...], w_ref[half:2 * half, :])
    x1 = x_ref[...] + g1_ref[0] * y
    return x1, _rmsnorm_mod(x1, gn_ref[...], sc_ref[0], sh_ref[0])


def _mix_specs(x2, ya, yb, w_out, tm, per_b):
    d = x2.shape[1]
    row = lambda i: (i, 0)
    bat = lambda i: (i // per_b, 0, 0)
    fix = lambda i: (0, 0)
    return [
        pl.BlockSpec((tm, d), row),
        pl.BlockSpec((tm, ya.shape[1]), row),
        pl.BlockSpec((tm, yb.shape[1]), row),
        pl.BlockSpec(w_out.shape, fix),
        pl.BlockSpec((1, 1, d), bat),
        pl.BlockSpec((1, d), fix),
        pl.BlockSpec((1, 1, d), bat),
        pl.BlockSpec((1, 1, d), bat),
    ]


def _outproj_router_kernel(x_ref, ya_ref, yb_ref, w_ref, g1_ref, gn_ref, sc_ref, sh_ref, rw_ref,
                           x1_ref, h_ref, lg_ref):
    x1, h = _mix_out(x_ref, ya_ref, yb_ref, w_ref, g1_ref, gn_ref, sc_ref, sh_ref)
    x1_ref[...] = x1
    h_ref[...] = h
    lg_ref[...] = lax.dot_general(rw_ref[...], h, (((1,), (1,)), ((), ())),
                                  precision=lax.Precision.HIGHEST, preferred_element_type=F32)


def _outproj_router(x2, ya, yb, w_out, g1, gn, sc, sh, seq, router_wt):
    n, d = x2.shape
    tm = min(ROW_TILE, seq)
    per_b = seq // tm
    row = lambda i: (i, 0)
    return pl.pallas_call(
        _outproj_router_kernel,
        grid=(n // tm,),
        in_specs=_mix_specs(x2, ya, yb, w_out, tm, per_b) + [pl.BlockSpec(router_wt.shape, lambda i: (0, 0))],
        out_specs=(pl.BlockSpec((tm, d), row), pl.BlockSpec((tm, d), row),
                   pl.BlockSpec((N_EXPERTS, tm), lambda i: (0, i))),
        out_shape=(jax.ShapeDtypeStruct((n, d), F32), jax.ShapeDtypeStruct((n, d), F32),
                   jax.ShapeDtypeStruct((N_EXPERTS, n), F32)),
        compiler_params=_cparams(("parallel",)),
        name="outproj_router",
    )(x2, ya, yb, w_out, g1, gn, sc, sh, router_wt)


def _outproj_ffn_kernel(x_ref, ya_ref, yb_ref, w_ref, g1_ref, gn_ref, sc_ref, sh_ref,
                        wg_ref, wu_ref, wd_ref, g2_ref, o_ref, act_scr, *, tf):
    x1, h = _mix_out(x_ref, ya_ref, yb_ref, w_ref, g1_ref, gn_ref, sc_ref, sh_ref)
    h = h.astype(BF16)
    d_ff = wg_ref.shape[1]
    for j in range(d_ff // tf):
        cs = slice(j * tf, (j + 1) * tf)
        act_scr[:, cs] = (_silu(_dot(h, wg_ref[:, cs])) * _dot(h, wu_ref[:, cs])).astype(BF16)
    o_ref[...] = x1 + g2_ref[0] * _dot(act_scr[...], wd_ref[...])


def _outproj_ffn(x2, ya, yb, w_out, g1, gn, sc, sh, wg, wu, wd, g2, seq):
    n, d = x2.shape
    d_ff = wg.shape[1]
    tm = min(ROW_TILE, seq)
    per_b = seq // tm
    tf = 2 * LANES
    row = lambda i: (i, 0)
    fix = lambda i: (0, 0)
    once = dict(pipeline_mode=pl.Buffered(1))
    return pl.pallas_call(
        functools.partial(_outproj_ffn_kernel, tf=tf),
        grid=(n // tm,),
        in_specs=_mix_specs(x2, ya, yb, w_out, tm, per_b) + [
            pl.BlockSpec(wg.shape, fix, **once),
            pl.BlockSpec(wu.shape, fix, **once),
            pl.BlockSpec(wd.shape, fix, **once),
            pl.BlockSpec((1, 1, d), lambda i: (i // per_b, 0, 0)),
        ],
        out_specs=pl.BlockSpec((tm, d), row),
        out_shape=jax.ShapeDtypeStruct((n, d), F32),
        scratch_shapes=[pltpu.VMEM((tm, d_ff), BF16)],
        compiler_params=_cparams(("parallel",)),
        name="outproj_ffn",
    )(x2, ya, yb, w_out, g1, gn, sc, sh, wg, wu, wd, g2)


def _route_kernel(lg_ref, dest_ref, wt_ref, meta_ref, cnt_scr, exc_scr, *, tile):
    n_e, n = lg_ref.shape
    lg = lg_ref[...]
    e_iota = lax.broadcasted_iota(I32, (n_e, n), 0)
    m1 = jnp.max(lg, axis=0, keepdims=True)
    e0 = jnp.min(jnp.where(lg == m1, e_iota, n_e), axis=0, keepdims=True)
    lg2 = jnp.where(e_iota == e0, -jnp.inf, lg)
    m2 = jnp.max(lg2, axis=0, keepdims=True)
    e1 = jnp.min(jnp.where(lg2 == m2, e_iota, n_e), axis=0, keepdims=True)
    ex = jnp.exp(m2 - m1)
    wt_ref[0:1, :] = 1.0 / (1.0 + ex)
    wt_ref[1:2, :] = ex / (1.0 + ex)
    oh0 = e_iota == e0
    oh1 = e_iota == e1
    cnt_scr[...] = jnp.where(oh0, 1.0, 0.0) + jnp.where(oh1, 1.0, 0.0)

    ri = lax.broadcasted_iota(I32, (LANES, 2 * LANES), 0)
    ci = lax.broadcasted_iota(I32, (LANES, 2 * LANES), 1)
    prefix_total = jnp.where((ci >= LANES) | (ri < ci), 1.0, 0.0).astype(BF16)

    def block(kb, carry):
        ls = pl.ds(pl.multiple_of(kb * LANES, LANES), LANES)
        both = _dot(cnt_scr[:, ls].astype(BF16), prefix_total)
        exc_scr[:, ls] = both[:, :LANES] + carry
        return carry + both[:, LANES:]

    total = lax.fori_loop(0, n // LANES, block, jnp.zeros((n_e, LANES), F32))
    padded = jnp.floor((total + (tile - 1)) / tile) * tile
    e_col = lax.broadcasted_iota(I32, (n_e, LANES), 0)
    starts = jnp.zeros((n_e, LANES), F32)
    for e in range(n_e - 1):
        starts = starts + jnp.where(e_col > e, padded[e:e + 1, :], 0.0)
    ends = starts + padded
    slot = starts[:, 0:1] + exc_scr[...]
    dest_ref[0:1, :] = jnp.sum(jnp.where(oh0, slot, 0.0), axis=0, keepdims=True).astype(I32)
    dest_ref[1:2, :] = jnp.sum(jnp.where(oh1, slot, 0.0), axis=0, keepdims=True).astype(I32)
    blk_start = (lax.broadcasted_iota(I32, (n_e, LANES), 1) * tile).astype(F32)
    blk_exp = jnp.sum(jnp.where(ends <= blk_start, 1.0, 0.0), axis=0, keepdims=True)
    meta_ref[0:1, :] = jnp.minimum(blk_exp, n_e - 1.0).astype(I32)
    meta_ref[1:2, :] = (ends[n_e - 1:n_e, :] / tile).astype(I32)
    on_diag = lax.broadcasted_iota(I32, (n_e, LANES), 1) == e_col
    meta_ref[2:3, :] = jnp.sum(jnp.where(on_diag, ends / tile, 0.0), axis=0, keepdims=True).astype(I32)
    meta_ref[3:4, :] = jnp.sum(jnp.where(on_diag, starts / tile, 0.0), axis=0, keepdims=True).astype(I32)
    meta_ref[4:SUBLANES, :] = jnp.zeros((SUBLANES - 4, LANES), I32)


def _route(logits_t, tile):
    n_e, n = logits_t.shape
    return pl.pallas_call(
        functools.partial(_route_kernel, tile=tile),
        out_shape=(jax.ShapeDtypeStruct((2, n), I32), jax.ShapeDtypeStruct((2, n), F32),
                   jax.ShapeDtypeStruct((SUBLANES, LANES), I32)),
        scratch_shapes=[pltpu.VMEM((n_e, n), F32), pltpu.VMEM((n_e, n), F32)],
        compiler_params=pltpu.CompilerParams(vmem_limit_bytes=VMEM_LIMIT),
        name="moe_route",
    )(logits_t)


def _dispatch_kernel(dest_ref, meta_ref, h_ref, wg_ref, wu_ref, wd_ref,
                     xs_ref, wgo_ref, wuo_ref, wdo_ref, zero_scr, sem, zsem, *, tile):
    i = pl.program_id(0)
    td = h_ref.shape[0]

    @pl.when(i == 0)
    def _():
        zero_scr[...] = jnp.zeros_like(zero_scr)
        n_blk = xs_ref.shape[0] // tile

        def zero_block(b):
            return pltpu.make_async_copy(zero_scr, xs_ref.at[pl.ds(pl.multiple_of(b * tile, tile), tile)], zsem)

        def each_block(fn):
            for e in range(N_EXPERTS):
                @pl.when(meta_ref[2, e] > meta_ref[3, e])
                def _():
                    fn(zero_block(meta_ref[2, e] - 1))

            def tail(b, _):
                fn(zero_block(b))
                return 0
            lax.fori_loop(meta_ref[1, 0], n_blk, tail, 0)

        each_block(lambda c: c.start())
        each_block(lambda c: c.wait())

    def copy(r, k):
        row = dest_ref[k * (pl.num_programs(0) * td) + i * td + r]
        return pltpu.make_async_copy(h_ref.at[pl.ds(r, 1)], xs_ref.at[pl.ds(row, 1)], sem)

    def start(r, _):
        copy(r, 0).start()
        copy(r, 1).start()
        return 0

    lax.fori_loop(0, td, start, 0, unroll=DMA_UNROLL)
    wgo_ref[...] = wg_ref[...].astype(BF16)
    wuo_ref[...] = wu_ref[...].astype(BF16)
    wdo_ref[...] = wd_ref[...].astype(BF16)
    for _ in range(2):
        pltpu.make_async_copy(h_ref, xs_ref.at[pl.ds(0, td)], sem).wait()


def _dispatch(dest, meta, h, n_rows, tile, weights):
    n, d = h.shape
    td = min(GATHER_TILE, n)
    n_steps = n // td
    flat = [w.reshape(-1, w.shape[-1]) for w in weights]
    slabs = [w.shape[0] // n_steps for w in flat]
    assert all(w.shape[0] == sl * n_steps and sl % (2 * SUBLANES) == 0 for w, sl in zip(flat, slabs))
    w_specs = [pl.BlockSpec((sl, w.shape[1]), lambda i, s, m: (i, 0)) for w, sl in zip(flat, slabs)]
    grid_spec = pltpu.PrefetchScalarGridSpec(
        num_scalar_prefetch=2,
        grid=(n_steps,),
        in_specs=[pl.BlockSpec((td, d), lambda i, s, m: (i, 0))] + w_specs,
        out_specs=[pl.BlockSpec(memory_space=pl.ANY)] + w_specs,
        scratch_shapes=[pltpu.VMEM((tile, d), h.dtype), pltpu.SemaphoreType.DMA(()),
                        pltpu.SemaphoreType.DMA(())],
    )
    outs = pl.pallas_call(
        functools.partial(_dispatch_kernel, tile=tile),
        grid_spec=grid_spec,
        out_shape=[jax.ShapeDtypeStruct((n_rows, d), h.dtype)]
        + [jax.ShapeDtypeStruct(w.shape, BF16) for w in flat],
        compiler_params=_cparams(("arbitrary",)),
        name="moe_dispatch",
    )(dest.reshape(-1), meta, h, *flat)
    return outs[0], [o.reshape(w.shape) for o, w in zip(outs[1:], weights)]


def _expert_kernel(meta_ref, x_ref, wg_ref, wu_ref, wd_ref, y_ref, xb_scr):
    i = pl.program_id(0)
    j = pl.program_id(1)

    @pl.when(i < meta_ref[1, 0])
    def _():
        @pl.when(j == 0)
        def _():
            xb_scr[...] = x_ref[...].astype(BF16)

        xb = xb_scr[...]
        act = (_silu(_dot(xb, wg_ref[0])) * _dot(xb, wu_ref[0])).astype(BF16)
        part = _dot(act, wd_ref[0])

        @pl.when(j == 0)
        def _():
            y_ref[...] = part

        @pl.when(j > 0)
        def _():
            y_ref[...] += part

    @pl.when((i >= meta_ref[1, 0]) & (j == 0))
    def _():
        y_ref[...] = jnp.zeros_like(y_ref)


def _experts(meta, xs, wg, wu, wd, tile):
    n_rows, d = xs.shape
    d_ff = wg.shape[2]
    tf = d_ff // 2 if (d_ff // 2) % (2 * LANES) == 0 else d_ff
    n_blk = n_rows // tile

    def blk(i, s):
        return jnp.minimum(i, s[1, 0] - 1)

    grid_spec = pltpu.PrefetchScalarGridSpec(
        num_scalar_prefetch=1,
        grid=(n_blk, d_ff // tf),
        in_specs=[
            pl.BlockSpec((tile, d), lambda i, j, s: (blk(i, s), 0)),
            pl.BlockSpec((1, d, tf), lambda i, j, s: (s[0, blk(i, s)], 0, jnp.where(i < s[1, 0], j, 0))),
            pl.BlockSpec((1, d, tf), lambda i, j, s: (s[0, blk(i, s)], 0, jnp.where(i < s[1, 0], j, 0))),
            pl.BlockSpec((1, tf, d), lambda i, j, s: (s[0, blk(i, s)], jnp.where(i < s[1, 0], j, 0), 0)),
        ],
        out_specs=pl.BlockSpec((tile, d), lambda i, j, s: (i, 0)),
        scratch_shapes=[pltpu.VMEM((tile, d), BF16)],
    )
    return pl.pallas_call(
        _expert_kernel,
        grid_spec=grid_spec,
        out_shape=jax.ShapeDtypeStruct((n_rows, d), F32),
        compiler_params=_cparams(("arbitrary", "arbitrary")),
        name="moe_experts",
    )(meta, xs, wg, wu, wd)


def _combine_kernel(dest_ref, y_ref, x_ref, wt_ref, g2_ref, gn_ref, o_ref, buf, sem):
    i = pl.program_id(0)
    n_steps = pl.num_programs(0)
    tc = x_ref.shape[0]
    slot = i % 2

    def gather(step, to_slot):
        def start(r, _):
            for k in range(2):
                row = dest_ref[k * (n_steps * tc) + step * tc + r]
                pltpu.make_async_copy(y_ref.at[pl.ds(row, 1)], buf.at[to_slot, k, pl.ds(r, 1)],
                                      sem.at[to_slot]).start()
            return 0
        lax.fori_loop(0, tc, start, 0, unroll=DMA_UNROLL)

    @pl.when(i == 0)
    def _():
        gather(0, 0)

    @pl.when(i + 1 < n_steps)
    def _():
        gather(i + 1, 1 - slot)

    for k in range(2):
        pltpu.make_async_copy(y_ref.at[pl.ds(0, tc)], buf.at[slot, k], sem.at[slot]).wait()
    wt = wt_ref[...]
    y = buf[slot, 0] * wt[:, 0:1] + buf[slot, 1] * wt[:, 1:2]
    x = x_ref[...] + g2_ref[0] * y
    ms = jnp.mean(x * x, axis=-1, keepdims=True)
    o_ref[...] = x * lax.rsqrt(ms + EPS) * gn_ref[...]


def _combine(dest, y, x1, wt, g2, gn, seq):
    n, d = x1.shape
    tc = min(GATHER_TILE, seq)
    per_b = seq // tc
    grid_spec = pltpu.PrefetchScalarGridSpec(
        num_scalar_prefetch=1,
        grid=(n // tc,),
        in_specs=[
            pl.BlockSpec(memory_space=pl.ANY),
            pl.BlockSpec((tc, d), lambda i, s: (i, 0)),
            pl.BlockSpec((tc, 2), lambda i, s: (i, 0)),
            pl.BlockSpec((1, 1, d), lambda i, s: (i // per_b, 0, 0)),
            pl.BlockSpec((1, d), lambda i, s: (0, 0)),
        ],
        out_specs=pl.BlockSpec((tc, d), lambda i, s: (i, 0)),
        scratch_shapes=[pltpu.VMEM((2, 2, tc, d), F32), pltpu.SemaphoreType.DMA((2,))],
    )
    return pl.pallas_call(
        _combine_kernel,
        grid_spec=grid_spec,
        out_shape=jax.ShapeDtypeStruct((n, d), F32),
        compiler_params=_cparams(("arbitrary",)),
        name="moe_combine_norm",
    )(dest.reshape(-1), y, x1, wt, g2, gn)


def _final_norm_kernel(x_ref, g_ref, o_ref):
    x = x_ref[...]
    ms = jnp.mean(x * x, axis=-1, keepdims=True)
    o_ref[...] = x * lax.rsqrt(ms + EPS) * g_ref[...]


def _mixer(x2, mod, l, batch, seq, norm_mix_g, w_in, b_in, cmp_pe, cmp_w1, cmp_w2,
           conv_w, conv_b, mlstm_norm_g):
    d = x2.shape[1]
    sh1, sc1 = mod[l, :, 0:d], mod[l, :, d:2 * d]
    w, b = _inproj_weights(w_in[l], b_in[l])
    q, kvp, sm, y_ml = _inproj(x2, norm_mix_g[l][None, :], sc1[:, None, :], sh1[:, None, :], w, b,
                               conv_w[l], conv_b[l], mlstm_norm_g[l], seq)
    pe, w1p, w2p = _compress_weights(cmp_pe[l], cmp_w1[l], cmp_w2[l])
    kc = _compress(kvp[0:NSA_KV_HEADS], pe, w1p, w2p, batch, seq)
    gates = sm[:, SM_GATE:SM_GATE + 24].reshape(batch, seq, NSA_KV_HEADS, NSA_REP * 3)
    gates = jnp.pad(gates.transpose(0, 2, 3, 1), ((0, 0), (0, 0), (0, 16 - NSA_REP * 3), (0, 0)))
    y_nsa = _nsa2(q, kvp, kc, gates.reshape(batch * NSA_KV_HEADS, 16, seq), batch, seq)
    return y_nsa, y_ml


def kernel(x, c, ada_w, ada_b, norm_mix_g, norm_ffn_g, w_in, b_in, cmp_pe, cmp_w1, cmp_w2, conv_w, conv_b, mlstm_norm_g, w_out, ffn_w_gate, ffn_w_up, ffn_w_down, router_w, moe_w_gate, moe_w_up, moe_w_down, final_norm_g):
    batch, seq, d = x.shape
    depth = ada_w.shape[0]
    n = batch * seq
    mod = _adaln(c, ada_w, ada_b)
    x2 = x.reshape(n, d)
    for l in range(depth):
        g1 = mod[l, :, 2 * d:3 * d][:, None, :]
        sh2 = mod[l, :, 3 * d:4 * d][:, None, :]
        sc2 = mod[l, :, 4 * d:5 * d][:, None, :]
        g2 = mod[l, :, 5 * d:6 * d][:, None, :]
        y_nsa, y_ml = _mixer(x2, mod, l, batch, seq, norm_mix_g, w_in, b_in, cmp_pe, cmp_w1, cmp_w2,
                             conv_w, conv_b, mlstm_norm_g)
        gn = norm_ffn_g[l][None, :]
        i = l // 2
        last = l == depth - 1
        if l % 2 == 0:
            x2 = _outproj_ffn(x2, y_nsa, y_ml, w_out[l].astype(BF16), g1, gn, sc2, sh2,
                              ffn_w_gate[i].astype(BF16), ffn_w_up[i].astype(BF16),
                              ffn_w_down[i].astype(BF16), g2, seq)
            if last:
                x2 = _final_norm(x2, final_norm_g)
        else:
            x1, h, logits_t = _outproj_router(x2, y_nsa, y_ml, w_out[l].astype(BF16), g1, gn, sc2, sh2, seq,
                                              router_w[i].T)
            n_rows = 2 * n + N_EXPERTS * MOE_TILE
            dest, wt, meta = _route(logits_t, MOE_TILE)
            xs, (wg, wu, wd) = _dispatch(dest, meta, h, n_rows, MOE_TILE,
                                         (moe_w_gate[i], moe_w_up[i], moe_w_down[i]))
            y = _experts(meta, xs, wg, wu, wd, MOE_TILE)
            unit = jnp.ones((1, d), F32)
            x2 = _combine(dest, y, x1, wt.T, g2, final_norm_g[None, :] if last else unit, seq)
            if not last:
                raise NotImplementedError("a MoE layer that is not the last layer")
    return x2.reshape(batch, seq, d)


def _final_norm(x2, g):
    n, d = x2.shape
    tm = min(ROW_TILE, n)
    return pl.pallas_call(
        _final_norm_kernel,
        grid=(n // tm,),
        in_specs=[pl.BlockSpec((tm, d), lambda i: (i, 0)), pl.BlockSpec((1, d), lambda i: (0, 0))],
        out_specs=pl.BlockSpec((tm, d), lambda i: (i, 0)),
        out_shape=jax.ShapeDtypeStruct((n, d), F32),
        compiler_params=_cparams(("parallel",)),
        name="final_norm",
    )(x2, g[None, :])
```

```python
import functools

import numpy as np
import jax
import jax.numpy as jnp
from jax import lax
from jax.experimental import pallas as pl
from jax.experimental.pallas import tpu as pltpu

F32 = jnp.float32
BF16 = jnp.bfloat16
I32 = jnp.int32

NSA_HEADS = 8
NSA_KV_HEADS = 2
NSA_REP = NSA_HEADS // NSA_KV_HEADS
NSA_HEAD_DIM = 64
CMP_LEN = 32
CMP_STRIDE = 16
CMP_HIDDEN = 128
SEL_BLOCK = 64
SEL_SHIFT = 6
SEL_TOP = 16
WINDOW = 512
Q_BLOCK = 256
FORCE_BONUS = 1e4
NEG_INF = -1e30
MLSTM_HEADS = 4
MLSTM_HEAD_DIM = 128
CONV_WIDTH = 4
N_EXPERTS = 8
EPS = 1e-6

LANES = 128
SUBLANES = 8
VMEM_LIMIT = 56 * 1024 * 1024

ROW_TILE = 512
SEL_CHUNK = 512
MLSTM_CHUNK = 256
MOE_TILE = 512
GATHER_TILE = 256
DMA_UNROLL = 8
VT_HEAD = 16

SM_GATE = 0
SM_I = 24
SM_F = 28


def _cparams(sem, vmem=VMEM_LIMIT):
    return pltpu.CompilerParams(dimension_semantics=sem, vmem_limit_bytes=vmem)


def _sigmoid(x):
    return 1.0 / (1.0 + jnp.exp(-x))


def _silu(x):
    return x * _sigmoid(x)


def _log_sigmoid(x):
    return jnp.minimum(x, 0.0) - jnp.log1p(jnp.exp(-jnp.abs(x)))


def _dot(a, b):
    return jnp.dot(a, b, preferred_element_type=F32)


def _dot_nt(a, b):
    return lax.dot_general(a, b, (((1,), (1,)), ((), ())), preferred_element_type=F32)


def _split3(x):
    hi = x.astype(BF16)
    r1 = x - hi.astype(F32)
    mid = r1.astype(BF16)
    lo = (r1 - mid.astype(F32)).astype(BF16)
    return hi, mid, lo


def _rmsnorm_mod(x, g, sc, sh):
    ms = jnp.mean(x * x, axis=-1, keepdims=True)
    y = x * lax.rsqrt(ms + EPS) * g
    return y * (1.0 + sc) + sh


def _adaln_kernel(c_ref, w_ref, b_ref, o_ref):
    c = c_ref[...]
    ca = _silu(c).astype(BF16)
    o_ref[0] = _dot(ca, w_ref[0].astype(BF16)) + b_ref[0]


def _adaln(c, ada_w, ada_b):
    depth, d, n6 = ada_w.shape
    b = c.shape[0]
    cp = jnp.zeros((SUBLANES, d), F32).at[:b].set(c)
    tn = n6 // 4
    out = pl.pallas_call(
        _adaln_kernel,
        grid=(depth, n6 // tn),
        in_specs=[
            pl.BlockSpec((SUBLANES, d), lambda l, j: (0, 0)),
            pl.BlockSpec((1, d, tn), lambda l, j: (l, 0, j)),
            pl.BlockSpec((1, 1, tn), lambda l, j: (l, 0, j)),
        ],
        out_specs=pl.BlockSpec((1, SUBLANES, tn), lambda l, j: (l, 0, j)),
        out_shape=jax.ShapeDtypeStruct((depth, SUBLANES, n6), F32),
        compiler_params=_cparams(("parallel", "parallel")),
        name="adaln",
    )(cp, ada_w, ada_b.reshape(depth, 1, n6))
    return out[:, :b]


def _inproj_kernel(x_ref, g_ref, sc_ref, sh_ref, wq_ref, wkv_ref, ws_ref, wm_ref, b_ref, cw_ref, cb_ref, gm_ref,
                   q_ref, kv_ref, sm_ref, y_ref,
                   qm_ref, km_ref, v_ref, o_ref, tail_scr, c_scr, n_scr, m_scr, *, per_b, lc):
    i = pl.program_id(0)

    @pl.when(i == 0)
    def _():
        tail_scr[...] = jnp.zeros_like(tail_scr)

    @pl.when(i % per_b == 0)
    def _():
        c_scr[...] = jnp.zeros_like(c_scr)
        n_scr[...] = jnp.zeros_like(n_scr)
        m_scr[...] = jnp.zeros_like(m_scr)

    h = _rmsnorm_mod(x_ref[...], g_ref[...], sc_ref[0], sh_ref[0]).astype(BF16)

    def sec(lo, width):
        for w_ref, base in ((wq_ref, 0), (wkv_ref, 512), (ws_ref, 1280), (wm_ref, 1408)):
            if base <= lo < base + w_ref.shape[1]:
                return _dot(h, w_ref[:, lo - base:lo - base + width]) + b_ref[:, lo:lo + width]
        raise ValueError(lo)

    strip = 2 * LANES
    row8 = lax.broadcasted_iota(I32, (SUBLANES, 1), 0)
    first = i % per_b == 0
    half = qm_ref.shape[1]
    for c0 in range(0, 2 * half, strip):
        cs = slice(c0, c0 + strip)
        cur = sec(1408 + c0, strip)
        tm = cur.shape[0]
        prev = jnp.where(first, 0.0, tail_scr[:, cs])
        tail_scr[:, cs] = cur[tm - SUBLANES:tm, :]
        y = cb_ref[:, cs]
        for tap in range(CONV_WIDTH):
            back = CONV_WIDTH - 1 - tap
            if back:
                rolled = pltpu.roll(cur, back, axis=0)
                top = jnp.where(row8 < back, pltpu.roll(prev, back, axis=0), rolled[0:SUBLANES])
                shifted = jnp.concatenate([top, rolled[SUBLANES:]], axis=0)
            else:
                shifted = cur
            y = y + shifted * cw_ref[tap:tap + 1, cs]
        y = _silu(y)
        if c0 < half:
            qm_ref[:, cs] = y.astype(BF16)
        else:
            km_ref[:, c0 - half:c0 - half + strip] = (y * (MLSTM_HEAD_DIM ** -0.5)).astype(BF16)

    for c0 in range(0, 512, strip):
        cs = slice(c0, c0 + strip)
        v_ref[:, cs] = sec(2432 + c0, strip).astype(BF16)
        o_ref[:, cs] = sec(2944 + c0, strip)
    sm_ref[...] = sec(512 + 6 * LANES, LANES)

    def q_strip(c0):
        q_ref[:, c0:c0 + strip] = (sec(c0, strip) * (NSA_HEAD_DIM ** -0.5)).astype(BF16)

    def kv_strip(s):
        pair = sec(512 + s * LANES, strip)
        kv_ref[s] = pair[:, :LANES].astype(BF16)
        kv_ref[s + 1] = pair[:, LANES:].astype(BF16)

    strips = [functools.partial(q_strip, c0) for c0 in range(0, 512, strip)]
    strips += [functools.partial(kv_strip, s) for s in range(0, 6, 2)]
    n_chunks = tm // lc
    for c in range(n_chunks):
        _mlstm_chunk(slice(c * lc, (c + 1) * lc), qm_ref, km_ref, v_ref, o_ref, sm_ref, gm_ref, y_ref,
                     c_scr, n_scr, m_scr)
        for emit in strips[c * len(strips) // n_chunks:(c + 1) * len(strips) // n_chunks]:
            emit()


def _inproj_weights(w_in, b_in):
    o_q, o_kv, o_gate, o_qk, o_v, o_o, o_i, o_f = [int(v) for v in np.cumsum((0, 512, 768, 24, 1024, 512, 512, 4))]

    def groups(a):
        kv = a[..., o_kv:o_gate].reshape(a.shape[:-1] + (3, 2, NSA_KV_HEADS, NSA_HEAD_DIM))
        kv = jnp.swapaxes(kv, -3, -2)
        kv = kv.reshape(a.shape[:-1] + (768,))
        small = jnp.concatenate(
            [a[..., o_gate:o_qk], a[..., o_i:o_f], a[..., o_f:o_f + 4],
             jnp.zeros(a.shape[:-1] + (LANES - 32,), a.dtype)], axis=-1)
        return a[..., o_q:o_kv], kv, small, a[..., o_qk:o_i]

    return tuple(w.astype(BF16) for w in groups(w_in)), jnp.concatenate(groups(b_in))[None, :].astype(F32)


def _inproj(x2, g, sc, sh, w, b, conv_w, conv_b, norm_g, seq):
    n, d = x2.shape
    tm = min(2 * ROW_TILE, seq)
    lc = min(MLSTM_CHUNK, tm)
    per_b = seq // tm
    wm = MLSTM_HEADS * MLSTM_HEAD_DIM
    row = lambda i: (i, 0)
    bat = lambda i: (i // per_b, 0, 0)
    fix = lambda i: (0, 0)
    outs = (
        jax.ShapeDtypeStruct((n, 512), BF16),
        jax.ShapeDtypeStruct((6, n, LANES), BF16),
        jax.ShapeDtypeStruct((n, 128), F32),
        jax.ShapeDtypeStruct((n, wm), BF16),
    )
    return pl.pallas_call(
        functools.partial(_inproj_kernel, per_b=per_b, lc=lc),
        grid=(n // tm,),
        in_specs=[
            pl.BlockSpec((tm, d), row),
            pl.BlockSpec((1, d), fix),
            pl.BlockSpec((1, 1, d), bat),
            pl.BlockSpec((1, 1, d), bat),
            *[pl.BlockSpec(wi.shape, fix) for wi in w],
            pl.BlockSpec(b.shape, fix),
            pl.BlockSpec(conv_w.shape, fix),
            pl.BlockSpec((1, conv_w.shape[1]), fix),
            pl.BlockSpec((1, wm), fix),
        ],
        out_specs=(
            pl.BlockSpec((tm, 512), row),
            pl.BlockSpec((6, tm, LANES), lambda i: (0, i, 0)),
            pl.BlockSpec((tm, 128), row),
            pl.BlockSpec((tm, wm), row),
        ),
        out_shape=outs,
        scratch_shapes=[
            pltpu.VMEM((tm, wm), BF16), pltpu.VMEM((tm, wm), BF16),
            pltpu.VMEM((tm, wm), BF16), pltpu.VMEM((tm, wm), F32),
            pltpu.VMEM((SUBLANES, 2 * wm), F32),
            pltpu.VMEM((MLSTM_HEADS, MLSTM_HEAD_DIM, MLSTM_HEAD_DIM), F32),
            pltpu.VMEM((SUBLANES, MLSTM_HEAD_DIM), F32),
            pltpu.VMEM((SUBLANES, LANES), F32),
        ],
        compiler_params=_cparams(("arbitrary",)),
        name="inproj_mlstm",
    )(x2, g, sc, sh, *w, b, conv_w, conv_b[None, :], norm_g[None, :])


def _compress_kernel(ch_ref, pe_ref, w1_ref, w2_ref, o_ref):
    ch = ch_ref[0].astype(F32)
    a0 = (ch + pe_ref[0:1, :]).astype(BF16)
    a1 = (ch + pe_ref[1:2, :]).astype(BF16)
    half = ch.shape[1]
    h0 = _dot(a0, w1_ref[0:half, :])
    h1 = _dot(a1, w1_ref[half:2 * half, :])
    n_chunk = ch.shape[0]
    hid = h0 + pltpu.roll(h1, n_chunk - 1, axis=0)
    o_ref[0] = _dot(_silu(hid).astype(BF16), w2_ref[...]).astype(BF16)


def _compress_weights(cmp_pe, cmp_w1, cmp_w2):
    dh, hid = NSA_HEAD_DIM, CMP_HIDDEN
    pe = jnp.concatenate([cmp_pe[0], cmp_pe[1]], axis=-1)
    pe = pe.reshape(2, CMP_STRIDE * LANES)
    w1 = cmp_w1.reshape(2, CMP_LEN, dh, hid)
    z = jnp.zeros((CMP_LEN, dh, hid), cmp_w1.dtype)
    wk = jnp.concatenate([w1[0], z], axis=1)
    wv = jnp.concatenate([z, w1[1]], axis=1)
    w1p = jnp.concatenate([wk, wv], axis=2).reshape(CMP_LEN * LANES, 2 * hid)
    z2 = jnp.zeros((hid, dh), cmp_w2.dtype)
    w2p = jnp.concatenate([jnp.concatenate([cmp_w2[0], z2], axis=1),
                           jnp.concatenate([z2, cmp_w2[1]], axis=1)], axis=0)
    return pe.astype(F32), w1p.astype(BF16), w2p.astype(BF16)


def _compress(kv_cmp, pe, w1p, w2p, batch, seq):
    g = kv_cmp.shape[0]
    n_chunk = seq // CMP_STRIDE
    ch = kv_cmp.reshape(g * batch, n_chunk, CMP_STRIDE * LANES)
    return pl.pallas_call(
        _compress_kernel,
        grid=(g * batch,),
        in_specs=[
            pl.BlockSpec((1, n_chunk, CMP_STRIDE * LANES), lambda i: (i, 0, 0)),
            pl.BlockSpec(pe.shape, lambda i: (0, 0)),
            pl.BlockSpec(w1p.shape, lambda i: (0, 0)),
            pl.BlockSpec(w2p.shape, lambda i: (0, 0)),
        ],
        out_specs=pl.BlockSpec((1, n_chunk, LANES), lambda i: (i, 0, 0)),
        out_shape=jax.ShapeDtypeStruct((g * batch, n_chunk, LANES), BF16),
        compiler_params=_cparams(("parallel",)),
        name="nsa_compress",
    )(ch, pe, w1p, w2p)


def _nsa2_kernel(slope_ref, q_ref, kc_ref, ks_ref, kw_ref, gate_ref, ovt_ref, feat_ref, featc_ref,
                 o_ref, kaug_c, vt_c, kaug_s, vt_s, kaug_w, vt_w, todo_scr, *, seq):
    qb = pl.program_id(1)
    n_cmp = seq // CMP_STRIDE
    n_sel = seq // SEL_BLOCK
    n_top = min(SEL_TOP, n_sel)
    n_chunk = seq // SEL_CHUNK
    dh = NSA_HEAD_DIM
    cols = NSA_REP * Q_BLOCK
    group_w = NSA_REP * dh
    blocks_per_chunk = SEL_CHUNK // SEL_BLOCK

    @pl.when(qb == 0)
    def _():
        def build(src_ref, f_ref, kaug, vt, n_rows):
            step = min(SEL_CHUNK, n_rows)
            lane = lax.broadcasted_iota(I32, (step, LANES), 1)
            ones_row = jnp.where(lax.broadcasted_iota(I32, (VT_HEAD, step), 0) == 0, 1.0, 0.0)
            for c0 in range(0, n_rows, step):
                x = src_ref[c0:c0 + step, :].astype(F32)
                kaug[c0:c0 + step, :] = jnp.where(lane < dh, x, f_ref[c0:c0 + step, :].astype(F32)).astype(BF16)
                vt[:, c0:c0 + step] = jnp.concatenate([ones_row, x.T[dh:]], axis=0).astype(BF16)

        for g in range(NSA_KV_HEADS):
            build(kc_ref.at[g, 0], featc_ref, kaug_c.at[g], vt_c.at[g], n_cmp)
            build(ks_ref.at[g], feat_ref, kaug_s.at[g], vt_s.at[g], seq)
            build(kw_ref.at[g], feat_ref, kaug_w.at[g], vt_w.at[g], seq)

    lane = lax.broadcasted_iota(I32, (Q_BLOCK, LANES), 1)
    blk_f = (lane - dh).astype(F32)
    t_lane = qb * Q_BLOCK + lax.broadcasted_iota(I32, (1, Q_BLOCK), 1)
    last = (qb * Q_BLOCK) // SEL_CHUNK

    def tile4(x):
        return jnp.concatenate([x] * NSA_REP, axis=1)

    def sel_scores(g, q_sel, c, size=SEL_CHUNK):
        start = pl.multiple_of(c * SEL_CHUNK, SEL_CHUNK)
        return _dot_nt(kaug_s[g, pl.ds(start, size), :], q_sel)

    def sel_update(g, c, s, carry):
        m, acc = carry
        start = pl.multiple_of(c * SEL_CHUNK, SEL_CHUNK)
        m_new = jnp.maximum(m, jnp.max(s, axis=0, keepdims=True))
        p = jnp.exp(s - m_new).astype(BF16)
        acc = jnp.exp(m - m_new) * acc + _dot(vt_s[g, :, pl.ds(start, s.shape[0])], p)
        return m_new, acc

    def front_scores(g):
        qf = q_ref[:, g * group_w:(g + 1) * group_w].astype(F32)
        parts = []
        for r in range(NSA_REP):
            pair = qf[:, (r // 2) * LANES:(r // 2 + 1) * LANES]
            if r % 2:
                pair = pltpu.roll(pair, dh, axis=1)
            slope = slope_ref[g * NSA_REP + r]
            parts.append(jnp.where(lane < dh, pair, jnp.where(lane == dh, slope, slope * SEL_BLOCK * blk_f)))
        q_all_f = jnp.concatenate(parts, axis=0)
        q_all = q_all_f.astype(BF16)

        s = _dot_nt(kaug_c[g], q_all)
        end_c = lax.broadcasted_iota(I32, (n_cmp, Q_BLOCK), 0) * CMP_STRIDE + (CMP_LEN - 1)
        return q_all_f, q_all, s + tile4(jnp.where(end_c <= t_lane, 0.0, NEG_INF))

    def front(g, q_all_f, q_all, s):
        e = jnp.exp(s - jnp.max(s, axis=0, keepdims=True))
        any_valid = tile4((t_lane >= CMP_LEN - 1).astype(F32))
        p_c = e * (any_valid / jnp.sum(e, axis=0, keepdims=True))
        o_c = _dot(vt_c[g], p_c.astype(BF16))

        p4 = p_c[:, 0:Q_BLOCK]
        for r in range(1, NSA_REP):
            p4 = p4 + p_c[:, r * Q_BLOCK:(r + 1) * Q_BLOCK]
        ovt = ovt_ref[...]
        imp = sum(_dot(ovt, piece) for piece in _split3(p4))
        j_col = lax.broadcasted_iota(I32, (n_sel, 1), 0)
        cur = jnp.right_shift(t_lane, SEL_SHIFT)
        forced = (j_col == 0) | (j_col == cur) | (j_col == cur - 1)
        imp = jnp.where(forced, imp + FORCE_BONUS, imp)
        imp = jnp.where(j_col <= cur, imp, -1.0)
        groups = [imp[v * SUBLANES:(v + 1) * SUBLANES, :] for v in range(n_sel // SUBLANES)]
        j_grp = lax.broadcasted_iota(I32, (SUBLANES, Q_BLOCK), 0)
        ranks = [jnp.zeros((SUBLANES, Q_BLOCK), F32) for _ in groups]
        for k in range(n_sel):
            row_k = groups[k // SUBLANES][k % SUBLANES:k % SUBLANES + 1, :]
            for v, grp in enumerate(groups):
                ge = jnp.where(row_k >= grp, 1.0, 0.0)
                gt = jnp.where(row_k > grp, 1.0, 0.0)
                if v * SUBLANES > k:
                    inc = ge
                elif (v + 1) * SUBLANES - 1 < k:
                    inc = gt
                else:
                    inc = jnp.where(j_grp + v * SUBLANES > k, ge, gt)
                ranks[v] = ranks[v] + inc
        sel_t = jnp.where(jnp.concatenate(ranks, axis=0) < n_top, 1.0, 0.0)
        pieces = [jnp.zeros((dh, Q_BLOCK), F32), sel_t]
        if n_sel < dh:
            pieces.append(jnp.zeros((dh - n_sel, Q_BLOCK), F32))
        selmat = jnp.concatenate(pieces, axis=0).T
        drop = jnp.concatenate([jnp.where(lane > dh, selmat, 1.0)] * NSA_REP, axis=0) < 0.5
        q_sel = jnp.where(drop, NEG_INF, q_all_f).astype(BF16)

        n_todo = jnp.int32(0)
        for c in range(n_chunk - 1):
            rows_c = sel_t[c * blocks_per_chunk:(c + 1) * blocks_per_chunk, :]
            wanted = (jnp.max(rows_c) > 0.5) & (c < last)
            todo_scr[g * n_chunk + n_todo] = jnp.int32(c)
            n_todo = n_todo + wanted.astype(I32)

        span = Q_BLOCK + WINDOW
        start = pl.multiple_of(jnp.maximum(qb * Q_BLOCK - WINDOW, 0), Q_BLOCK)
        s = _dot_nt(kaug_w[g, pl.ds(start, span), :], q_all)
        dist = t_lane - (start + lax.broadcasted_iota(I32, (span, Q_BLOCK), 0))
        in_band = pltpu.bitcast(dist, jnp.uint32) < WINDOW
        s = s + tile4(jnp.where(in_band, 0.0, NEG_INF))
        p = jnp.exp(s - jnp.max(s, axis=0, keepdims=True)).astype(BF16)
        acc = _dot(vt_w[g, :, pl.ds(start, span)], p)
        return q_sel, n_todo, o_c, acc / acc[0:1, :]

    def middle(g, q_sel, n_todo):
        def pair(i, carry):
            c0, c1 = todo_scr[g * n_chunk + 2 * i], todo_scr[g * n_chunk + 2 * i + 1]
            s0, s1 = sel_scores(g, q_sel, c0), sel_scores(g, q_sel, c1)
            return sel_update(g, c1, s1, sel_update(g, c0, s0, carry))

        def single(k, carry):
            c = todo_scr[g * n_chunk + k]
            return sel_update(g, c, sel_scores(g, q_sel, c), carry)

        init = (jnp.full((1, cols), NEG_INF, F32), jnp.zeros((VT_HEAD + dh, cols), F32))
        carry = lax.fori_loop(0, n_todo // 2, pair, init)
        return lax.fori_loop(2 * (n_todo // 2), n_todo, single, carry)

    def back_scores(g, q_sel, n_keys):
        pos = last * SEL_CHUNK + lax.broadcasted_iota(I32, (n_keys, Q_BLOCK), 0)
        return sel_scores(g, q_sel, last, n_keys) + tile4(jnp.where(pos <= t_lane, 0.0, NEG_INF))

    def back(g, s, carry, o_c, o_w):
        _, acc = sel_update(g, last, s, carry)
        o_s = acc / acc[0:1, :]
        gates = _sigmoid(gate_ref[0, g])
        mixed = []
        for r in range(NSA_REP):
            cs = slice(r * Q_BLOCK, (r + 1) * Q_BLOCK)
            mixed.append(gates[3 * r:3 * r + 1, :] * o_c[VT_HEAD:, cs]
                         + gates[3 * r + 1:3 * r + 2, :] * o_s[VT_HEAD:, cs]
                         + gates[3 * r + 2:3 * r + 3, :] * o_w[VT_HEAD:, cs])
        for pr in range(NSA_REP // 2):
            pair_t = jnp.concatenate([mixed[2 * pr], mixed[2 * pr + 1]], axis=0)
            lo = g * group_w + pr * LANES
            o_ref[:, lo:lo + LANES] = pair_t.T.astype(o_ref.dtype)

    starts = [front_scores(g) for g in range(NSA_KV_HEADS)]
    fronts = [front(g, *starts[g]) for g in range(NSA_KV_HEADS)]
    carries = [middle(g, fronts[g][0], fronts[g][1]) for g in range(NSA_KV_HEADS)]
    per_chunk = SEL_CHUNK // Q_BLOCK
    for v in range(per_chunk):
        @pl.when(qb % per_chunk == v)
        def _():
            scores = [back_scores(g, fronts[g][0], (v + 1) * Q_BLOCK) for g in range(NSA_KV_HEADS)]
            for g in range(NSA_KV_HEADS):
                _, _, o_c, o_w = fronts[g]
                back(g, scores[g], carries[g], o_c, o_w)


def _nsa_constants(seq):
    n_cmp, n_sel = seq // CMP_STRIDE, seq // SEL_BLOCK
    assert n_sel <= NSA_HEAD_DIM, "one feature lane per selection block"
    slopes = 2.0 ** (-8.0 * np.arange(1, NSA_HEADS + 1) / NSA_HEADS)
    far = (slopes[:, None] * SEL_BLOCK * np.arange(n_sel)[None, :]).astype(np.float32)
    assert np.array_equal(far.astype(BF16).astype(np.float32), far), "ALiBi features must be exact in bf16"
    lo_c = np.arange(n_cmp)[:, None] * CMP_STRIDE
    lo_s = np.arange(n_sel)[None, :] * SEL_BLOCK
    ov = np.clip(np.minimum(lo_c + CMP_LEN, lo_s + SEL_BLOCK) - np.maximum(lo_c, lo_s), 0, None) / CMP_LEN
    ov[n_cmp - 1] = 0.0

    def feats(pos):
        f = np.zeros((pos.shape[0], LANES), np.float32)
        f[:, NSA_HEAD_DIM] = pos % SEL_BLOCK
        blk = pos // SEL_BLOCK
        ok = (blk >= 1) & (blk < NSA_HEAD_DIM)
        f[np.nonzero(ok)[0], NSA_HEAD_DIM + blk[ok]] = 1.0
        return f

    feat = feats(np.arange(seq))
    pos_c = np.arange(n_cmp) * CMP_STRIDE + CMP_LEN - 1
    featc = feats(pos_c)
    featc[pos_c >= seq] = 0.0
    return (jnp.asarray(slopes, F32), jnp.asarray(ov.T, BF16), jnp.asarray(feat, BF16), jnp.asarray(featc, BF16))


def _nsa2(q, kvp, kc, gates_t, batch, seq):
    n, width = q.shape
    g = NSA_KV_HEADS
    n_qb = seq // Q_BLOCK
    n_cmp = seq // CMP_STRIDE
    vt_rows = VT_HEAD + NSA_HEAD_DIM
    slopes, ovt, feat, featc = _nsa_constants(seq)
    fix = lambda b, i, s: (0, 0)
    branch = lambda br: (lambda b, i, s: (br, b, 0))
    grid_spec = pltpu.PrefetchScalarGridSpec(
        num_scalar_prefetch=1,
        grid=(batch, n_qb),
        in_specs=[
            pl.BlockSpec((Q_BLOCK, width), lambda b, i, s: (b * n_qb + i, 0)),
            pl.BlockSpec((g, 1, n_cmp, LANES), lambda b, i, s: (0, b, 0, 0)),
            pl.BlockSpec((g, seq, LANES), branch(1)),
            pl.BlockSpec((g, seq, LANES), branch(2)),
            pl.BlockSpec((1, g, 16, Q_BLOCK), lambda b, i, s: (b, 0, 0, i)),
            pl.BlockSpec(ovt.shape, fix),
            pl.BlockSpec(feat.shape, fix),
            pl.BlockSpec(featc.shape, fix),
        ],
        out_specs=pl.BlockSpec((Q_BLOCK, width), lambda b, i, s: (b * n_qb + i, 0)),
        scratch_shapes=[
            pltpu.VMEM((g, n_cmp, LANES), BF16), pltpu.VMEM((g, vt_rows, n_cmp), BF16),
            pltpu.VMEM((g, seq, LANES), BF16), pltpu.VMEM((g, vt_rows, seq), BF16),
            pltpu.VMEM((g, seq, LANES), BF16), pltpu.VMEM((g, vt_rows, seq), BF16),
            pltpu.SMEM((g * (seq // SEL_CHUNK),), I32),
        ],
    )
    return pl.pallas_call(
        functools.partial(_nsa2_kernel, seq=seq),
        grid_spec=grid_spec,
        out_shape=jax.ShapeDtypeStruct((n, width), BF16),
        compiler_params=_cparams(("parallel", "arbitrary")),
        name="nsa_attention",
    )(slopes, q, kc.reshape(g, batch, n_cmp, LANES), kvp, kvp,
      gates_t.reshape(batch, g, 16, seq), ovt, feat, featc)


def _mlstm_chunk(rows, q_ref, k_ref, v_ref, o_ref, sm_ref, g_ref, y_ref, c_scr, n_scr, m_scr):
    lc = rows.stop - rows.start
    d = MLSTM_HEAD_DIM
    sm = sm_ref[rows, :]
    lf = _log_sigmoid(sm)
    ri = lax.broadcasted_iota(I32, (lc, lc), 0)
    ci = lax.broadcasted_iota(I32, (lc, lc), 1)
    tri = jnp.where(ri >= ci, 1.0, 0.0).astype(BF16)
    a_col = sum(_dot(tri, piece) for piece in _split3(lf))
    a_row = a_col.T

    def head(h):
        hs = slice(h * d, (h + 1) * d)
        a_j = a_row[SM_F + h:SM_F + h + 1, :]
        gap_s = sm[:, SM_I + h:SM_I + h + 1] - a_col[:, SM_F + h:SM_F + h + 1]
        m_prev = m_scr[h:h + 1, 0:1]
        qh, kh, vh = q_ref[rows, hs], k_ref[rows, hs], v_ref[rows, hs]
        v_t = vh.astype(F32).T.astype(BF16)
        c_prev = c_scr[h]
        n_prev = n_scr[h:h + 1, :]

        log_d = jnp.where(ri <= ci, a_j + gap_s, -jnp.inf)
        m_inter = a_j + m_prev
        m_t = jnp.maximum(m_inter, jnp.max(log_d, axis=0, keepdims=True))
        d_mat = jnp.exp(log_d - m_t)
        inter = jnp.exp(m_inter - m_t)
        yield
        s_qk = _dot_nt(kh, qh) * d_mat
        yield
        num = inter * _dot_nt(c_prev.astype(BF16), qh) + _dot(v_t, s_qk.astype(BF16))
        n_rows = jnp.broadcast_to(n_prev, (SUBLANES, d)).astype(BF16)
        den = inter * _dot_nt(n_rows, qh)[0:1, :] + jnp.sum(s_qk, axis=0, keepdims=True)
        hh = (num / jnp.maximum(jnp.abs(den), jnp.exp(-m_t))).T
        yield

        a_last = a_j[:, lc - 1:lc]
        log_w = a_last + gap_s
        m_new = jnp.maximum(a_last + m_prev, jnp.max(log_w, axis=0, keepdims=True))
        wk = jnp.exp(log_w - m_new) * kh.astype(F32)
        decay = jnp.exp(a_last + m_prev - m_new)
        c_scr[h] = decay * c_prev + _dot(v_t, wk.astype(BF16))
        n_scr[h:h + 1, :] = decay * n_prev + jnp.sum(wk, axis=0, keepdims=True)
        m_scr[h:h + 1, :] = jnp.broadcast_to(m_new, (1, LANES))
        yield

        hg = _sigmoid(o_ref[rows, hs]) * hh
        hn = hg * lax.rsqrt(jnp.mean(hg * hg, axis=-1, keepdims=True) + EPS)
        y_ref[rows, hs] = (hn * g_ref[:, hs]).astype(y_ref.dtype)

    heads = [head(h) for h in range(MLSTM_HEADS)]
    for _ in range(5):
        for gen in heads:
            next(gen, None)


def _mix_out(x_ref, ya_ref, yb_ref, w_ref, g1_ref, gn_ref, sc_ref, sh_ref):
    half = ya_ref.shape[1]
    y = _dot(ya_ref[...], w_ref[0:half, :]) + _dot(yb_ref[...], w_ref[half:2 * half, :])
    x1 = x_ref[...] + g1_ref[0] * y
    return x1, _rmsnorm_mod(x1, gn_ref[...], sc_ref[0], sh_ref[0])


def _mix_specs(x2, ya, yb, w_out, tm, per_b):
    d = x2.shape[1]
    row = lambda i: (i, 0)
    bat = lambda i: (i // per_b, 0, 0)
    fix = lambda i: (0, 0)
    return [
        pl.BlockSpec((tm, d), row),
        pl.BlockSpec((tm, ya.shape[1]), row),
        pl.BlockSpec((tm, yb.shape[1]), row),
        pl.BlockSpec(w_out.shape, fix),
        pl.BlockSpec((1, 1, d), bat),
        pl.BlockSpec((1, d), fix),
        pl.BlockSpec((1, 1, d), bat),
        pl.BlockSpec((1, 1, d), bat),
    ]


def _outproj_router_kernel(x_ref, ya_ref, yb_ref, w_ref, g1_ref, gn_ref, sc_ref, sh_ref, rw_ref,
                           x1_ref, h_ref, lg_ref):
    x1, h = _mix_out(x_ref, ya_ref, yb_ref, w_ref, g1_ref, gn_ref, sc_ref, sh_ref)
    x1_ref[...] = x1
    h_ref[...] = h
    lg_ref[...] = lax.dot_general(rw_ref[...], h, (((1,), (1,)), ((), ())),
                                  precision=lax.Precision.HIGHEST, preferred_element_type=F32)


def _outproj_router(x2, ya, yb, w_out, g1, gn, sc, sh, seq, router_wt):
    n, d = x2.shape
    tm = min(ROW_TILE, seq)
    per_b = seq // tm
    row = lambda i: (i, 0)
    return pl.pallas_call(
        _outproj_router_kernel,
        grid=(n // tm,),
        in_specs=_mix_specs(x2, ya, yb, w_out, tm, per_b) + [pl.BlockSpec(router_wt.shape, lambda i: (0, 0))],
        out_specs=(pl.BlockSpec((tm, d), row), pl.BlockSpec((tm, d), row),
                   pl.BlockSpec((N_EXPERTS, tm), lambda i: (0, i))),
        out_shape=(jax.ShapeDtypeStruct((n, d), F32), jax.ShapeDtypeStruct((n, d), F32),
                   jax.ShapeDtypeStruct((N_EXPERTS, n), F32)),
        compiler_params=_cparams(("parallel",)),
        name="outproj_router",
    )(x2, ya, yb, w_out, g1, gn, sc, sh, router_wt)


def _outproj_ffn_kernel(x_ref, ya_ref, yb_ref, w_ref, g1_ref, gn_ref, sc_ref, sh_ref,
                        wg_ref, wu_ref, wd_ref, g2_ref, o_ref, act_scr, *, tf):
    x1, h = _mix_out(x_ref, ya_ref, yb_ref, w_ref, g1_ref, gn_ref, sc_ref, sh_ref)
    h = h.astype(BF16)
    d_ff = wg_ref.shape[1]
    for j in range(d_ff // tf):
        cs = slice(j * tf, (j + 1) * tf)
        act_scr[:, cs] = (_silu(_dot(h, wg_ref[:, cs])) * _dot(h, wu_ref[:, cs])).astype(BF16)
    o_ref[...] = x1 + g2_ref[0] * _dot(act_scr[...], wd_ref[...])


def _outproj_ffn(x2, ya, yb, w_out, g1, gn, sc, sh, wg, wu, wd, g2, seq):
    n, d = x2.shape
    d_ff = wg.shape[1]
    tm = min(ROW_TILE, seq)
    per_b = seq // tm
    tf = 2 * LANES
    row = lambda i: (i, 0)
    fix = lambda i: (0, 0)
    once = dict(pipeline_mode=pl.Buffered(1))
    return pl.pallas_call(
        functools.partial(_outproj_ffn_kernel, tf=tf),
        grid=(n // tm,),
        in_specs=_mix_specs(x2, ya, yb, w_out, tm, per_b) + [
            pl.BlockSpec(wg.shape, fix, **once),
            pl.BlockSpec(wu.shape, fix, **once),
            pl.BlockSpec(wd.shape, fix, **once),
            pl.BlockSpec((1, 1, d), lambda i: (i // per_b, 0, 0)),
        ],
        out_specs=pl.BlockSpec((tm, d), row),
        out_shape=jax.ShapeDtypeStruct((n, d), F32),
        scratch_shapes=[pltpu.VMEM((tm, d_ff), BF16)],
        compiler_params=_cparams(("parallel",)),
        name="outproj_ffn",
    )(x2, ya, yb, w_out, g1, gn, sc, sh, wg, wu, wd, g2)


def _route_kernel(lg_ref, dest_ref, wt_ref, meta_ref, cnt_scr, exc_scr, *, tile):
    n_e, n = lg_ref.shape
    lg = lg_ref[...]
    e_iota = lax.broadcasted_iota(I32, (n_e, n), 0)
    m1 = jnp.max(lg, axis=0, keepdims=True)
    e0 = jnp.min(jnp.where(lg == m1, e_iota, n_e), axis=0, keepdims=True)
    lg2 = jnp.where(e_iota == e0, -jnp.inf, lg)
    m2 = jnp.max(lg2, axis=0, keepdims=True)
    e1 = jnp.min(jnp.where(lg2 == m2, e_iota, n_e), axis=0, keepdims=True)
    ex = jnp.exp(m2 - m1)
    wt_ref[0:1, :] = 1.0 / (1.0 + ex)
    wt_ref[1:2, :] = ex / (1.0 + ex)
    oh0 = e_iota == e0
    oh1 = e_iota == e1
    cnt_scr[...] = jnp.where(oh0, 1.0, 0.0) + jnp.where(oh1, 1.0, 0.0)

    ri = lax.broadcasted_iota(I32, (LANES, 2 * LANES), 0)
    ci = lax.broadcasted_iota(I32, (LANES, 2 * LANES), 1)
    prefix_total = jnp.where((ci >= LANES) | (ri < ci), 1.0, 0.0).astype(BF16)

    def block(kb, carry):
        ls = pl.ds(pl.multiple_of(kb * LANES, LANES), LANES)
        both = _dot(cnt_scr[:, ls].astype(BF16), prefix_total)
        exc_scr[:, ls] = both[:, :LANES] + carry
        return carry + both[:, LANES:]

    total = lax.fori_loop(0, n // LANES, block, jnp.zeros((n_e, LANES), F32))
    padded = jnp.floor((total + (tile - 1)) / tile) * tile
    e_col = lax.broadcasted_iota(I32, (n_e, LANES), 0)
    starts = jnp.zeros((n_e, LANES), F32)
    for e in range(n_e - 1):
        starts = starts + jnp.where(e_col > e, padded[e:e + 1, :], 0.0)
    ends = starts + padded
    slot = starts[:, 0:1] + exc_scr[...]
    dest_ref[0:1, :] = jnp.sum(jnp.where(oh0, slot, 0.0), axis=0, keepdims=True).astype(I32)
    dest_ref[1:2, :] = jnp.sum(jnp.where(oh1, slot, 0.0), axis=0, keepdims=True).astype(I32)
    blk_start = (lax.broadcasted_iota(I32, (n_e, LANES), 1) * tile).astype(F32)
    blk_exp = jnp.sum(jnp.where(ends <= blk_start, 1.0, 0.0), axis=0, keepdims=True)
    meta_ref[0:1, :] = jnp.minimum(blk_exp, n_e - 1.0).astype(I32)
    meta_ref[1:2, :] = (ends[n_e - 1:n_e, :] / tile).astype(I32)
    on_diag = lax.broadcasted_iota(I32, (n_e, LANES), 1) == e_col
    meta_ref[2:3, :] = jnp.sum(jnp.where(on_diag, ends / tile, 0.0), axis=0, keepdims=True).astype(I32)
    meta_ref[3:4, :] = jnp.sum(jnp.where(on_diag, starts / tile, 0.0), axis=0, keepdims=True).astype(I32)
    meta_ref[4:SUBLANES, :] = jnp.zeros((SUBLANES - 4, LANES), I32)


def _route(logits_t, tile):
    n_e, n = logits_t.shape
    return pl.pallas_call(
        functools.partial(_route_kernel, tile=tile),
        out_shape=(jax.ShapeDtypeStruct((2, n), I32), jax.ShapeDtypeStruct((2, n), F32),
                   jax.ShapeDtypeStruct((SUBLANES, LANES), I32)),
        scratch_shapes=[pltpu.VMEM((n_e, n), F32), pltpu.VMEM((n_e, n), F32)],
        compiler_params=pltpu.CompilerParams(vmem_limit_bytes=VMEM_LIMIT),
        name="moe_route",
    )(logits_t)


def _dispatch_kernel(dest_ref, meta_ref, h_ref, wg_ref, wu_ref, wd_ref,
                     xs_ref, wgo_ref, wuo_ref, wdo_ref, zero_scr, sem, zsem, *, tile):
    i = pl.program_id(0)
    td = h_ref.shape[0]

    @pl.when(i == 0)
    def _():
        zero_scr[...] = jnp.zeros_like(zero_scr)
        n_blk = xs_ref.shape[0] // tile

        def zero_block(b):
            return pltpu.make_async_copy(zero_scr, xs_ref.at[pl.ds(pl.multiple_of(b * tile, tile), tile)], zsem)

        def each_block(fn):
            for e in range(N_EXPERTS):
                @pl.when(meta_ref[2, e] > meta_ref[3, e])
                def _():
                    fn(zero_block(meta_ref[2, e] - 1))

            def tail(b, _):
                fn(zero_block(b))
                return 0
            lax.fori_loop(meta_ref[1, 0], n_blk, tail, 0)

        each_block(lambda c: c.start())
        each_block(lambda c: c.wait())

    def copy(r, k):
        row = dest_ref[k * (pl.num_programs(0) * td) + i * td + r]
        return pltpu.make_async_copy(h_ref.at[pl.ds(r, 1)], xs_ref.at[pl.ds(row, 1)], sem)

    def start(r, _):
        copy(r, 0).start()
        copy(r, 1).start()
        return 0

    lax.fori_loop(0, td, start, 0, unroll=DMA_UNROLL)
    wgo_ref[...] = wg_ref[...].astype(BF16)
    wuo_ref[...] = wu_ref[...].astype(BF16)
    wdo_ref[...] = wd_ref[...].astype(BF16)
    for _ in range(2):
        pltpu.make_async_copy(h_ref, xs_ref.at[pl.ds(0, td)], sem).wait()


def _dispatch(dest, meta, h, n_rows, tile, weights):
    n, d = h.shape
    td = min(GATHER_TILE, n)
    n_steps = n // td
    flat = [w.reshape(-1, w.shape[-1]) for w in weights]
    slabs = [w.shape[0] // n_steps for w in flat]
    assert all(w.shape[0] == sl * n_steps and sl % (2 * SUBLANES) == 0 for w, sl in zip(flat, slabs))
    w_specs = [pl.BlockSpec((sl, w.shape[1]), lambda i, s, m: (i, 0)) for w, sl in zip(flat, slabs)]
    grid_spec = pltpu.PrefetchScalarGridSpec(
        num_scalar_prefetch=2,
        grid=(n_steps,),
        in_specs=[pl.BlockSpec((td, d), lambda i, s, m: (i, 0))] + w_specs,
        out_specs=[pl.BlockSpec(memory_space=pl.ANY)] + w_specs,
        scratch_shapes=[pltpu.VMEM((tile, d), h.dtype), pltpu.SemaphoreType.DMA(()),
                        pltpu.SemaphoreType.DMA(())],
    )
    outs = pl.pallas_call(
        functools.partial(_dispatch_kernel, tile=tile),
        grid_spec=grid_spec,
        out_shape=[jax.ShapeDtypeStruct((n_rows, d), h.dtype)]
        + [jax.ShapeDtypeStruct(w.shape, BF16) for w in flat],
        compiler_params=_cparams(("arbitrary",)),
        name="moe_dispatch",
    )(dest.reshape(-1), meta, h, *flat)
    return outs[0], [o.reshape(w.shape) for o, w in zip(outs[1:], weights)]


def _expert_kernel(meta_ref, x_ref, wg_ref, wu_ref, wd_ref, y_ref, xb_scr):
    i = pl.program_id(0)
    j = pl.program_id(1)

    @pl.when(i < meta_ref[1, 0])
    def _():
        @pl.when(j == 0)
        def _():
            xb_scr[...] = x_ref[...].astype(BF16)

        xb = xb_scr[...]
        act = (_silu(_dot(xb, wg_ref[0])) * _dot(xb, wu_ref[0])).astype(BF16)
        part = _dot(act, wd_ref[0])

        @pl.when(j == 0)
        def _():
            y_ref[...] = part

        @pl.when(j > 0)
        def _():
            y_ref[...] += part

    @pl.when((i >= meta_ref[1, 0]) & (j == 0))
    def _():
        y_ref[...] = jnp.zeros_like(y_ref)


def _experts(meta, xs, wg, wu, wd, tile):
    n_rows, d = xs.shape
    d_ff = wg.shape[2]
    tf = d_ff // 2 if (d_ff // 2) % (2 * LANES) == 0 else d_ff
    n_blk = n_rows // tile

    def blk(i, s):
        return jnp.minimum(i, s[1, 0] - 1)

    grid_spec = pltpu.PrefetchScalarGridSpec(
        num_scalar_prefetch=1,
        grid=(n_blk, d_ff // tf),
        in_specs=[
            pl.BlockSpec((tile, d), lambda i, j, s: (blk(i, s), 0)),
            pl.BlockSpec((1, d, tf), lambda i, j, s: (s[0, blk(i, s)], 0, jnp.where(i < s[1, 0], j, 0))),
            pl.BlockSpec((1, d, tf), lambda i, j, s: (s[0, blk(i, s)], 0, jnp.where(i < s[1, 0], j, 0))),
            pl.BlockSpec((1, tf, d), lambda i, j, s: (s[0, blk(i, s)], jnp.where(i < s[1, 0], j, 0), 0)),
        ],
        out_specs=pl.BlockSpec((tile, d), lambda i, j, s: (i, 0)),
        scratch_shapes=[pltpu.VMEM((tile, d), BF16)],
    )
    return pl.pallas_call(
        _expert_kernel,
        grid_spec=grid_spec,
        out_shape=jax.ShapeDtypeStruct((n_rows, d), F32),
        compiler_params=_cparams(("arbitrary", "arbitrary")),
        name="moe_experts",
    )(meta, xs, wg, wu, wd)


def _combine_kernel(dest_ref, y_ref, x_ref, wt_ref, g2_ref, gn_ref, o_ref, buf, sem):
    i = pl.program_id(0)
    n_steps = pl.num_programs(0)
    tc = x_ref.shape[0]
    slot = i % 2

    def gather(step, to_slot):
        def start(r, _):
            for k in range(2):
                row = dest_ref[k * (n_steps * tc) + step * tc + r]
                pltpu.make_async_copy(y_ref.at[pl.ds(row, 1)], buf.at[to_slot, k, pl.ds(r, 1)],
                                      sem.at[to_slot]).start()
            return 0
        lax.fori_loop(0, tc, start, 0, unroll=DMA_UNROLL)

    @pl.when(i == 0)
    def _():
        gather(0, 0)

    @pl.when(i + 1 < n_steps)
    def _():
        gather(i + 1, 1 - slot)

    for k in range(2):
        pltpu.make_async_copy(y_ref.at[pl.ds(0, tc)], buf.at[slot, k], sem.at[slot]).wait()
    wt = wt_ref[...]
    y = buf[slot, 0] * wt[:, 0:1] + buf[slot, 1] * wt[:, 1:2]
    x = x_ref[...] + g2_ref[0] * y
    ms = jnp.mean(x * x, axis=-1, keepdims=True)
    o_ref[...] = x * lax.rsqrt(ms + EPS) * gn_ref[...]


def _combine(dest, y, x1, wt, g2, gn, seq):
    n, d = x1.shape
    tc = min(GATHER_TILE, seq)
    per_b = seq // tc
    grid_spec = pltpu.PrefetchScalarGridSpec(
        num_scalar_prefetch=1,
        grid=(n // tc,),
        in_specs=[
            pl.BlockSpec(memory_space=pl.ANY),
            pl.BlockSpec((tc, d), lambda i, s: (i, 0)),
            pl.BlockSpec((tc, 2), lambda i, s: (i, 0)),
            pl.BlockSpec((1, 1, d), lambda i, s: (i // per_b, 0, 0)),
            pl.BlockSpec((1, d), lambda i, s: (0, 0)),
        ],
        out_specs=pl.BlockSpec((tc, d), lambda i, s: (i, 0)),
        scratch_shapes=[pltpu.VMEM((2, 2, tc, d), F32), pltpu.SemaphoreType.DMA((2,))],
    )
    return pl.pallas_call(
        _combine_kernel,
        grid_spec=grid_spec,
        out_shape=jax.ShapeDtypeStruct((n, d), F32),
        compiler_params=_cparams(("arbitrary",)),
        name="moe_combine_norm",
    )(dest.reshape(-1), y, x1, wt, g2, gn)


def _final_norm_kernel(x_ref, g_ref, o_ref):
    x = x_ref[...]
    ms = jnp.mean(x * x, axis=-1, keepdims=True)
    o_ref[...] = x * lax.rsqrt(ms + EPS) * g_ref[...]


def _mixer(x2, mod, l, batch, seq, norm_mix_g, w_in, b_in, cmp_pe, cmp_w1, cmp_w2,
           conv_w, conv_b, mlstm_norm_g):
    d = x2.shape[1]
    sh1, sc1 = mod[l, :, 0:d], mod[l, :, d:2 * d]
    w, b = _inproj_weights(w_in[l], b_in[l])
    q, kvp, sm, y_ml = _inproj(x2, norm_mix_g[l][None, :], sc1[:, None, :], sh1[:, None, :], w, b,
                               conv_w[l], conv_b[l], mlstm_norm_g[l], seq)
    pe, w1p, w2p = _compress_weights(cmp_pe[l], cmp_w1[l], cmp_w2[l])
    kc = _compress(kvp[0:NSA_KV_HEADS], pe, w1p, w2p, batch, seq)
    gates = sm[:, SM_GATE:SM_GATE + 24].reshape(batch, seq, NSA_KV_HEADS, NSA_REP * 3)
    gates = jnp.pad(gates.transpose(0, 2, 3, 1), ((0, 0), (0, 0), (0, 16 - NSA_REP * 3), (0, 0)))
    y_nsa = _nsa2(q, kvp, kc, gates.reshape(batch * NSA_KV_HEADS, 16, seq), batch, seq)
    return y_nsa, y_ml


def kernel(x, c, ada_w, ada_b, norm_mix_g, norm_ffn_g, w_in, b_in, cmp_pe, cmp_w1, cmp_w2, conv_w, conv_b, mlstm_norm_g, w_out, ffn_w_gate, ffn_w_up, ffn_w_down, router_w, moe_w_gate, moe_w_up, moe_w_down, final_norm_g):
    batch, seq, d = x.shape
    depth = ada_w.shape[0]
    n = batch * seq
    mod = _adaln(c, ada_w, ada_b)
    x2 = x.reshape(n, d)
    for l in range(depth):
        g1 = mod[l, :, 2 * d:3 * d][:, None, :]
        sh2 = mod[l, :, 3 * d:4 * d][:, None, :]
        sc2 = mod[l, :, 4 * d:5 * d][:, None, :]
        g2 = mod[l, :, 5 * d:6 * d][:, None, :]
        y_nsa, y_ml = _mixer(x2, mod, l, batch, seq, norm_mix_g, w_in, b_in, cmp_pe, cmp_w1, cmp_w2,
                             conv_w, conv_b, mlstm_norm_g)
        gn = norm_ffn_g[l][None, :]
        i = l // 2
        last = l == depth - 1
        if l % 2 == 0:
            x2 = _outproj_ffn(x2, y_nsa, y_ml, w_out[l].astype(BF16), g1, gn, sc2, sh2,
                              ffn_w_gate[i].astype(BF16), ffn_w_up[i].astype(BF16),
                              ffn_w_down[i].astype(BF16), g2, seq)
            if last:
                x2 = _final_norm(x2, final_norm_g)
        else:
            x1, h, logits_t = _outproj_router(x2, y_nsa, y_ml, w_out[l].astype(BF16), g1, gn, sc2, sh2, seq,
                                              router_w[i].T)
            n_rows = 2 * n + N_EXPERTS * MOE_TILE
            dest, wt, meta = _route(logits_t, MOE_TILE)
            xs, (wg, wu, wd) = _dispatch(dest, meta, h, n_rows, MOE_TILE,
                                         (moe_w_gate[i], moe_w_up[i], moe_w_down[i]))
            y = _experts(meta, xs, wg, wu, wd, MOE_TILE)
            unit = jnp.ones((1, d), F32)
            x2 = _combine(dest, y, x1, wt.T, g2, final_norm_g[None, :] if last else unit, seq)
            if not last:
                raise NotImplementedError("a MoE layer that is not the last layer")
    return x2.reshape(batch, seq, d)


def _final_norm(x2, g):
    n, d = x2.shape
    tm = min(ROW_TILE, n)
    return pl.pallas_call(
        _final_norm_kernel,
        grid=(n // tm,),
        in_specs=[pl.BlockSpec((tm, d), lambda i: (i, 0)), pl.BlockSpec((1, d), lambda i: (0, 0))],
        out_specs=pl.BlockSpec((tm, d), lambda i: (i, 0)),
        out_shape=jax.ShapeDtypeStruct((n, d), F32),
        compiler_params=_cparams(("parallel",)),
        name="final_norm",
    )(x2, g[None, :])
```

```python
import functools

import numpy as np
import jax
import jax.numpy as jnp
from jax import lax
from jax.experimental import pallas as pl
from jax.experimental.pallas import tpu as pltpu

F32 = jnp.float32
BF16 = jnp.bfloat16
I32 = jnp.int32

NSA_HEADS = 8
NSA_KV_HEADS = 2
NSA_REP = NSA_HEADS // NSA_KV_HEADS
NSA_HEAD_DIM = 64
CMP_LEN = 32
CMP_STRIDE = 16
CMP_HIDDEN = 128
SEL_BLOCK = 64
SEL_SHIFT = 6
SEL_TOP = 16
WINDOW = 512
Q_BLOCK = 256
FORCE_BONUS = 1e4
NEG_INF = -1e30
MLSTM_HEADS = 4
MLSTM_HEAD_DIM = 128
CONV_WIDTH = 4
N_EXPERTS = 8
EPS = 1e-6

LANES = 128
SUBLANES = 8
VMEM_LIMIT = 56 * 1024 * 1024

ROW_TILE = 512
SEL_CHUNK = 512
MLSTM_CHUNK = 256
MOE_TILE = 512
GATHER_TILE = 256
DMA_UNROLL = 8
VT_HEAD = 16

SM_GATE = 0
SM_I = 24
SM_F = 28


def _cparams(sem, vmem=VMEM_LIMIT):
    return pltpu.CompilerParams(dimension_semantics=sem, vmem_limit_bytes=vmem)


def _sigmoid(x):
    return 1.0 / (1.0 + jnp.exp(-x))


def _silu(x):
    return x * _sigmoid(x)


def _log_sigmoid(x):
    return jnp.minimum(x, 0.0) - jnp.log1p(jnp.exp(-jnp.abs(x)))


def _dot(a, b):
    return jnp.dot(a, b, preferred_element_type=F32)


def _dot_nt(a, b):
    return lax.dot_general(a, b, (((1,), (1,)), ((), ())), preferred_element_type=F32)


def _split3(x):
    hi = x.astype(BF16)
    r1 = x - hi.astype(F32)
    mid = r1.astype(BF16)
    lo = (r1 - mid.astype(F32)).astype(BF16)
    return hi, mid, lo


def _rmsnorm_mod(x, g, sc, sh):
    ms = jnp.mean(x * x, axis=-1, keepdims=True)
    y = x * lax.rsqrt(ms + EPS) * g
    return y * (1.0 + sc) + sh


def _adaln_kernel(c_ref, w_ref, b_ref, o_ref):
    c = c_ref[...]
    ca = _silu(c).astype(BF16)
    o_ref[0] = _dot(ca, w_ref[0].astype(BF16)) + b_ref[0]


def _adaln(c, ada_w, ada_b):
    depth, d, n6 = ada_w.shape
    b = c.shape[0]
    cp = jnp.zeros((SUBLANES, d), F32).at[:b].set(c)
    tn = n6 // 4
    out = pl.pallas_call(
        _adaln_kernel,
        grid=(depth, n6 // tn),
        in_specs=[
            pl.BlockSpec((SUBLANES, d), lambda l, j: (0, 0)),
            pl.BlockSpec((1, d, tn), lambda l, j: (l, 0, j)),
            pl.BlockSpec((1, 1, tn), lambda l, j: (l, 0, j)),
        ],
        out_specs=pl.BlockSpec((1, SUBLANES, tn), lambda l, j: (l, 0, j)),
        out_shape=jax.ShapeDtypeStruct((depth, SUBLANES, n6), F32),
        compiler_params=_cparams(("parallel", "parallel")),
        name="adaln",
    )(cp, ada_w, ada_b.reshape(depth, 1, n6))
    return out[:, :b]


def _inproj_kernel(x_ref, g_ref, sc_ref, sh_ref, wq_ref, wkv_ref, ws_ref, wm_ref, b_ref, cw_ref, cb_ref, gm_ref,
                   q_ref, kv_ref, sm_ref, y_ref,
                   qm_ref, km_ref, v_ref, o_ref, tail_scr, c_scr, n_scr, m_scr, *, per_b, lc):
    i = pl.program_id(0)

    @pl.when(i == 0)
    def _():
        tail_scr[...] = jnp.zeros_like(tail_scr)

    @pl.when(i % per_b == 0)
    def _():
        c_scr[...] = jnp.zeros_like(c_scr)
        n_scr[...] = jnp.zeros_like(n_scr)
        m_scr[...] = jnp.zeros_like(m_scr)

    h = _rmsnorm_mod(x_ref[...], g_ref[...], sc_ref[0], sh_ref[0]).astype(BF16)

    def sec(lo, width):
        for w_ref, base in ((wq_ref, 0), (wkv_ref, 512), (ws_ref, 1280), (wm_ref, 1408)):
            if base <= lo < base + w_ref.shape[1]:
                return _dot(h, w_ref[:, lo - base:lo - base + width]) + b_ref[:, lo:lo + width]
        raise ValueError(lo)

    strip = 2 * LANES
    row8 = lax.broadcasted_iota(I32, (SUBLANES, 1), 0)
    first = i % per_b == 0
    half = qm_ref.shape[1]
    for c0 in range(0, 2 * half, strip):
        cs = slice(c0, c0 + strip)
        cur = sec(1408 + c0, strip)
        tm = cur.shape[0]
        prev = jnp.where(first, 0.0, tail_scr[:, cs])
        tail_scr[:, cs] = cur[tm - SUBLANES:tm, :]
        y = cb_ref[:, cs]
        for tap in range(CONV_WIDTH):
            back = CONV_WIDTH - 1 - tap
            if back:
                rolled = pltpu.roll(cur, back, axis=0)
                top = jnp.where(row8 < back, pltpu.roll(prev, back, axis=0), rolled[0:SUBLANES])
                shifted = jnp.concatenate([top, rolled[SUBLANES:]], axis=0)
            else:
                shifted = cur
            y = y + shifted * cw_ref[tap:tap + 1, cs]
        y = _silu(y)
        if c0 < half:
            qm_ref[:, cs] = y.astype(BF16)
        else:
            km_ref[:, c0 - half:c0 - half + strip] = (y * (MLSTM_HEAD_DIM ** -0.5)).astype(BF16)

    for c0 in range(0, 512, strip):
        cs = slice(c0, c0 + strip)
        v_ref[:, cs] = sec(2432 + c0, strip).astype(BF16)
        o_ref[:, cs] = sec(2944 + c0, strip)
    sm_ref[...] = sec(512 + 6 * LANES, LANES)

    def q_strip(c0):
        q_ref[:, c0:c0 + strip] = (sec(c0, strip) * (NSA_HEAD_DIM ** -0.5)).astype(BF16)

    def kv_strip(s):
        pair = sec(512 + s * LANES, strip)
        kv_ref[s] = pair[:, :LANES].astype(BF16)
        kv_ref[s + 1] = pair[:, LANES:].astype(BF16)

    strips = [functools.partial(q_strip, c0) for c0 in range(0, 512, strip)]
    strips += [functools.partial(kv_strip, s) for s in range(0, 6, 2)]
    n_chunks = tm // lc
    for c in range(n_chunks):
        _mlstm_chunk(slice(c * lc, (c + 1) * lc), qm_ref, km_ref, v_ref, o_ref, sm_ref, gm_ref, y_ref,
                     c_scr, n_scr, m_scr)
        for emit in strips[c * len(strips) // n_chunks:(c + 1) * len(strips) // n_chunks]:
            emit()


def _inproj_weights(w_in, b_in):
    o_q, o_kv, o_gate, o_qk, o_v, o_o, o_i, o_f = [int(v) for v in np.cumsum((0, 512, 768, 24, 1024, 512, 512, 4))]

    def groups(a):
        kv = a[..., o_kv:o_gate].reshape(a.shape[:-1] + (3, 2, NSA_KV_HEADS, NSA_HEAD_DIM))
        kv = jnp.swapaxes(kv, -3, -2)
        kv = kv.reshape(a.shape[:-1] + (768,))
        small = jnp.concatenate(
            [a[..., o_gate:o_qk], a[..., o_i:o_f], a[..., o_f:o_f + 4],
             jnp.zeros(a.shape[:-1] + (LANES - 32,), a.dtype)], axis=-1)
        return a[..., o_q:o_kv], kv, small, a[..., o_qk:o_i]

    return tuple(w.astype(BF16) for w in groups(w_in)), jnp.concatenate(groups(b_in))[None, :].astype(F32)


def _inproj(x2, g, sc, sh, w, b, conv_w, conv_b, norm_g, seq):
    n, d = x2.shape
    tm = min(2 * ROW_TILE, seq)
    lc = min(MLSTM_CHUNK, tm)
    per_b = seq // tm
    wm = MLSTM_HEADS * MLSTM_HEAD_DIM
    row = lambda i: (i, 0)
    bat = lambda i: (i // per_b, 0, 0)
    fix = lambda i: (0, 0)
    outs = (
        jax.ShapeDtypeStruct((n, 512), BF16),
        jax.ShapeDtypeStruct((6, n, LANES), BF16),
        jax.ShapeDtypeStruct((n, 128), F32),
        jax.ShapeDtypeStruct((n, wm), BF16),
    )
    return pl.pallas_call(
        functools.partial(_inproj_kernel, per_b=per_b, lc=lc),
        grid=(n // tm,),
        in_specs=[
            pl.BlockSpec((tm, d), row),
            pl.BlockSpec((1, d), fix),
            pl.BlockSpec((1, 1, d), bat),
            pl.BlockSpec((1, 1, d), bat),
            *[pl.BlockSpec(wi.shape, fix) for wi in w],
            pl.BlockSpec(b.shape, fix),
            pl.BlockSpec(conv_w.shape, fix),
            pl.BlockSpec((1, conv_w.shape[1]), fix),
            pl.BlockSpec((1, wm), fix),
        ],
        out_specs=(
            pl.BlockSpec((tm, 512), row),
            pl.BlockSpec((6, tm, LANES), lambda i: (0, i, 0)),
            pl.BlockSpec((tm, 128), row),
            pl.BlockSpec((tm, wm), row),
        ),
        out_shape=outs,
        scratch_shapes=[
            pltpu.VMEM((tm, wm), BF16), pltpu.VMEM((tm, wm), BF16),
            pltpu.VMEM((tm, wm), BF16), pltpu.VMEM((tm, wm), F32),
            pltpu.VMEM((SUBLANES, 2 * wm), F32),
            pltpu.VMEM((MLSTM_HEADS, MLSTM_HEAD_DIM, MLSTM_HEAD_DIM), F32),
            pltpu.VMEM((SUBLANES, MLSTM_HEAD_DIM), F32),
            pltpu.VMEM((SUBLANES, LANES), F32),
        ],
        compiler_params=_cparams(("arbitrary",)),
        name="inproj_mlstm",
    )(x2, g, sc, sh, *w, b, conv_w, conv_b[None, :], norm_g[None, :])


def _compress_kernel(ch_ref, pe_ref, w1_ref, w2_ref, o_ref):
    ch = ch_ref[0].astype(F32)
    a0 = (ch + pe_ref[0:1, :]).astype(BF16)
    a1 = (ch + pe_ref[1:2, :]).astype(BF16)
    half = ch.shape[1]
    h0 = _dot(a0, w1_ref[0:half, :])
    h1 = _dot(a1, w1_ref[half:2 * half, :])
    n_chunk = ch.shape[0]
    hid = h0 + pltpu.roll(h1, n_chunk - 1, axis=0)
    o_ref[0] = _dot(_silu(hid).astype(BF16), w2_ref[...]).astype(BF16)


def _compress_weights(cmp_pe, cmp_w1, cmp_w2):
    dh, hid = NSA_HEAD_DIM, CMP_HIDDEN
    pe = jnp.concatenate([cmp_pe[0], cmp_pe[1]], axis=-1)
    pe = pe.reshape(2, CMP_STRIDE * LANES)
    w1 = cmp_w1.reshape(2, CMP_LEN, dh, hid)
    z = jnp.zeros((CMP_LEN, dh, hid), cmp_w1.dtype)
    wk = jnp.concatenate([w1[0], z], axis=1)
    wv = jnp.concatenate([z, w1[1]], axis=1)
    w1p = jnp.concatenate([wk, wv], axis=2).reshape(CMP_LEN * LANES, 2 * hid)
    z2 = jnp.zeros((hid, dh), cmp_w2.dtype)
    w2p = jnp.concatenate([jnp.concatenate([cmp_w2[0], z2], axis=1),
                           jnp.concatenate([z2, cmp_w2[1]], axis=1)], axis=0)
    return pe.astype(F32), w1p.astype(BF16), w2p.astype(BF16)


def _compress(kv_cmp, pe, w1p, w2p, batch, seq):
    g = kv_cmp.shape[0]
    n_chunk = seq // CMP_STRIDE
    ch = kv_cmp.reshape(g * batch, n_chunk, CMP_STRIDE * LANES)
    return pl.pallas_call(
        _compress_kernel,
        grid=(g * batch,),
        in_specs=[
            pl.BlockSpec((1, n_chunk, CMP_STRIDE * LANES), lambda i: (i, 0, 0)),
            pl.BlockSpec(pe.shape, lambda i: (0, 0)),
            pl.BlockSpec(w1p.shape, lambda i: (0, 0)),
            pl.BlockSpec(w2p.shape, lambda i: (0, 0)),
        ],
        out_specs=pl.BlockSpec((1, n_chunk, LANES), lambda i: (i, 0, 0)),
        out_shape=jax.ShapeDtypeStruct((g * batch, n_chunk, LANES), BF16),
        compiler_params=_cparams(("parallel",)),
        name="nsa_compress",
    )(ch, pe, w1p, w2p)


def _nsa2_kernel(slope_ref, q_ref, kc_ref, ks_ref, kw_ref, gate_ref, ovt_ref, feat_ref, featc_ref,
                 o_ref, kaug_c, vt_c, kaug_s, vt_s, kaug_w, vt_w, todo_scr, *, seq):
    qb = pl.program_id(1)
    n_cmp = seq // CMP_STRIDE
    n_sel = seq // SEL_BLOCK
    n_top = min(SEL_TOP, n_sel)
    n_chunk = seq // SEL_CHUNK
    dh = NSA_HEAD_DIM
    cols = NSA_REP * Q_BLOCK
    group_w = NSA_REP * dh
    blocks_per_chunk = SEL_CHUNK // SEL_BLOCK

    @pl.when(qb == 0)
    def _():
        def build(src_ref, f_ref, kaug, vt, n_rows):
            step = min(SEL_CHUNK, n_rows)
            lane = lax.broadcasted_iota(I32, (step, LANES), 1)
            ones_row = jnp.where(lax.broadcasted_iota(I32, (VT_HEAD, step), 0) == 0, 1.0, 0.0)
            for c0 in range(0, n_rows, step):
                x = src_ref[c0:c0 + step, :].astype(F32)
                kaug[c0:c0 + step, :] = jnp.where(lane < dh, x, f_ref[c0:c0 + step, :].astype(F32)).astype(BF16)
                vt[:, c0:c0 + step] = jnp.concatenate([ones_row, x.T[dh:]], axis=0).astype(BF16)

        for g in range(NSA_KV_HEADS):
            build(kc_ref.at[g, 0], featc_ref, kaug_c.at[g], vt_c.at[g], n_cmp)
            build(ks_ref.at[g], feat_ref, kaug_s.at[g], vt_s.at[g], seq)
            build(kw_ref.at[g], feat_ref, kaug_w.at[g], vt_w.at[g], seq)

    lane = lax.broadcasted_iota(I32, (Q_BLOCK, LANES), 1)
    blk_f = (lane - dh).astype(F32)
    t_lane = qb * Q_BLOCK + lax.broadcasted_iota(I32, (1, Q_BLOCK), 1)
    last = (qb * Q_BLOCK) // SEL_CHUNK

    def tile4(x):
        return jnp.concatenate([x] * NSA_REP, axis=1)

    def sel_scores(g, q_sel, c, size=SEL_CHUNK):
        start = pl.multiple_of(c * SEL_CHUNK, SEL_CHUNK)
        return _dot_nt(kaug_s[g, pl.ds(start, size), :], q_sel)

    def sel_update(g, c, s, carry):
        m, acc = carry
        start = pl.multiple_of(c * SEL_CHUNK, SEL_CHUNK)
        m_new = jnp.maximum(m, jnp.max(s, axis=0, keepdims=True))
        p = jnp.exp(s - m_new).astype(BF16)
        acc = jnp.exp(m - m_new) * acc + _dot(vt_s[g, :, pl.ds(start, s.shape[0])], p)
        return m_new, acc

    def front_scores(g):
        qf = q_ref[:, g * group_w:(g + 1) * group_w].astype(F32)
        parts = []
        for r in range(NSA_REP):
            pair = qf[:, (r // 2) * LANES:(r // 2 + 1) * LANES]
            if r % 2:
                pair = pltpu.roll(pair, dh, axis=1)
            slope = slope_ref[g * NSA_REP + r]
            parts.append(jnp.where(lane < dh, pair, jnp.where(lane == dh, slope, slope * SEL_BLOCK * blk_f)))
        q_all_f = jnp.concatenate(parts, axis=0)
        q_all = q_all_f.astype(BF16)

        s = _dot_nt(kaug_c[g], q_all)
        end_c = lax.broadcasted_iota(I32, (n_cmp, Q_BLOCK), 0) * CMP_STRIDE + (CMP_LEN - 1)
        return q_all_f, q_all, s + tile4(jnp.where(end_c <= t_lane, 0.0, NEG_INF))

    def front(g, q_all_f, q_all, s):
        e = jnp.exp(s - jnp.max(s, axis=0, keepdims=True))
        any_valid = tile4((t_lane >= CMP_LEN - 1).astype(F32))
        p_c = e * (any_valid / jnp.sum(e, axis=0, keepdims=True))
        o_c = _dot(vt_c[g], p_c.astype(BF16))

        p4 = p_c[:, 0:Q_BLOCK]
        for r in range(1, NSA_REP):
            p4 = p4 + p_c[:, r * Q_BLOCK:(r + 1) * Q_BLOCK]
        ovt = ovt_ref[...]
        imp = sum(_dot(ovt, piece) for piece in _split3(p4))
        j_col = lax.broadcasted_iota(I32, (n_sel, 1), 0)
        cur = jnp.right_shift(t_lane, SEL_SHIFT)
        forced = (j_col == 0) | (j_col == cur) | (j_col == cur - 1)
        imp = jnp.where(forced, imp + FORCE_BONUS, imp)
        imp = jnp.where(j_col <= cur, imp, -1.0)
        groups = [imp[v * SUBLANES:(v + 1) * SUBLANES, :] for v in range(n_sel // SUBLANES)]
        j_grp = lax.broadcasted_iota(I32, (SUBLANES, Q_BLOCK), 0)
        ranks = [jnp.zeros((SUBLANES, Q_BLOCK), F32) for _ in groups]
        for k in range(n_sel):
            row_k = groups[k // SUBLANES][k % SUBLANES:k % SUBLANES + 1, :]
            for v, grp in enumerate(groups):
                ge = jnp.where(row_k >= grp, 1.0, 0.0)
                gt = jnp.where(row_k > grp, 1.0, 0.0)
                if v * SUBLANES > k:
                    inc = ge
                elif (v + 1) * SUBLANES - 1 < k:
                    inc = gt
                else:
                    inc = jnp.where(j_grp + v * SUBLANES > k, ge, gt)
                ranks[v] = ranks[v] + inc
        sel_t = jnp.where(jnp.concatenate(ranks, axis=0) < n_top, 1.0, 0.0)
        pieces = [jnp.zeros((dh, Q_BLOCK), F32), sel_t]
        if n_sel < dh:
            pieces.append(jnp.zeros((dh - n_sel, Q_BLOCK), F32))
        selmat = jnp.concatenate(pieces, axis=0).T
        drop = jnp.concatenate([jnp.where(lane > dh, selmat, 1.0)] * NSA_REP, axis=0) < 0.5
        q_sel = jnp.where(drop, NEG_INF, q_all_f).astype(BF16)

        n_todo = jnp.int32(0)
        for c in range(n_chunk - 1):
            rows_c = sel_t[c * blocks_per_chunk:(c + 1) * blocks_per_chunk, :]
            wanted = (jnp.max(rows_c) > 0.5) & (c < last)
            todo_scr[g * n_chunk + n_todo] = jnp.int32(c)
            n_todo = n_todo + wanted.astype(I32)

        span = Q_BLOCK + WINDOW
        start = pl.multiple_of(jnp.maximum(qb * Q_BLOCK - WINDOW, 0), Q_BLOCK)
        s = _dot_nt(kaug_w[g, pl.ds(start, span), :], q_all)
        dist = t_lane - (start + lax.broadcasted_iota(I32, (span, Q_BLOCK), 0))
        in_band = pltpu.bitcast(dist, jnp.uint32) < WINDOW
        s = s + tile4(jnp.where(in_band, 0.0, NEG_INF))
        p = jnp.exp(s - jnp.max(s, axis=0, keepdims=True)).astype(BF16)
        acc = _dot(vt_w[g, :, pl.ds(start, span)], p)
        return q_sel, n_todo, o_c, acc / acc[0:1, :]

    def middle(g, q_sel, n_todo):
        def pair(i, carry):
            c0, c1 = todo_scr[g * n_chunk + 2 * i], todo_scr[g * n_chunk + 2 * i + 1]
            s0, s1 = sel_scores(g, q_sel, c0), sel_scores(g, q_sel, c1)
            return sel_update(g, c1, s1, sel_update(g, c0, s0, carry))

        def single(k, carry):
            c = todo_scr[g * n_chunk + k]
            return sel_update(g, c, sel_scores(g, q_sel, c), carry)

        init = (jnp.full((1, cols), NEG_INF, F32), jnp.zeros((VT_HEAD + dh, cols), F32))
        carry = lax.fori_loop(0, n_todo // 2, pair, init)
        return lax.fori_loop(2 * (n_todo // 2), n_todo, single, carry)

    def back_scores(g, q_sel, n_keys):
        pos = last * SEL_CHUNK + lax.broadcasted_iota(I32, (n_keys, Q_BLOCK), 0)
        return sel_scores(g, q_sel, last, n_keys) + tile4(jnp.where(pos <= t_lane, 0.0, NEG_INF))

    def back(g, s, carry, o_c, o_w):
        _, acc = sel_update(g, last, s, carry)
        o_s = acc / acc[0:1, :]
        gates = _sigmoid(gate_ref[0, g])
        mixed = []
        for r in range(NSA_REP):
            cs = slice(r * Q_BLOCK, (r + 1) * Q_BLOCK)
            mixed.append(gates[3 * r:3 * r + 1, :] * o_c[VT_HEAD:, cs]
                         + gates[3 * r + 1:3 * r + 2, :] * o_s[VT_HEAD:, cs]
                         + gates[3 * r + 2:3 * r + 3, :] * o_w[VT_HEAD:, cs])
        for pr in range(NSA_REP // 2):
            pair_t = jnp.concatenate([mixed[2 * pr], mixed[2 * pr + 1]], axis=0)
            lo = g * group_w + pr * LANES
            o_ref[:, lo:lo + LANES] = pair_t.T.astype(o_ref.dtype)

    starts = [front_scores(g) for g in range(NSA_KV_HEADS)]
    fronts = [front(g, *starts[g]) for g in range(NSA_KV_HEADS)]
    carries = [middle(g, fronts[g][0], fronts[g][1]) for g in range(NSA_KV_HEADS)]
    per_chunk = SEL_CHUNK // Q_BLOCK
    for v in range(per_chunk):
        @pl.when(qb % per_chunk == v)
        def _():
            scores = [back_scores(g, fronts[g][0], (v + 1) * Q_BLOCK) for g in range(NSA_KV_HEADS)]
            for g in range(NSA_KV_HEADS):
                _, _, o_c, o_w = fronts[g]
                back(g, scores[g], carries[g], o_c, o_w)


def _nsa_constants(seq):
    n_cmp, n_sel = seq // CMP_STRIDE, seq // SEL_BLOCK
    assert n_sel <= NSA_HEAD_DIM, "one feature lane per selection block"
    slopes = 2.0 ** (-8.0 * np.arange(1, NSA_HEADS + 1) / NSA_HEADS)
    far = (slopes[:, None] * SEL_BLOCK * np.arange(n_sel)[None, :]).astype(np.float32)
    assert np.array_equal(far.astype(BF16).astype(np.float32), far), "ALiBi features must be exact in bf16"
    lo_c = np.arange(n_cmp)[:, None] * CMP_STRIDE
    lo_s = np.arange(n_sel)[None, :] * SEL_BLOCK
    ov = np.clip(np.minimum(lo_c + CMP_LEN, lo_s + SEL_BLOCK) - np.maximum(lo_c, lo_s), 0, None) / CMP_LEN
    ov[n_cmp - 1] = 0.0

    def feats(pos):
        f = np.zeros((pos.shape[0], LANES), np.float32)
        f[:, NSA_HEAD_DIM] = pos % SEL_BLOCK
        blk = pos // SEL_BLOCK
        ok = (blk >= 1) & (blk < NSA_HEAD_DIM)
        f[np.nonzero(ok)[0], NSA_HEAD_DIM + blk[ok]] = 1.0
        return f

    feat = feats(np.arange(seq))
    pos_c = np.arange(n_cmp) * CMP_STRIDE + CMP_LEN - 1
    featc = feats(pos_c)
    featc[pos_c >= seq] = 0.0
    return (jnp.asarray(slopes, F32), jnp.asarray(ov.T, BF16), jnp.asarray(feat, BF16), jnp.asarray(featc, BF16))


def _nsa2(q, kvp, kc, gates_t, batch, seq):
    n, width = q.shape
    g = NSA_KV_HEADS
    n_qb = seq // Q_BLOCK
    n_cmp = seq // CMP_STRIDE
    vt_rows = VT_HEAD + NSA_HEAD_DIM
    slopes, ovt, feat, featc = _nsa_constants(seq)
    fix = lambda b, i, s: (0, 0)
    branch = lambda br: (lambda b, i, s: (br, b, 0))
    grid_spec = pltpu.PrefetchScalarGridSpec(
        num_scalar_prefetch=1,
        grid=(batch, n_qb),
        in_specs=[
            pl.BlockSpec((Q_BLOCK, width), lambda b, i, s: (b * n_qb + i, 0)),
            pl.BlockSpec((g, 1, n_cmp, LANES), lambda b, i, s: (0, b, 0, 0)),
            pl.BlockSpec((g, seq, LANES), branch(1)),
            pl.BlockSpec((g, seq, LANES), branch(2)),
            pl.BlockSpec((1, g, 16, Q_BLOCK), lambda b, i, s: (b, 0, 0, i)),
            pl.BlockSpec(ovt.shape, fix),
            pl.BlockSpec(feat.shape, fix),
            pl.BlockSpec(featc.shape, fix),
        ],
        out_specs=pl.BlockSpec((Q_BLOCK, width), lambda b, i, s: (b * n_qb + i, 0)),
        scratch_shapes=[
            pltpu.VMEM((g, n_cmp, LANES), BF16), pltpu.VMEM((g, vt_rows, n_cmp), BF16),
            pltpu.VMEM((g, seq, LANES), BF16), pltpu.VMEM((g, vt_rows, seq), BF16),
            pltpu.VMEM((g, seq, LANES), BF16), pltpu.VMEM((g, vt_rows, seq), BF16),
            pltpu.SMEM((g * (seq // SEL_CHUNK),), I32),
        ],
    )
    return pl.pallas_call(
        functools.partial(_nsa2_kernel, seq=seq),
        grid_spec=grid_spec,
        out_shape=jax.ShapeDtypeStruct((n, width), BF16),
        compiler_params=_cparams(("parallel", "arbitrary")),
        name="nsa_attention",
    )(slopes, q, kc.reshape(g, batch, n_cmp, LANES), kvp, kvp,
      gates_t.reshape(batch, g, 16, seq), ovt, feat, featc)


def _mlstm_chunk(rows, q_ref, k_ref, v_ref, o_ref, sm_ref, g_ref, y_ref, c_scr, n_scr, m_scr):
    lc = rows.stop - rows.start
    d = MLSTM_HEAD_DIM
    sm = sm_ref[rows, :]
    lf = _log_sigmoid(sm)
    ri = lax.broadcasted_iota(I32, (lc, lc), 0)
    ci = lax.broadcasted_iota(I32, (lc, lc), 1)
    tri = jnp.where(ri >= ci, 1.0, 0.0).astype(BF16)
    a_col = sum(_dot(tri, piece) for piece in _split3(lf))
    a_row = a_col.T

    def head(h):
        hs = slice(h * d, (h + 1) * d)
        a_j = a_row[SM_F + h:SM_F + h + 1, :]
        gap_s = sm[:, SM_I + h:SM_I + h + 1] - a_col[:, SM_F + h:SM_F + h + 1]
        m_prev = m_scr[h:h + 1, 0:1]
        qh, kh, vh = q_ref[rows, hs], k_ref[rows, hs], v_ref[rows, hs]
        v_t = vh.astype(F32).T.astype(BF16)
        c_prev = c_scr[h]
        n_prev = n_scr[h:h + 1, :]

        log_d = jnp.where(ri <= ci, a_j + gap_s, -jnp.inf)
        m_inter = a_j + m_prev
        m_t = jnp.maximum(m_inter, jnp.max(log_d, axis=0, keepdims=True))
        d_mat = jnp.exp(log_d - m_t)
        inter = jnp.exp(m_inter - m_t)
        yield
        s_qk = _dot_nt(kh, qh) * d_mat
        yield
        num = inter * _dot_nt(c_prev.astype(BF16), qh) + _dot(v_t, s_qk.astype(BF16))
        n_rows = jnp.broadcast_to(n_prev, (SUBLANES, d)).astype(BF16)
        den = inter * _dot_nt(n_rows, qh)[0:1, :] + jnp.sum(s_qk, axis=0, keepdims=True)
        hh = (num / jnp.maximum(jnp.abs(den), jnp.exp(-m_t))).T
        yield

        a_last = a_j[:, lc - 1:lc]
        log_w = a_last + gap_s
        m_new = jnp.maximum(a_last + m_prev, jnp.max(log_w, axis=0, keepdims=True))
        wk = jnp.exp(log_w - m_new) * kh.astype(F32)
        decay = jnp.exp(a_last + m_prev - m_new)
        c_scr[h] = decay * c_prev + _dot(v_t, wk.astype(BF16))
        n_scr[h:h + 1, :] = decay * n_prev + jnp.sum(wk, axis=0, keepdims=True)
        m_scr[h:h + 1, :] = jnp.broadcast_to(m_new, (1, LANES))
        yield

        hg = _sigmoid(o_ref[rows, hs]) * hh
        hn = hg * lax.rsqrt(jnp.mean(hg * hg, axis=-1, keepdims=True) + EPS)
        y_ref[rows, hs] = (hn * g_ref[:, hs]).astype(y_ref.dtype)

    heads = [head(h) for h in range(MLSTM_HEADS)]
    for _ in range(5):
        for gen in heads:
            next(gen, None)


def _mix_out(x_ref, ya_ref, yb_ref, w_ref, g1_ref, gn_ref, sc_ref, sh_ref):
    half = ya_ref.shape[1]
    y = _dot(ya_ref[...], w_ref[0:half, :]) + _dot(yb_ref[...], w_ref[half:2 * half, :])
    x1 = x_ref[...] + g1_ref[0] * y
    return x1, _rmsnorm_mod(x1, gn_ref[...], sc_ref[0], sh_ref[0])


def _mix_specs(x2, ya, yb, w_out, tm, per_b):
    d = x2.shape[1]
    row = lambda i: (i, 0)
    bat = lambda i: (i // per_b, 0, 0)
    fix = lambda i: (0, 0)
    return [
        pl.BlockSpec((tm, d), row),
        pl.BlockSpec((tm, ya.shape[1]), row),
        pl.BlockSpec((tm, yb.shape[1]), row),
        pl.BlockSpec(w_out.shape, fix),
        pl.BlockSpec((1, 1, d), bat),
        pl.BlockSpec((1, d), fix),
        pl.BlockSpec((1, 1, d), bat),
        pl.BlockSpec((1, 1, d), bat),
    ]


def _outproj_router_kernel(x_ref, ya_ref, yb_ref, w_ref, g1_ref, gn_ref, sc_ref, sh_ref, rw_ref,
                           x1_ref, h_ref, lg_ref):
    x1, h = _mix_out(x_ref, ya_ref, yb_ref, w_ref, g1_ref, gn_ref, sc_ref, sh_ref)
    x1_ref[...] = x1
    h_ref[...] = h
    lg_ref[...] = lax.dot_general(rw_ref[...], h, (((1,), (1,)), ((), ())),
                                  precision=lax.Precision.HIGHEST, preferred_element_type=F32)


def _outproj_router(x2, ya, yb, w_out, g1, gn, sc, sh, seq, router_wt):
    n, d = x2.shape
    tm = min(ROW_TILE, seq)
    per_b = seq // tm
    row = lambda i: (i, 0)
    return pl.pallas_call(
        _outproj_router_kernel,
        grid=(n // tm,),
        in_specs=_mix_specs(x2, ya, yb, w_out, tm, per_b) + [pl.BlockSpec(router_wt.shape, lambda i: (0, 0))],
        out_specs=(pl.BlockSpec((tm, d), row), pl.BlockSpec((tm, d), row),
                   pl.BlockSpec((N_EXPERTS, tm), lambda i: (0, i))),
        out_shape=(jax.ShapeDtypeStruct((n, d), F32), jax.ShapeDtypeStruct((n, d), F32),
                   jax.ShapeDtypeStruct((N_EXPERTS, n), F32)),
        compiler_params=_cparams(("parallel",)),
        name="outproj_router",
    )(x2, ya, yb, w_out, g1, gn, sc, sh, router_wt)


def _outproj_ffn_kernel(x_ref, ya_ref, yb_ref, w_ref, g1_ref, gn_ref, sc_ref, sh_ref,
                        wg_ref, wu_ref, wd_ref, g2_ref, o_ref, act_scr, *, tf):
    x1, h = _mix_out(x_ref, ya_ref, yb_ref, w_ref, g1_ref, gn_ref, sc_ref, sh_ref)
    h = h.astype(BF16)
    d_ff = wg_ref.shape[1]
    for j in range(d_ff // tf):
        cs = slice(j * tf, (j + 1) * tf)
        act_scr[:, cs] = (_silu(_dot(h, wg_ref[:, cs])) * _dot(h, wu_ref[:, cs])).astype(BF16)
    o_ref[...] = x1 + g2_ref[0] * _dot(act_scr[...], wd_ref[...])


def _outproj_ffn(x2, ya, yb, w_out, g1, gn, sc, sh, wg, wu, wd, g2, seq):
    n, d = x2.shape
    d_ff = wg.shape[1]
    tm = min(ROW_TILE, seq)
    per_b = seq // tm
    tf = 2 * LANES
    row = lambda i: (i, 0)
    fix = lambda i: (0, 0)
    once = dict(pipeline_mode=pl.Buffered(1))
    return pl.pallas_call(
        functools.partial(_outproj_ffn_kernel, tf=tf),
        grid=(n // tm,),
        in_specs=_mix_specs(x2, ya, yb, w_out, tm, per_b) + [
            pl.BlockSpec(wg.shape, fix, **once),
            pl.BlockSpec(wu.shape, fix, **once),
            pl.BlockSpec(wd.shape, fix, **once),
            pl.BlockSpec((1, 1, d), lambda i: (i // per_b, 0, 0)),
        ],
        out_specs=pl.BlockSpec((tm, d), row),
        out_shape=jax.ShapeDtypeStruct((n, d), F32),
        scratch_shapes=[pltpu.VMEM((tm, d_ff), BF16)],
        compiler_params=_cparams(("parallel",)),
        name="outproj_ffn",
    )(x2, ya, yb, w_out, g1, gn, sc, sh, wg, wu, wd, g2)


def _route_kernel(lg_ref, dest_ref, wt_ref, meta_ref, cnt_scr, exc_scr, *, tile):
    n_e, n = lg_ref.shape
    lg = lg_ref[...]
    e_iota = lax.broadcasted_iota(I32, (n_e, n), 0)
    m1 = jnp.max(lg, axis=0, keepdims=True)
    e0 = jnp.min(jnp.where(lg == m1, e_iota, n_e), axis=0, keepdims=True)
    lg2 = jnp.where(e_iota == e0, -jnp.inf, lg)
    m2 = jnp.max(lg2, axis=0, keepdims=True)
    e1 = jnp.min(jnp.where(lg2 == m2, e_iota, n_e), axis=0, keepdims=True)
    ex = jnp.exp(m2 - m1)
    wt_ref[0:1, :] = 1.0 / (1.0 + ex)
    wt_ref[1:2, :] = ex / (1.0 + ex)
    oh0 = e_iota == e0
    oh1 = e_iota == e1
    cnt_scr[...] = jnp.where(oh0, 1.0, 0.0) + jnp.where(oh1, 1.0, 0.0)

    ri = lax.broadcasted_iota(I32, (LANES, 2 * LANES), 0)
    ci = lax.broadcasted_iota(I32, (LANES, 2 * LANES), 1)
    prefix_total = jnp.where((ci >= LANES) | (ri < ci), 1.0, 0.0).astype(BF16)

    def block(kb, carry):
        ls = pl.ds(pl.multiple_of(kb * LANES, LANES), LANES)
        both = _dot(cnt_scr[:, ls].astype(BF16), prefix_total)
        exc_scr[:, ls] = both[:, :LANES] + carry
        return carry + both[:, LANES:]

    total = lax.fori_loop(0, n // LANES, block, jnp.zeros((n_e, LANES), F32))
    padded = jnp.floor((total + (tile - 1)) / tile) * tile
    e_col = lax.broadcasted_iota(I32, (n_e, LANES), 0)
    starts = jnp.zeros((n_e, LANES), F32)
    for e in range(n_e - 1):
        starts = starts + jnp.where(e_col > e, padded[e:e + 1, :], 0.0)
    ends = starts + padded
    slot = starts[:, 0:1] + exc_scr[...]
    dest_ref[0:1, :] = jnp.sum(jnp.where(oh0, slot, 0.0), axis=0, keepdims=True).astype(I32)
    dest_ref[1:2, :] = jnp.sum(jnp.where(oh1, slot, 0.0), axis=0, keepdims=True).astype(I32)
    blk_start = (lax.broadcasted_iota(I32, (n_e, LANES), 1) * tile).astype(F32)
    blk_exp = jnp.sum(jnp.where(ends <= blk_start, 1.0, 0.0), axis=0, keepdims=True)
    meta_ref[0:1, :] = jnp.minimum(blk_exp, n_e - 1.0).astype(I32)
    meta_ref[1:2, :] = (ends[n_e - 1:n_e, :] / tile).astype(I32)
    on_diag = lax.broadcasted_iota(I32, (n_e, LANES), 1) == e_col
    meta_ref[2:3, :] = jnp.sum(jnp.where(on_diag, ends / tile, 0.0), axis=0, keepdims=True).astype(I32)
    meta_ref[3:4, :] = jnp.sum(jnp.where(on_diag, starts / tile, 0.0), axis=0, keepdims=True).astype(I32)
    meta_ref[4:SUBLANES, :] = jnp.zeros((SUBLANES - 4, LANES), I32)


def _route(logits_t, tile):
    n_e, n = logits_t.shape
    return pl.pallas_call(
        functools.partial(_route_kernel, tile=tile),
        out_shape=(jax.ShapeDtypeStruct((2, n), I32), jax.ShapeDtypeStruct((2, n), F32),
                   jax.ShapeDtypeStruct((SUBLANES, LANES), I32)),
        scratch_shapes=[pltpu.VMEM((n_e, n), F32), pltpu.VMEM((n_e, n), F32)],
        compiler_params=pltpu.CompilerParams(vmem_limit_bytes=VMEM_LIMIT),
        name="moe_route",
    )(logits_t)


def _dispatch_kernel(dest_ref, meta_ref, h_ref, wg_ref, wu_ref, wd_ref,
                     xs_ref, wgo_ref, wuo_ref, wdo_ref, zero_scr, sem, zsem, *, tile):
    i = pl.program_id(0)
    td = h_ref.shape[0]

    @pl.when(i == 0)
    def _():
        zero_scr[...] = jnp.zeros_like(zero_scr)
        n_blk = xs_ref.shape[0] // tile

        def zero_block(b):
            return pltpu.make_async_copy(zero_scr, xs_ref.at[pl.ds(pl.multiple_of(b * tile, tile), tile)], zsem)

        def each_block(fn):
            for e in range(N_EXPERTS):
                @pl.when(meta_ref[2, e] > meta_ref[3, e])
                def _():
                    fn(zero_block(meta_ref[2, e] - 1))

            def tail(b, _):
                fn(zero_block(b))
                return 0
            lax.fori_loop(meta_ref[1, 0], n_blk, tail, 0)

        each_block(lambda c: c.start())
        each_block(lambda c: c.wait())

    def copy(r, k):
        row = dest_ref[k * (pl.num_programs(0) * td) + i * td + r]
        return pltpu.make_async_copy(h_ref.at[pl.ds(r, 1)], xs_ref.at[pl.ds(row, 1)], sem)

    def start(r, _):
        copy(r, 0).start(priority=0)
        copy(r, 1).start(priority=1)
        return 0

    lax.fori_loop(0, td, start, 0, unroll=DMA_UNROLL)
    wgo_ref[...] = wg_ref[...].astype(BF16)
    wuo_ref[...] = wu_ref[...].astype(BF16)
    wdo_ref[...] = wd_ref[...].astype(BF16)
    for _ in range(2):
        pltpu.make_async_copy(h_ref, xs_ref.at[pl.ds(0, td)], sem).wait()


def _dispatch(dest, meta, h, n_rows, tile, weights):
    n, d = h.shape
    td = min(GATHER_TILE, n)
    n_steps = n // td
    flat = [w.reshape(-1, w.shape[-1]) for w in weights]
    slabs = [w.shape[0] // n_steps for w in flat]
    assert all(w.shape[0] == sl * n_steps and sl % (2 * SUBLANES) == 0 for w, sl in zip(flat, slabs))
    w_specs = [pl.BlockSpec((sl, w.shape[1]), lambda i, s, m: (i, 0)) for w, sl in zip(flat, slabs)]
    grid_spec = pltpu.PrefetchScalarGridSpec(
        num_scalar_prefetch=2,
        grid=(n_steps,),
        in_specs=[pl.BlockSpec((td, d), lambda i, s, m: (i, 0))] + w_specs,
        out_specs=[pl.BlockSpec(memory_space=pl.ANY)] + w_specs,
        scratch_shapes=[pltpu.VMEM((tile, d), h.dtype), pltpu.SemaphoreType.DMA(()),
                        pltpu.SemaphoreType.DMA(())],
    )
    outs = pl.pallas_call(
        functools.partial(_dispatch_kernel, tile=tile),
        grid_spec=grid_spec,
        out_shape=[jax.ShapeDtypeStruct((n_rows, d), h.dtype)]
        + [jax.ShapeDtypeStruct(w.shape, BF16) for w in flat],
        compiler_params=_cparams(("arbitrary",)),
        name="moe_dispatch",
    )(dest.reshape(-1), meta, h, *flat)
    return outs[0], [o.reshape(w.shape) for o, w in zip(outs[1:], weights)]


def _expert_kernel(meta_ref, x_ref, wg_ref, wu_ref, wd_ref, y_ref, xb_scr):
    i = pl.program_id(0)
    j = pl.program_id(1)

    @pl.when(i < meta_ref[1, 0])
    def _():
        @pl.when(j == 0)
        def _():
            xb_scr[...] = x_ref[...].astype(BF16)

        xb = xb_scr[...]
        act = (_silu(_dot(xb, wg_ref[0])) * _dot(xb, wu_ref[0])).astype(BF16)
        part = _dot(act, wd_ref[0])

        @pl.when(j == 0)
        def _():
            y_ref[...] = part

        @pl.when(j > 0)
        def _():
            y_ref[...] += part

    @pl.when((i >= meta_ref[1, 0]) & (j == 0))
    def _():
        y_ref[...] = jnp.zeros_like(y_ref)


def _experts(meta, xs, wg, wu, wd, tile):
    n_rows, d = xs.shape
    d_ff = wg.shape[2]
    tf = d_ff // 2 if (d_ff // 2) % (2 * LANES) == 0 else d_ff
    n_blk = n_rows // tile

    def blk(i, s):
        return jnp.minimum(i, s[1, 0] - 1)

    grid_spec = pltpu.PrefetchScalarGridSpec(
        num_scalar_prefetch=1,
        grid=(n_blk, d_ff // tf),
        in_specs=[
            pl.BlockSpec((tile, d), lambda i, j, s: (blk(i, s), 0)),
            pl.BlockSpec((1, d, tf), lambda i, j, s: (s[0, blk(i, s)], 0, jnp.where(i < s[1, 0], j, 0))),
            pl.BlockSpec((1, d, tf), lambda i, j, s: (s[0, blk(i, s)], 0, jnp.where(i < s[1, 0], j, 0))),
            pl.BlockSpec((1, tf, d), lambda i, j, s: (s[0, blk(i, s)], jnp.where(i < s[1, 0], j, 0), 0)),
        ],
        out_specs=pl.BlockSpec((tile, d), lambda i, j, s: (i, 0)),
        scratch_shapes=[pltpu.VMEM((tile, d), BF16)],
    )
    return pl.pallas_call(
        _expert_kernel,
        grid_spec=grid_spec,
        out_shape=jax.ShapeDtypeStruct((n_rows, d), F32),
        compiler_params=_cparams(("arbitrary", "arbitrary")),
        name="moe_experts",
    )(meta, xs, wg, wu, wd)


def _combine_kernel(dest_ref, y_ref, x_ref, wt_ref, g2_ref, gn_ref, o_ref, buf, sem):
    i = pl.program_id(0)
    n_steps = pl.num_programs(0)
    tc = x_ref.shape[0]
    slot = i % 2

    def gather(step, to_slot):
        def start(r, _):
            for k in range(2):
                row = dest_ref[k * (n_steps * tc) + step * tc + r]
                pltpu.make_async_copy(y_ref.at[pl.ds(row, 1)], buf.at[to_slot, k, pl.ds(r, 1)],
                                      sem.at[to_slot]).start(priority=k)
            return 0
        lax.fori_loop(0, tc, start, 0, unroll=DMA_UNROLL)

    @pl.when(i == 0)
    def _():
        gather(0, 0)

    @pl.when(i + 1 < n_steps)
    def _():
        gather(i + 1, 1 - slot)

    for k in range(2):
        pltpu.make_async_copy(y_ref.at[pl.ds(0, tc)], buf.at[slot, k], sem.at[slot]).wait()
    wt = wt_ref[...]
    y = buf[slot, 0] * wt[:, 0:1] + buf[slot, 1] * wt[:, 1:2]
    x = x_ref[...] + g2_ref[0] * y
    ms = jnp.mean(x * x, axis=-1, keepdims=True)
    o_ref[...] = x * lax.rsqrt(ms + EPS) * gn_ref[...]


def _combine(dest, y, x1, wt, g2, gn, seq):
    n, d = x1.shape
    tc = min(GATHER_TILE, seq)
    per_b = seq // tc
    grid_spec = pltpu.PrefetchScalarGridSpec(
        num_scalar_prefetch=1,
        grid=(n // tc,),
        in_specs=[
            pl.BlockSpec(memory_space=pl.ANY),
            pl.BlockSpec((tc, d), lambda i, s: (i, 0)),
            pl.BlockSpec((tc, 2), lambda i, s: (i, 0)),
            pl.BlockSpec((1, 1, d), lambda i, s: (i // per_b, 0, 0)),
            pl.BlockSpec((1, d), lambda i, s: (0, 0)),
        ],
        out_specs=pl.BlockSpec((tc, d), lambda i, s: (i, 0)),
        scratch_shapes=[pltpu.VMEM((2, 2, tc, d), F32), pltpu.SemaphoreType.DMA((2,))],
    )
    return pl.pallas_call(
        _combine_kernel,
        grid_spec=grid_spec,
        out_shape=jax.ShapeDtypeStruct((n, d), F32),
        compiler_params=_cparams(("arbitrary",)),
        name="moe_combine_norm",
    )(dest.reshape(-1), y, x1, wt, g2, gn)


def _final_norm_kernel(x_ref, g_ref, o_ref):
    x = x_ref[...]
    ms = jnp.mean(x * x, axis=-1, keepdims=True)
    o_ref[...] = x * lax.rsqrt(ms + EPS) * g_ref[...]


def _mixer(x2, mod, l, batch, seq, norm_mix_g, w_in, b_in, cmp_pe, cmp_w1, cmp_w2,
           conv_w, conv_b, mlstm_norm_g):
    d = x2.shape[1]
    sh1, sc1 = mod[l, :, 0:d], mod[l, :, d:2 * d]
    w, b = _inproj_weights(w_in[l], b_in[l])
    q, kvp, sm, y_ml = _inproj(x2, norm_mix_g[l][None, :], sc1[:, None, :], sh1[:, None, :], w, b,
                               conv_w[l], conv_b[l], mlstm_norm_g[l], seq)
    pe, w1p, w2p = _compress_weights(cmp_pe[l], cmp_w1[l], cmp_w2[l])
    kc = _compress(kvp[0:NSA_KV_HEADS], pe, w1p, w2p, batch, seq)
    gates = sm[:, SM_GATE:SM_GATE + 24].reshape(batch, seq, NSA_KV_HEADS, NSA_REP * 3)
    gates = jnp.pad(gates.transpose(0, 2, 3, 1), ((0, 0), (0, 0), (0, 16 - NSA_REP * 3), (0, 0)))
    y_nsa = _nsa2(q, kvp, kc, gates.reshape(batch * NSA_KV_HEADS, 16, seq), batch, seq)
    return y_nsa, y_ml


def kernel(x, c, ada_w, ada_b, norm_mix_g, norm_ffn_g, w_in, b_in, cmp_pe, cmp_w1, cmp_w2, conv_w, conv_b, mlstm_norm_g, w_out, ffn_w_gate, ffn_w_up, ffn_w_down, router_w, moe_w_gate, moe_w_up, moe_w_down, final_norm_g):
    batch, seq, d = x.shape
    depth = ada_w.shape[0]
    n = batch * seq
    mod = _adaln(c, ada_w, ada_b)
    x2 = x.reshape(n, d)
    for l in range(depth):
        g1 = mod[l, :, 2 * d:3 * d][:, None, :]
        sh2 = mod[l, :, 3 * d:4 * d][:, None, :]
        sc2 = mod[l, :, 4 * d:5 * d][:, None, :]
        g2 = mod[l, :, 5 * d:6 * d][:, None, :]
        y_nsa, y_ml = _mixer(x2, mod, l, batch, seq, norm_mix_g, w_in, b_in, cmp_pe, cmp_w1, cmp_w2,
                             conv_w, conv_b, mlstm_norm_g)
        gn = norm_ffn_g[l][None, :]
        i = l // 2
        last = l == depth - 1
        if l % 2 == 0:
            x2 = _outproj_ffn(x2, y_nsa, y_ml, w_out[l].astype(BF16), g1, gn, sc2, sh2,
                              ffn_w_gate[i].astype(BF16), ffn_w_up[i].astype(BF16),
                              ffn_w_down[i].astype(BF16), g2, seq)
            if last:
                x2 = _final_norm(x2, final_norm_g)
        else:
            x1, h, logits_t = _outproj_router(x2, y_nsa, y_ml, w_out[l].astype(BF16), g1, gn, sc2, sh2, seq,
                                              router_w[i].T)
            n_rows = 2 * n + N_EXPERTS * MOE_TILE
            dest, wt, meta = _route(logits_t, MOE_TILE)
            xs, (wg, wu, wd) = _dispatch(dest, meta, h, n_rows, MOE_TILE,
                                         (moe_w_gate[i], moe_w_up[i], moe_w_down[i]))
            y = _experts(meta, xs, wg, wu, wd, MOE_TILE)
            unit = jnp.ones((1, d), F32)
            x2 = _combine(dest, y, x1, wt.T, g2, final_norm_g[None, :] if last else unit, seq)
            if not last:
                raise NotImplementedError("a MoE layer that is not the last layer")
    return x2.reshape(batch, seq, d)


def _final_norm(x2, g):
    n, d = x2.shape
    tm = min(ROW_TILE, n)
    return pl.pallas_call(
        _final_norm_kernel,
        grid=(n // tm,),
        in_specs=[pl.BlockSpec((tm, d), lambda i: (i, 0)), pl.BlockSpec((1, d), lambda i: (0, 0))],
        out_specs=pl.BlockSpec((tm, d), lambda i: (i, 0)),
        out_shape=jax.ShapeDtypeStruct((n, d), F32),
        compiler_params=_cparams(("parallel",)),
        name="final_norm",
    )(x2, g[None, :])
```
